```python
import jax, jax.numpy as jnp
from jax import lax
import numpy as np

D_MODEL = 1024
BATCH = 8
SEQ = 2048
DEPTH = 4

CTX_LEN = 256
GRID_W = 64

N_HEADS = 8
Q_LORA = 384
KV_LORA = 256
QK_NOPE = 64
QK_ROPE = 32
V_DIM = 64
QK_DIM = QK_NOPE + QK_ROPE
ROPE_BASE = 10000.0
Q_BLOCK = 128
F_GROUPS = 4
F_GROUP_W = 128
F_WIDTH = F_GROUPS * F_GROUP_W
RNN_WIDTH = 512
RNN_HEADS = 8
RNN_BLOCK = RNN_WIDTH // RNN_HEADS
RNN_CONV = 4
LRU_C = 8.0
D_FF = 2816
FFN_CONV = 3
N_BRANCH = 3
N_MOD = 6
EPS = 1e-6

IN_SIZES = (Q_LORA, KV_LORA, QK_ROPE, F_WIDTH, RNN_WIDTH, RNN_WIDTH, N_BRANCH * D_MODEL)
D_IN = Q_LORA + KV_LORA + QK_ROPE + F_WIDTH + 2 * RNN_WIDTH + N_BRANCH * D_MODEL

kernel_name = 'hybrid_mla_fnet_rglru_prefix_dit'


def rms_norm(x, g):
    xf = x.astype(jnp.float32)
    y = xf * lax.rsqrt(jnp.mean(xf * xf, axis=-1, keepdims=True) + EPS)
    return (y * g.astype(jnp.float32)).astype(x.dtype)


def modulate(h, shift, scale):
    return h * (1 + scale) + shift


def split_in(p):
    idx = np.cumsum(IN_SIZES)[:-1].tolist()
    return jnp.split(p, idx, axis=-1)


def axial_rope(n_rows, dtype):
    row = jnp.repeat(jnp.arange(n_rows, dtype=jnp.float32), GRID_W)
    col = jnp.tile(jnp.arange(GRID_W, dtype=jnp.float32), n_rows)
    n_freq = QK_ROPE // 4
    inv = ROPE_BASE ** (-jnp.arange(n_freq, dtype=jnp.float32) / n_freq)
    ang = jnp.concatenate([row[:, None] * inv, col[:, None] * inv], axis=-1)
    return jnp.cos(ang).astype(dtype), jnp.sin(ang).astype(dtype)


def apply_rope(x, cos, sin):
    half = x.shape[-1] // 2
    x1, x2 = x[..., :half], x[..., half:]
    return jnp.concatenate([x1 * cos - x2 * sin, x2 * cos + x1 * sin], axis=-1)


def mla_q(cq, q_norm, w_uq, rope):
    q = jnp.einsum('bld,dhe->blhe', rms_norm(cq, q_norm), w_uq)
    q_nope, q_rope = q[..., :QK_NOPE], q[..., QK_NOPE:]
    if rope is not None:
        q_rope = apply_rope(q_rope, rope[0][:, None, :], rope[1][:, None, :])
    return jnp.concatenate([q_nope, q_rope], axis=-1)


def mla_kv(ckv, kr, kv_norm, w_ukv, rope):
    kv = jnp.einsum('bld,dhe->blhe', rms_norm(ckv, kv_norm), w_ukv)
    k_nope, v = kv[..., :QK_NOPE], kv[..., QK_NOPE:]
    if rope is not None:
        kr = apply_rope(kr, rope[0], rope[1])
    k_rope = jnp.broadcast_to(kr[:, :, None, :], k_nope.shape[:-1] + (QK_ROPE,))
    return jnp.concatenate([k_nope, k_rope], axis=-1), v


def attend(q, k, v):
    s = jnp.einsum('bqhe,bkhe->bhqk', q, k).astype(jnp.float32) * (QK_DIM ** -0.5)
    p = jax.nn.softmax(s, axis=-1).astype(v.dtype)
    return jnp.einsum('bhqk,bkhv->bqhv', p, v)


def blocked_attention(q, k, v):
    B, L, H, E = q.shape
    nb = L // Q_BLOCK
    qb = q.reshape(B, nb, Q_BLOCK, H, E).transpose(1, 0, 2, 3, 4)
    out = lax.map(lambda qi: attend(qi, k, v), qb)
    return out.transpose(1, 0, 2, 3, 4).reshape(B, L, H, v.shape[-1])


def fourier_mix(u):
    B, L, _ = u.shape
    ug = u.reshape(B, L, F_GROUPS, F_GROUP_W).astype(jnp.float32)
    y = jnp.fft.fftn(ug, axes=(1, 3), norm='ortho').real
    return y.reshape(B, L, F_WIDTH).astype(u.dtype)


def depthwise_conv(x, w, b, pad_left):
    K = w.shape[0]
    L = x.shape[1]
    xp = jnp.pad(x, ((0, 0), (pad_left, K - 1 - pad_left), (0, 0)))
    y = b + w[0] * xp[:, 0:L]
    for k in range(1, K):
        y = y + w[k] * xp[:, k:k + L]
    return y


def _lin_combine(left, right):
    a1, b1 = left
    a2, b2 = right
    return a1 * a2, a2 * b1 + b2


def rglru_scan(u, params, h0, reverse):
    w_a, b_a, w_x, b_x, lam = params
    B, L, R = u.shape
    ub = u.reshape(B, L, RNN_HEADS, RNN_BLOCK)
    r = jax.nn.sigmoid((jnp.einsum('blhi,hij->blhj', ub, w_a).reshape(B, L, R) + b_a).astype(jnp.float32))
    gi = jax.nn.sigmoid((jnp.einsum('blhi,hij->blhj', ub, w_x).reshape(B, L, R) + b_x).astype(jnp.float32))
    log_a = -LRU_C * r * jax.nn.softplus(-lam.astype(jnp.float32))
    a = jnp.exp(log_a)
    b = jnp.sqrt(-jnp.expm1(2.0 * log_a)) * gi * u.astype(jnp.float32)
    A, Bc = lax.associative_scan(_lin_combine, (a, b), axis=1, reverse=reverse)
    return A * h0[:, None, :] + Bc


def merge(y_att, y_f, y_rnn, gate_logits, w_out):
    g = jax.nn.sigmoid(gate_logits.astype(jnp.float32)).astype(y_att.dtype)
    g_att, g_f, g_rnn = jnp.split(g, N_BRANCH, axis=-1)
    return (g_att * y_att + g_f * y_f + g_rnn * y_rnn) @ w_out


def conv_ffn(h, w_up, cw, cb, w_down):
    u = depthwise_conv(h @ w_up, cw, cb, FFN_CONV // 2)
    up, gate = jnp.split(u, 2, axis=-1)
    return (jax.nn.gelu(gate) * up) @ w_down


def setup_inputs(seed: int = 0) -> dict:
    key = jax.random.key(seed)
    ks = jax.random.split(key, 40)
    f32 = jnp.float32
    L, D = DEPTH, D_MODEL

    def nrm(k, shape, scale):
        return jax.random.normal(k, shape, f32) * scale

    def gain(k, shape):
        return 1.0 + 0.02 * jax.random.normal(k, shape, f32)

    u = jax.random.uniform(ks[20], (L, 2, RNN_WIDTH), f32, 0.9, 0.999)
    s = u ** (1.0 / LRU_C)
    rg_lambda = jnp.log(s) - jnp.log1p(-s)
    return {
        'x': nrm(ks[0], (BATCH, SEQ, D), 1.0),
        'c': nrm(ks[1], (BATCH, D), 1.0),
        'ctx': nrm(ks[2], (BATCH, CTX_LEN, D), 1.0),
        'c_ctx': nrm(ks[3], (D,), 1.0),
        'w_ada': nrm(ks[4], (L, D, N_MOD * D), 0.02),
        'b_ada': nrm(ks[5], (L, N_MOD * D), 0.02),
        'norm_mix': gain(ks[6], (L, D)),
        'norm_ffn': gain(ks[7], (L, D)),
        'w_in': nrm(ks[8], (L, D, D_IN), D ** -0.5),
        'q_norm': gain(ks[9], (L, Q_LORA)),
        'kv_norm': gain(ks[10], (L, KV_LORA)),
        'w_uq': nrm(ks[11], (L, Q_LORA, N_HEADS, QK_DIM), Q_LORA ** -0.5),
        'w_ukv': nrm(ks[12], (L, KV_LORA, N_HEADS, QK_NOPE + V_DIM), KV_LORA ** -0.5),
        'w_o_attn': nrm(ks[13], (L, N_HEADS * V_DIM, D), (N_HEADS * V_DIM) ** -0.5),
        'w_o_fourier': nrm(ks[14], (L, F_WIDTH, D), F_WIDTH ** -0.5),
        'rnn_conv_w': nrm(ks[15], (L, RNN_CONV, RNN_WIDTH), RNN_CONV ** -0.5),
        'rnn_conv_b': nrm(ks[16], (L, RNN_WIDTH), 0.02),
        'rg_w_a': nrm(ks[17], (L, 2, RNN_HEADS, RNN_BLOCK, RNN_BLOCK), RNN_BLOCK ** -0.5),
        'rg_b_a': nrm(ks[18], (L, 2, RNN_WIDTH), 0.02),
        'rg_w_x': nrm(ks[19], (L, 2, RNN_HEADS, RNN_BLOCK, RNN_BLOCK), RNN_BLOCK ** -0.5),
        'rg_b_x': nrm(ks[21], (L, 2, RNN_WIDTH), 0.02),
        'rg_lambda': rg_lambda,
        'w_o_rnn': nrm(ks[22], (L, RNN_WIDTH, D), RNN_WIDTH ** -0.5),
        'w_out': nrm(ks[23], (L, D, D), D ** -0.5),
        'w_up': nrm(ks[24], (L, D, 2 * D_FF), D ** -0.5),
        'ffn_conv_w': nrm(ks[25], (L, FFN_CONV, 2 * D_FF), FFN_CONV ** -0.5),
        'ffn_conv_b': nrm(ks[26], (L, 2 * D_FF), 0.02),
        'w_down': nrm(ks[27], (L, D_FF, D), D_FF ** -0.5),
        'final_norm': gain(ks[28], (D,)),
    }


def reference(x, c, ctx, c_ctx, w_ada, b_ada, norm_mix, norm_ffn, w_in, q_norm, kv_norm, w_uq, w_ukv,
              w_o_attn, w_o_fourier, rnn_conv_w, rnn_conv_b, rg_w_a, rg_b_a, rg_w_x, rg_b_x, rg_lambda,
              w_o_rnn, w_out, w_up, ffn_conv_w, ffn_conv_b, w_down, final_norm):
    B, S, _ = x.shape
    C = ctx.shape[1]
    dt = x.dtype
    n_rows = S // GRID_W
    rope = axial_rope(n_rows, dt)
    silu_c = jax.nn.silu(c)
    silu_cc = jax.nn.silu(c_ctx)
    h0_zero = jnp.zeros((B, RNN_WIDTH), jnp.float32)
    for l in range(DEPTH):
        last = l == DEPTH - 1
        mx = jnp.split((silu_c @ w_ada[l] + b_ada[l])[:, None, :], N_MOD, axis=-1)
        mc = jnp.split(silu_cc @ w_ada[l] + b_ada[l], N_MOD, axis=-1)

        hx = modulate(rms_norm(x, norm_mix[l]), mx[0], mx[1])
        hc = modulate(rms_norm(ctx, norm_mix[l]), mc[0], mc[1])
        cq_x, ckv_x, kr_x, uf_x, ur_x, ug_x, gl_x = split_in(hx @ w_in[l])
        cq_c, ckv_c, kr_c, uf_c, ur_c, ug_c, gl_c = split_in(hc @ w_in[l])

        k_c, v_c = mla_kv(ckv_c, kr_c, kv_norm[l], w_ukv[l], None)
        k_x, v_x = mla_kv(ckv_x, kr_x, kv_norm[l], w_ukv[l], rope)
        q_x = mla_q(cq_x, q_norm[l], w_uq[l], rope)
        att_x = blocked_attention(q_x, jnp.concatenate([k_c, k_x], axis=1), jnp.concatenate([v_c, v_x], axis=1))

        dir_f = (rg_w_a[l, 0], rg_b_a[l, 0], rg_w_x[l, 0], rg_b_x[l, 0], rg_lambda[l, 0])
        dir_b = (rg_w_a[l, 1], rg_b_a[l, 1], rg_w_x[l, 1], rg_b_x[l, 1], rg_lambda[l, 1])
        uc = depthwise_conv(ur_c, rnn_conv_w[l], rnn_conv_b[l], RNN_CONV // 2)
        h_cf = rglru_scan(uc, dir_f, h0_zero, False)
        h_cb = rglru_scan(uc, dir_b, h0_zero, True)
        ux = depthwise_conv(ur_x, rnn_conv_w[l], rnn_conv_b[l], RNN_CONV // 2)
        h_xf = rglru_scan(ux, dir_f, h_cf[:, -1], False)
        h_xb = rglru_scan(ux, dir_b, h_cb[:, 0], True)

        y_x = merge(att_x.reshape(B, S, N_HEADS * V_DIM) @ w_o_attn[l],
                    fourier_mix(uf_x) @ w_o_fourier[l],
                    ((h_xf + h_xb).astype(dt) * jax.nn.gelu(ug_x)) @ w_o_rnn[l],
                    gl_x, w_out[l])
        x = x + mx[2] * y_x
        x = x + mx[5] * conv_ffn(modulate(rms_norm(x, norm_ffn[l]), mx[3], mx[4]),
                                 w_up[l], ffn_conv_w[l], ffn_conv_b[l], w_down[l])

        if not last:
            q_c = mla_q(cq_c, q_norm[l], w_uq[l], None)
            att_c = attend(q_c, k_c, v_c)
            y_c = merge(att_c.reshape(B, C, N_HEADS * V_DIM) @ w_o_attn[l],
                        fourier_mix(uf_c) @ w_o_fourier[l],
                        ((h_cf + h_cb).astype(dt) * jax.nn.gelu(ug_c)) @ w_o_rnn[l],
                        gl_c, w_out[l])
            ctx = ctx + mc[2] * y_c
            ctx = ctx + mc[5] * conv_ffn(modulate(rms_norm(ctx, norm_ffn[l]), mc[3], mc[4]),
                                         w_up[l], ffn_conv_w[l], ffn_conv_b[l], w_down[l])
    return rms_norm(x, final_norm)
```

```python
import functools
import math

import jax
import jax.numpy as jnp
import numpy as np
from jax import lax
from jax.experimental import pallas as pl
from jax.experimental.pallas import tpu as pltpu

F32 = jnp.float32
BF16 = jnp.bfloat16

D_MODEL = 1024
GRID_W = 64
N_HEADS = 8
Q_LORA = 384
KV_LORA = 256
QK_NOPE = 64
QK_ROPE = 32
V_DIM = 64
QK_DIM = QK_NOPE + QK_ROPE
ROPE_BASE = 10000.0
F_GROUPS = 4
F_GROUP_W = 128
F_WIDTH = F_GROUPS * F_GROUP_W
RNN_WIDTH = 512
RNN_HEADS = 8
RNN_BLOCK = RNN_WIDTH // RNN_HEADS
RNN_CONV = 4
LRU_C = 8.0
D_FF = 2816
FFN_CONV = 3
N_MOD = 6
EPS = 1e-6

LANES = 128
SUBLANES = 8
HEAD_PAD = 128
MOD_ROWS = 16

_O_CQ = 0
_O_CKV = _O_CQ + Q_LORA
_O_KR = _O_CKV + KV_LORA
_O_UF = _O_KR + 2 * HEAD_PAD
_O_UR = _O_UF + F_WIDTH
_O_UG = _O_UR + RNN_WIDTH
_O_GL = _O_UG + RNN_WIDTH
_D_IN_PAD = _O_GL + 3 * D_MODEL

_VMEM_LIMIT = 56 * 2 ** 20


def _cparams(sem, vmem=_VMEM_LIMIT):
    return pltpu.CompilerParams(dimension_semantics=sem, vmem_limit_bytes=vmem)


def _resident(shape):
    nd = len(shape)
    return pl.BlockSpec(shape, lambda *_: (0,) * nd, pipeline_mode=pl.Buffered(1))


def _dot(a, b):
    return jnp.dot(a, b, preferred_element_type=F32)


def _sigmoid(x):
    return 0.5 * (1.0 + jnp.tanh(0.5 * x))


def _gelu(x):
    return 0.5 * x * (1.0 + jnp.tanh(math.sqrt(2.0 / math.pi) * (x + 0.044715 * (x * x * x))))


def _rms(x, g):
    return x * lax.rsqrt(jnp.mean(x * x, axis=-1, keepdims=True) + EPS) * g


def _ada_kernel(cc_ref, w_ref, b_ref, o_ref):
    cc = cc_ref[...]
    s = cc * _sigmoid(cc)
    o_ref[...] = _dot(s.astype(BF16), w_ref[...].astype(BF16)) + b_ref[...]


def _ada_call(cc, w_ada, b_ada):
    depth, d, n = w_ada.shape
    tn = 1536
    return pl.pallas_call(
        _ada_kernel,
        grid=(depth, n // tn),
        in_specs=[
            pl.BlockSpec((MOD_ROWS, d), lambda l, j: (0, 0)),
            pl.BlockSpec((None, d, tn), lambda l, j: (l, 0, j)),
            pl.BlockSpec((None, 1, tn), lambda l, j: (l, 0, j)),
        ],
        out_specs=pl.BlockSpec((None, MOD_ROWS, tn), lambda l, j: (l, 0, j)),
        out_shape=jax.ShapeDtypeStruct((depth, MOD_ROWS, n), F32),
        compiler_params=_cparams(("arbitrary", "arbitrary")),
        name="adaln",
    )(cc, w_ada, b_ada.reshape(depth, 1, n))


def _in_kernel(x_ref, g_ref, shift_ref, scale_ref, w_ref, qn_ref, kvn_ref, wuq_ref, wukv_ref,
               cq_ref, sq_ref, ck_ref, sk_ref,
               q_out, k_out, v_out, uf_out, ur_out, gug_out, g_out):
    h = _rms(x_ref[...], g_ref[...]) * (1.0 + scale_ref[...]) + shift_ref[...]
    hb = h.astype(BF16)
    hp = N_HEADS * HEAD_PAD

    cqn = _rms(_dot(hb, w_ref[:, _O_CQ:_O_CQ + Q_LORA]), qn_ref[...]).astype(BF16)
    y = _dot(cqn, wuq_ref[...])
    cq_t = cq_ref[...]
    sq_t = sq_ref[...]
    for hd in range(N_HEADS):
        a = hd * HEAD_PAD
        q_out[:, a:a + HEAD_PAD] = (y[:, a:a + HEAD_PAD] * cq_t + y[:, hp + a:hp + a + HEAD_PAD] * sq_t).astype(BF16)

    ckvn = _rms(_dot(hb, w_ref[:, _O_CKV:_O_CKV + KV_LORA]), kvn_ref[...]).astype(BF16)
    kv = _dot(ckvn, wukv_ref[...])
    kr = _dot(hb, w_ref[:, _O_KR:_O_KR + 2 * HEAD_PAD])
    k_rope = kr[:, :HEAD_PAD] * ck_ref[...] + kr[:, HEAD_PAD:] * sk_ref[...]
    for hd in range(N_HEADS):
        a = hd * HEAD_PAD
        k_out[:, a:a + HEAD_PAD] = (kv[:, a:a + HEAD_PAD] + k_rope).astype(BF16)
    v_out[...] = kv[:, hp:].astype(BF16)

    uf_out[...] = _dot(hb, w_ref[:, _O_UF:_O_UF + F_WIDTH]).astype(BF16)
    ur_out[...] = _dot(hb, w_ref[:, _O_UR:_O_UR + RNN_WIDTH])
    gug_out[...] = _gelu(_dot(hb, w_ref[:, _O_UG:_O_UG + RNN_WIDTH])).astype(BF16)
    for j in range(3):
        a = j * D_MODEL
        g_out[:, a:a + D_MODEL] = _sigmoid(_dot(hb, w_ref[:, _O_GL + a:_O_GL + a + D_MODEL])).astype(BF16)


def _in_call(x2d, mod_l, norm_g, w_in_p, qn, kvn, wuq, wukv, tabs, seq_len, mod_row, tm):
    rows, d = x2d.shape
    tiles_per_seq = seq_len // tm
    hp = N_HEADS * HEAD_PAD

    def row_spec(width):
        return pl.BlockSpec((tm, width), lambda i: (i, 0))

    def mod_spec(chunk):
        return pl.BlockSpec((None, None, 1, d), lambda i: (mod_row(i), chunk, 0, 0))

    tab_spec = pl.BlockSpec((tm, HEAD_PAD), lambda i: (i % tiles_per_seq, 0))
    out_shapes = (
        jax.ShapeDtypeStruct((rows, hp), BF16),
        jax.ShapeDtypeStruct((rows, hp), BF16),
        jax.ShapeDtypeStruct((rows, hp), BF16),
        jax.ShapeDtypeStruct((rows, F_WIDTH), BF16),
        jax.ShapeDtypeStruct((rows, RNN_WIDTH), F32),
        jax.ShapeDtypeStruct((rows, RNN_WIDTH), BF16),
        jax.ShapeDtypeStruct((rows, 3 * d), BF16),
    )
    return pl.pallas_call(
        _in_kernel,
        grid=(rows // tm,),
        in_specs=[
            row_spec(d),
            _resident((1, d)),
            mod_spec(0),
            mod_spec(1),
            _resident(w_in_p.shape),
            _resident((1, Q_LORA)),
            _resident((1, KV_LORA)),
            _resident(wuq.shape),
            _resident(wukv.shape),
            tab_spec, tab_spec, tab_spec, tab_spec,
        ],
        out_specs=tuple(row_spec(s.shape[1]) for s in out_shapes),
        out_shape=out_shapes,
        compiler_params=_cparams(("arbitrary",)),
        name="in_proj",
    )(x2d, norm_g, mod_l, mod_l, w_in_p, qn, kvn, wuq, wukv, *tabs)


def _attn_kernel(*refs, n_seg):
    q_ref = refs[0]
    k_refs = refs[1:1 + n_seg]
    v_refs = refs[1 + n_seg:1 + 2 * n_seg]
    o_ref = refs[1 + 2 * n_seg]
    nt = (((1,), (1,)), ((), ()))
    for pair in range(N_HEADS // 2):
        acc = None
        for hd in (2 * pair, 2 * pair + 1):
            a = hd * HEAD_PAD
            qh = q_ref[:, a:a + HEAD_PAD]
            s = [lax.dot_general(qh, k[:, a:a + HEAD_PAD], nt, preferred_element_type=F32) for k in k_refs]
            m = functools.reduce(jnp.maximum, [jnp.max(si, axis=-1, keepdims=True) for si in s])
            p = [jnp.exp(si - m) for si in s]
            denom = functools.reduce(jnp.add, [jnp.sum(pi, axis=-1, keepdims=True) for pi in p])
            o = functools.reduce(jnp.add, [_dot(pi.astype(BF16), v[:, a:a + HEAD_PAD]) for pi, v in zip(p, v_refs)])
            o = o * (1.0 / denom)
            acc = o if acc is None else acc + o
        o_ref[:, pair * HEAD_PAD:(pair + 1) * HEAD_PAD] = acc.astype(BF16)


def _attn_call(q, segs, q_len, tq):
    rows, hp = q.shape
    tiles = q_len // tq
    n_seg = len(segs)
    k_specs = [pl.BlockSpec((ln, hp), lambda i: (i // tiles, 0)) for _, _, ln in segs]
    return pl.pallas_call(
        functools.partial(_attn_kernel, n_seg=n_seg),
        grid=(rows // tq,),
        in_specs=[pl.BlockSpec((tq, hp), lambda i: (i, 0))] + k_specs + k_specs,
        out_specs=pl.BlockSpec((tq, N_HEADS * V_DIM), lambda i: (i, 0)),
        out_shape=jax.ShapeDtypeStruct((rows, N_HEADS * V_DIM), BF16),
        compiler_params=_cparams(("arbitrary",)),
        name="attention",
    )(q, *[k for k, _, _ in segs], *[v for _, v, _ in segs])


def _fourier_kernel(u_ref, cs_ref, dft_ref, o_ref, ab_ref, *, seq_len):
    @pl.when(pl.program_id(1) == 0)
    def _():
        ab = _dot(u_ref[...], cs_ref[...])
        ab_ref[:seq_len, :] = ab[:, :F_WIDTH].astype(BF16)
        ab_ref[seq_len:, :] = ab[:, F_WIDTH:].astype(BF16)

    o_ref[...] = _dot(dft_ref[...], ab_ref[...]).astype(BF16)


def _fourier_call(uf, cs, dft, seq_len, tmf):
    rows = uf.shape[0]
    tiles = seq_len // tmf
    return pl.pallas_call(
        functools.partial(_fourier_kernel, seq_len=seq_len),
        grid=(rows // seq_len, tiles),
        in_specs=[
            pl.BlockSpec((seq_len, F_WIDTH), lambda b, m: (b, 0)),
            _resident(cs.shape),
            pl.BlockSpec((tmf, 2 * seq_len), lambda b, m: (m, 0)),
        ],
        out_specs=pl.BlockSpec((tmf, F_WIDTH), lambda b, m: (b * tiles + m, 0)),
        out_shape=jax.ShapeDtypeStruct((rows, F_WIDTH), BF16),
        scratch_shapes=[pltpu.VMEM((2 * seq_len, F_WIDTH), BF16)],
        compiler_params=_cparams(("arbitrary", "arbitrary")),
        name="fourier",
    )(uf, cs, dft)


_RNN_HALF = RNN_WIDTH // 2
_RNN_CHUNK = 256


def _rnn_kernel(urc_ref, urx_ref, gugc_ref, gugx_ref, cw_ref, cb_ref, wg_ref, bg_ref, lam_ref,
                yc_ref, yx_ref, pad_ref, af_ref, bf_ref, ab_ref, bb_ref, *, ctx_len, seq_len):
    half = _RNN_HALF
    row8 = lax.broadcasted_iota(jnp.int32, (SUBLANES, half), 0)
    zeros8 = jnp.zeros((SUBLANES, half), F32)

    def gates(ur_ref, n, j):
        lo = j * half
        pad_ref[0:SUBLANES, :] = zeros8
        pad_ref[SUBLANES:SUBLANES + n, :] = ur_ref[:, lo:lo + half]
        pad_ref[SUBLANES + n:2 * SUBLANES + n, :] = zeros8
        cw = cw_ref[:, lo:lo + half]
        cb = cb_ref[:, lo:lo + half]
        lam = lam_ref[:, lo:lo + half]
        nsp = -LRU_C * (jnp.maximum(-lam, 0.0) + jnp.log1p(jnp.exp(-jnp.abs(lam))))
        wg = wg_ref[j]
        bg = bg_ref[j]
        ext_rows = _RNN_CHUNK + 2 * SUBLANES

        def chunk(c, carry):
            r0 = pl.multiple_of(c * _RNN_CHUNK, _RNN_CHUNK)
            ext = pad_ref[pl.ds(r0, ext_rows), :]
            sl = slice(SUBLANES, SUBLANES + _RNN_CHUNK)
            uc = (cb + cw[0:1] * pltpu.roll(ext, 2, axis=0)[sl] + cw[1:2] * pltpu.roll(ext, 1, axis=0)[sl]
                  + cw[2:3] * ext[sl] + cw[3:4] * pltpu.roll(ext, ext_rows - 1, axis=0)[sl])
            g = _dot(uc.astype(BF16), wg) + bg
            for d, (a_ref, b_ref) in enumerate(((af_ref, bf_ref), (ab_ref, bb_ref))):
                r = _sigmoid(g[:, (2 * d) * half:(2 * d + 1) * half])
                gi = _sigmoid(g[:, (2 * d + 1) * half:(2 * d + 2) * half])
                log_a = nsp[d:d + 1] * r
                a = jnp.exp(log_a)
                a_ref[pl.ds(r0, _RNN_CHUNK), :] = a
                b_ref[pl.ds(r0, _RNN_CHUNK), :] = jnp.sqrt(-jnp.tanh(log_a) * (1.0 + a * a)) * gi * uc
            return carry

        lax.fori_loop(0, n // _RNN_CHUNK, chunk, 0)

    def scan(n, hf, hb):
        nb = n // SUBLANES

        def block(i, carry):
            hf, hb = carry
            rf = pl.multiple_of(i * SUBLANES, SUBLANES)
            a = af_ref[pl.ds(rf, SUBLANES), :]
            b = bf_ref[pl.ds(rf, SUBLANES), :]
            for d in (1, 2, 4):
                ok = row8 >= d
                b = jnp.where(ok, a * pltpu.roll(b, d, axis=0) + b, b)
                a = jnp.where(ok, a * pltpu.roll(a, d, axis=0), a)
            h = a * hf + b
            bf_ref[pl.ds(rf, SUBLANES), :] = h
            hf = jnp.broadcast_to(h[SUBLANES - 1:SUBLANES, :], h.shape)

            rb = pl.multiple_of((nb - 1 - i) * SUBLANES, SUBLANES)
            a = ab_ref[pl.ds(rb, SUBLANES), :]
            b = bb_ref[pl.ds(rb, SUBLANES), :]
            for d in (1, 2, 4):
                ok = row8 < SUBLANES - d
                b = jnp.where(ok, a * pltpu.roll(b, SUBLANES - d, axis=0) + b, b)
                a = jnp.where(ok, a * pltpu.roll(a, SUBLANES - d, axis=0), a)
            h = a * hb + b
            bb_ref[pl.ds(rb, SUBLANES), :] = h
            hb = jnp.broadcast_to(h[0:1, :], h.shape)
            return hf, hb

        return lax.fori_loop(0, nb, block, (hf, hb))

    for j in range(RNN_WIDTH // half):
        lo = j * half
        gates(urc_ref, ctx_len, j)
        hf, hb = scan(ctx_len, zeros8, zeros8)
        yc_ref[:, lo:lo + half] = ((bf_ref[0:ctx_len, :] + bb_ref[0:ctx_len, :])
                                   * gugc_ref[:, lo:lo + half].astype(F32)).astype(BF16)
        gates(urx_ref, seq_len, j)
        scan(seq_len, hf, hb)
        yx_ref[:, lo:lo + half] = ((bf_ref[0:seq_len, :] + bb_ref[0:seq_len, :])
                                   * gugx_ref[:, lo:lo + half].astype(F32)).astype(BF16)


def _rnn_call(ur_c, ur_x, gug_c, gug_x, cw, cb, wg, bg, lam, ctx_len, seq_len):
    batch = ur_x.shape[0] // seq_len
    w = RNN_WIDTH

    def seq_spec(n):
        return pl.BlockSpec((n, w), lambda b: (b, 0))

    return pl.pallas_call(
        functools.partial(_rnn_kernel, ctx_len=ctx_len, seq_len=seq_len),
        grid=(batch,),
        in_specs=[seq_spec(ctx_len), seq_spec(seq_len), seq_spec(ctx_len), seq_spec(seq_len),
                  _resident(cw.shape), _resident(cb.shape), _resident(wg.shape), _resident(bg.shape),
                  _resident(lam.shape)],
        out_specs=(seq_spec(ctx_len), seq_spec(seq_len)),
        out_shape=(jax.ShapeDtypeStruct((batch * ctx_len, w), BF16),
                   jax.ShapeDtypeStruct((batch * seq_len, w), BF16)),
        scratch_shapes=[pltpu.VMEM((seq_len + 2 * SUBLANES, _RNN_HALF), F32)]
        + [pltpu.VMEM((seq_len, _RNN_HALF), F32)] * 4,
        compiler_params=_cparams(("arbitrary",)),
        name="rglru",
    )(ur_c, ur_x, gug_c, gug_x, cw, cb, wg, bg, lam)


def _merge_kernel(x_ref, att_ref, yf_ref, yr_ref, g_ref, gate_ref, woa_ref, wof_ref, wor_ref, wout_ref, o_ref):
    d = D_MODEL
    m = (g_ref[:, 0:d].astype(F32) * _dot(att_ref[...], woa_ref[...])
         + g_ref[:, d:2 * d].astype(F32) * _dot(yf_ref[...], wof_ref[...])
         + g_ref[:, 2 * d:3 * d].astype(F32) * _dot(yr_ref[...], wor_ref[...]))
    o_ref[...] = x_ref[...] + gate_ref[...] * _dot(m.astype(BF16), wout_ref[...])


def _merge_call(x2d, att, yf, yr, g, mod_l, woa, wof, wor, wout, mod_row, tm):
    rows, d = x2d.shape

    def row_spec(width):
        return pl.BlockSpec((tm, width), lambda i: (i, 0))

    return pl.pallas_call(
        _merge_kernel,
        grid=(rows // tm,),
        in_specs=[row_spec(d), row_spec(att.shape[1]), row_spec(yf.shape[1]), row_spec(yr.shape[1]),
                  row_spec(3 * d),
                  pl.BlockSpec((None, None, 1, d), lambda i: (mod_row(i), 2, 0, 0)),
                  _resident(woa.shape), _resident(wof.shape), _resident(wor.shape), _resident(wout.shape)],
        out_specs=row_spec(d),
        out_shape=jax.ShapeDtypeStruct((rows, d), F32),
        compiler_params=_cparams(("arbitrary",)),
        name="merge",
    )(x2d, att, yf, yr, g, mod_l, woa, wof, wor, wout)


_FF_CHUNK = 256


def _ffn_kernel(x_ref, xp_ref, xn_ref, g_ref, shift_ref, scale_ref, gate_ref,
                wu_ref, wg_ref, cwu_ref, cwg_ref, cbu_ref, cbg_ref, wd_ref, fin_ref,
                o_ref, h_ref, acc_ref, *, tm, tiles_per_seq, final_norm):
    i = pl.program_id(0)
    c = pl.program_id(1)
    ext = tm + 2 * SUBLANES

    @pl.when(c == 0)
    def _():
        def prep(x):
            return (_rms(x, g_ref[...]) * (1.0 + scale_ref[...]) + shift_ref[...]).astype(BF16)

        first = (i % tiles_per_seq) == 0
        last = (i % tiles_per_seq) == tiles_per_seq - 1
        hp = prep(xp_ref[...])
        hn = prep(xn_ref[...])
        h_ref[0:SUBLANES, :] = jnp.where(first, jnp.zeros_like(hp), hp)
        h_ref[SUBLANES:SUBLANES + tm, :] = prep(x_ref[...])
        h_ref[SUBLANES + tm:, :] = jnp.where(last, jnp.zeros_like(hn), hn)
        acc_ref[...] = jnp.zeros_like(acc_ref)

    h = h_ref[...]
    sl = slice(SUBLANES, SUBLANES + tm)

    def conv(u, cw_ref, cb_ref):
        cw = cw_ref[...]
        return (cb_ref[...] + cw[0:1] * pltpu.roll(u, 1, axis=0)[sl] + cw[1:2] * u[sl]
                + cw[2:3] * pltpu.roll(u, ext - 1, axis=0)[sl])

    up = conv(_dot(h, wu_ref[...]), cwu_ref, cbu_ref)
    gt = conv(_dot(h, wg_ref[...]), cwg_ref, cbg_ref)
    acc_ref[...] += _dot((_gelu(gt) * up).astype(BF16), wd_ref[...])

    @pl.when(c == pl.num_programs(1) - 1)
    def _():
        y = x_ref[...] + gate_ref[...] * acc_ref[...]
        if final_norm:
            y = _rms(y, fin_ref[...])
        o_ref[...] = y


def _ffn_call(x2d, mod_l, norm_g, w_up, cw, cb, w_down, fin_g, seq_len, mod_row, tm, final_norm):
    rows, d = x2d.shape
    tiles_per_seq = seq_len // tm
    n_chunks = D_FF // _FF_CHUNK
    blk8 = tm // SUBLANES
    n_blk8 = rows // SUBLANES

    def mod_spec(chunk):
        return pl.BlockSpec((None, None, 1, d), lambda i, c: (mod_row(i), chunk, 0, 0))

    def up_spec(shape, off):
        return pl.BlockSpec(shape, lambda i, c: (0, c + off))

    return pl.pallas_call(
        functools.partial(_ffn_kernel, tm=tm, tiles_per_seq=tiles_per_seq, final_norm=final_norm),
        grid=(rows // tm, n_chunks),
        in_specs=[
            pl.BlockSpec((tm, d), lambda i, c: (i, 0)),
            pl.BlockSpec((SUBLANES, d), lambda i, c: (jnp.maximum(i * blk8 - 1, 0), 0)),
            pl.BlockSpec((SUBLANES, d), lambda i, c: (jnp.minimum((i + 1) * blk8, n_blk8 - 1), 0)),
            _resident((1, d)),
            mod_spec(3), mod_spec(4), mod_spec(5),
            up_spec((d, _FF_CHUNK), 0), up_spec((d, _FF_CHUNK), n_chunks),
            up_spec((FFN_CONV, _FF_CHUNK), 0), up_spec((FFN_CONV, _FF_CHUNK), n_chunks),
            up_spec((1, _FF_CHUNK), 0), up_spec((1, _FF_CHUNK), n_chunks),
            pl.BlockSpec((_FF_CHUNK, d), lambda i, c: (c, 0)),
            _resident((1, d)),
        ],
        out_specs=pl.BlockSpec((tm, d), lambda i, c: (i, 0)),
        out_shape=jax.ShapeDtypeStruct((rows, d), F32),
        scratch_shapes=[pltpu.VMEM((tm + 2 * SUBLANES, d), BF16), pltpu.VMEM((tm, d), F32)],
        compiler_params=_cparams(("arbitrary", "arbitrary")),
        name="conv_ffn",
    )(x2d, x2d, x2d, norm_g, mod_l, mod_l, mod_l, w_up, w_up, cw, cw, cb, cb, w_down, fin_g)


def _rope_tables(seq_len, rope):
    scale = QK_DIM ** -0.5
    if rope:
        n_rows = seq_len // GRID_W
        row = jnp.repeat(jnp.arange(n_rows, dtype=F32), GRID_W)
        col = jnp.tile(jnp.arange(GRID_W, dtype=F32), n_rows)
        n_freq = QK_ROPE // 4
        inv = ROPE_BASE ** (-jnp.arange(n_freq, dtype=F32) / n_freq)
        ang = jnp.concatenate([row[:, None] * inv, col[:, None] * inv], axis=-1)
        cos, sin = jnp.cos(ang), jnp.sin(ang)
    else:
        cos = jnp.ones((seq_len, QK_ROPE // 2), F32)
        sin = jnp.zeros((seq_len, QK_ROPE // 2), F32)
    ones = jnp.ones((seq_len, QK_NOPE), F32)
    zeros = jnp.zeros((seq_len, QK_NOPE), F32)
    tail = jnp.zeros((seq_len, HEAD_PAD - QK_DIM), F32)
    cos2 = jnp.concatenate([cos, cos], axis=-1)
    sin2 = jnp.concatenate([-sin, sin], axis=-1)
    cq = scale * jnp.concatenate([ones, cos2, tail], axis=-1)
    sq = scale * jnp.concatenate([zeros, sin2, tail], axis=-1)
    ck = jnp.concatenate([zeros, cos2, tail], axis=-1)
    sk = jnp.concatenate([zeros, sin2, tail], axis=-1)
    return cq, sq, ck, sk


def _swap_halves(w):
    half = w.shape[-1] // 2
    return jnp.concatenate([w[..., half:], w[..., :half]], axis=-1)


def _prep_w_in(w):
    d = w.shape[0]
    idx = np.cumsum((Q_LORA, KV_LORA, QK_ROPE, F_WIDTH, RNN_WIDTH, RNN_WIDTH)).tolist()
    cq, ckv, kr, uf, ur, ug, gl = jnp.split(w, idx, axis=-1)
    z_lo = jnp.zeros((d, QK_NOPE), w.dtype)
    z_hi = jnp.zeros((d, HEAD_PAD - QK_DIM), w.dtype)
    out = jnp.concatenate([cq, ckv, z_lo, kr, z_hi, z_lo, _swap_halves(kr), z_hi, uf, ur, ug, gl], axis=-1)
    assert out.shape[1] == _D_IN_PAD
    return out.astype(BF16)


def _prep_w_uq(w):
    r, h, _ = w.shape
    pad = jnp.zeros((r, h, HEAD_PAD - QK_DIM), w.dtype)
    y1 = jnp.concatenate([w, pad], axis=-1)
    y2 = jnp.concatenate([jnp.zeros((r, h, QK_NOPE), w.dtype), _swap_halves(w[..., QK_NOPE:]), pad], axis=-1)
    return jnp.concatenate([y1.reshape(r, h * HEAD_PAD), y2.reshape(r, h * HEAD_PAD)], axis=-1).astype(BF16)


def _prep_w_ukv(w):
    r, h, _ = w.shape
    k = jnp.concatenate([w[..., :QK_NOPE], jnp.zeros((r, h, HEAD_PAD - QK_NOPE), w.dtype)], axis=-1)
    v = w[..., QK_NOPE:]
    z = jnp.zeros_like(v)
    odd = (jnp.arange(h) % 2 == 1)[None, :, None]
    vp = jnp.concatenate([jnp.where(odd, z, v), jnp.where(odd, v, z)], axis=-1)
    return jnp.concatenate([k.reshape(r, h * HEAD_PAD), vp.reshape(r, h * HEAD_PAD)], axis=-1).astype(BF16)


def _prep_gates(w_a, b_a, w_x, b_x):
    half = _RNN_HALF
    heads_per_half = half // RNN_BLOCK

    def blockdiag(w):
        eye = jnp.eye(heads_per_half, dtype=w.dtype)
        return jnp.einsum('hij,hk->hikj', w, eye).reshape(half, half)

    ws, bs = [], []
    for j in range(RNN_WIDTH // half):
        hs = slice(j * heads_per_half, (j + 1) * heads_per_half)
        ls = slice(j * half, (j + 1) * half)
        ws.append(jnp.concatenate([blockdiag(w_a[0, hs]), blockdiag(w_x[0, hs]),
                                   blockdiag(w_a[1, hs]), blockdiag(w_x[1, hs])], axis=-1))
        bs.append(jnp.concatenate([b_a[0, ls], b_x[0, ls], b_a[1, ls], b_x[1, ls]], axis=-1)[None, :])
    return jnp.stack(ws).astype(BF16), jnp.stack(bs)


def _dft_tables(seq_len):
    k = jnp.arange(seq_len, dtype=jnp.int32)
    ang = (2.0 * math.pi / seq_len) * ((k[:, None] * k[None, :]) % seq_len).astype(F32)
    s = seq_len ** -0.5
    dft = jnp.concatenate([jnp.cos(ang) * s, jnp.sin(ang) * s], axis=-1).astype(BF16)
    return dft


def _channel_dft():
    k = jnp.arange(F_GROUP_W, dtype=jnp.int32)
    ang = (2.0 * math.pi / F_GROUP_W) * ((k[:, None] * k[None, :]) % F_GROUP_W).astype(F32)
    s = F_GROUP_W ** -0.5
    eye = jnp.eye(F_GROUPS, dtype=F32)
    c = jnp.kron(eye, jnp.cos(ang) * s)
    sn = jnp.kron(eye, jnp.sin(ang) * s)
    return jnp.concatenate([c, -sn], axis=-1).astype(BF16)


def kernel(x, c, ctx, c_ctx, w_ada, b_ada, norm_mix, norm_ffn, w_in, q_norm, kv_norm, w_uq, w_ukv, w_o_attn,
           w_o_fourier, rnn_conv_w, rnn_conv_b, rg_w_a, rg_b_a, rg_w_x, rg_b_x, rg_lambda, w_o_rnn, w_out,
           w_up, ffn_conv_w, ffn_conv_b, w_down, final_norm):
    batch, seq_len, d = x.shape
    ctx_len = ctx.shape[1]
    depth = w_ada.shape[0]
    assert batch + 1 <= MOD_ROWS and d == D_MODEL

    cc = jnp.concatenate([c, c_ctx[None, :], jnp.zeros((MOD_ROWS - batch - 1, d), F32)], axis=0)
    mod = _ada_call(cc, w_ada, b_ada).reshape(depth, MOD_ROWS, N_MOD, 1, d)

    tabs_x = _rope_tables(seq_len, True)
    tabs_c = _rope_tables(ctx_len, False)
    dft_x = _dft_tables(seq_len)
    dft_c = _dft_tables(ctx_len)
    cs = _channel_dft()

    tm_x = 256
    tm_c = min(256, ctx_len)
    tq = 256
    tm_ffn = min(1024, seq_len)
    tiles_x = seq_len // tm_x
    x_row = lambda i: i // tiles_x
    c_row = lambda i: batch
    ffn_tiles_x = seq_len // tm_ffn
    x_row_ffn = lambda i: i // ffn_tiles_x

    xs = x.reshape(batch * seq_len, d)
    cs_tok = ctx.reshape(batch * ctx_len, d)

    for l in range(depth):
        last = l == depth - 1
        mod_l = mod[l]
        w_in_p = _prep_w_in(w_in[l])
        wuq = _prep_w_uq(w_uq[l])
        wukv = _prep_w_ukv(w_ukv[l])
        wg, bg = _prep_gates(rg_w_a[l], rg_b_a[l], rg_w_x[l], rg_b_x[l])
        nm = norm_mix[l][None, :]
        nf = norm_ffn[l][None, :]
        qn = q_norm[l][None, :]
        kvn = kv_norm[l][None, :]
        woa = w_o_attn[l].astype(BF16)
        wof = w_o_fourier[l].astype(BF16)
        wor = w_o_rnn[l].astype(BF16)
        wout = w_out[l].astype(BF16)
        wup = w_up[l].astype(BF16)
        wdn = w_down[l].astype(BF16)
        fcw = ffn_conv_w[l]
        fcb = ffn_conv_b[l][None, :]
        rcw = rnn_conv_w[l]
        rcb = rnn_conv_b[l][None, :]
        fin = final_norm[None, :]

        q_x, k_x, v_x, uf_x, ur_x, gug_x, g_x = _in_call(xs, mod_l, nm, w_in_p, qn, kvn, wuq, wukv, tabs_x,
                                                          seq_len, x_row, tm_x)
        q_c, k_c, v_c, uf_c, ur_c, gug_c, g_c = _in_call(cs_tok, mod_l, nm, w_in_p, qn, kvn, wuq, wukv, tabs_c,
                                                          ctx_len, c_row, tm_c)

        att_x = _attn_call(q_x, [(k_c, v_c, ctx_len), (k_x, v_x, seq_len)], seq_len, tq)
        yf_x = _fourier_call(uf_x, cs, dft_x, seq_len, min(512, seq_len))
        yr_c, yr_x = _rnn_call(ur_c, ur_x, gug_c, gug_x, rcw, rcb, wg, bg, rg_lambda[l], ctx_len, seq_len)

        xs = _merge_call(xs, att_x, yf_x, yr_x, g_x, mod_l, woa, wof, wor, wout, x_row, tm_x)
        xs = _ffn_call(xs, mod_l, nf, wup, fcw, fcb, wdn, fin, seq_len, x_row_ffn, tm_ffn, last)

        if not last:
            att_c = _attn_call(q_c, [(k_c, v_c, ctx_len)], ctx_len, min(tq, ctx_len))
            yf_c = _fourier_call(uf_c, cs, dft_c, ctx_len, ctx_len)
            cs_tok = _merge_call(cs_tok, att_c, yf_c, yr_c, g_c, mod_l, woa, wof, wor, wout, c_row, tm_c)
            cs_tok = _ffn_call(cs_tok, mod_l, nf, wup, fcw, fcb, wdn, fin, ctx_len, c_row, ctx_len, False)

    return xs.reshape(batch, seq_len, d)
```

```python
import functools
import math

import jax
import jax.numpy as jnp
import numpy as np
from jax import lax
from jax.experimental import pallas as pl
from jax.experimental.pallas import tpu as pltpu

F32 = jnp.float32
BF16 = jnp.bfloat16

D_MODEL = 1024
GRID_W = 64
N_HEADS = 8
Q_LORA = 384
KV_LORA = 256
QK_NOPE = 64
QK_ROPE = 32
V_DIM = 64
QK_DIM = QK_NOPE + QK_ROPE
ROPE_BASE = 10000.0
F_GROUPS = 4
F_GROUP_W = 128
F_WIDTH = F_GROUPS * F_GROUP_W
RNN_WIDTH = 512
RNN_HEADS = 8
RNN_BLOCK = RNN_WIDTH // RNN_HEADS
RNN_CONV = 4
LRU_C = 8.0
D_FF = 2816
FFN_CONV = 3
N_MOD = 6
EPS = 1e-6

LANES = 128
SUBLANES = 8
HEAD_PAD = 128
MOD_ROWS = 16

_O_UF = 0
_O_UR = _O_UF + F_WIDTH
_O_UG = _O_UR + RNN_WIDTH
_O_GL = _O_UG + RNN_WIDTH

_VMEM_LIMIT = 56 * 2 ** 20


_INTERLEAVE = None


def _cparams(sem, vmem=_VMEM_LIMIT, flags=None):
    return pltpu.CompilerParams(dimension_semantics=sem, vmem_limit_bytes=vmem, flags=flags)


def _resident(shape):
    nd = len(shape)
    return pl.BlockSpec(shape, lambda *_: (0,) * nd, pipeline_mode=pl.Buffered(1))


def _dot(a, b):
    return jnp.dot(a, b, preferred_element_type=F32)


def _sigmoid(x):
    return 0.5 * (1.0 + jnp.tanh(0.5 * x))


def _gelu(x):
    return 0.5 * x * (1.0 + jnp.tanh(math.sqrt(2.0 / math.pi) * (x + 0.044715 * (x * x * x))))


def _rms(x, g):
    return x * lax.rsqrt(jnp.mean(x * x, axis=-1, keepdims=True) + EPS) * g


def _ada_kernel(cc_ref, w_ref, b_ref, o_ref):
    cc = cc_ref[...]
    s = cc * _sigmoid(cc)
    o_ref[...] = _dot(s.astype(BF16), w_ref[...].astype(BF16)) + b_ref[...]


def _ada_call(cc, w_ada, b_ada):
    depth, d, n = w_ada.shape
    tn = 1536
    return pl.pallas_call(
        _ada_kernel,
        grid=(depth, n // tn),
        in_specs=[
            pl.BlockSpec((MOD_ROWS, d), lambda l, j: (0, 0)),
            pl.BlockSpec((None, d, tn), lambda l, j: (l, 0, j)),
            pl.BlockSpec((None, 1, tn), lambda l, j: (l, 0, j)),
        ],
        out_specs=pl.BlockSpec((None, MOD_ROWS, tn), lambda l, j: (l, 0, j)),
        out_shape=jax.ShapeDtypeStruct((depth, MOD_ROWS, n), F32),
        compiler_params=_cparams(("arbitrary", "arbitrary")),
        name="adaln",
    )(cc, w_ada, b_ada.reshape(depth, 1, n))


def _in_kernel(x_ref, g_ref, shift_ref, scale_ref, wf_ref, wkr_ref, w_ref, qn_ref, kvn_ref, wuq_ref, wukv_ref,
               cq_ref, sq_ref, ck_ref, sk_ref, vone_ref,
               q_out, k_out, v_out, uf_out, ur_out, gug_out, g_out):
    h = _rms(x_ref[...], g_ref[...]) * (1.0 + scale_ref[...]) + shift_ref[...]
    hb = h.astype(BF16)
    hp = N_HEADS * HEAD_PAD

    cqn = _rms(_dot(hb, wf_ref[:, 0:Q_LORA]), qn_ref[...]).astype(BF16)
    y = _dot(cqn, wuq_ref[...])
    cq_t = cq_ref[...]
    sq_t = sq_ref[...]
    for hd in range(N_HEADS):
        a = hd * HEAD_PAD
        q_out[:, a:a + HEAD_PAD] = (y[:, a:a + HEAD_PAD] * cq_t + y[:, hp + a:hp + a + HEAD_PAD] * sq_t).astype(BF16)

    ckvn = _rms(_dot(hb, wf_ref[:, Q_LORA:Q_LORA + KV_LORA]), kvn_ref[...]).astype(BF16)
    kv = _dot(ckvn, wukv_ref[...])
    kr = _dot(hb, wkr_ref[...])
    k_rope = kr[:, :HEAD_PAD] * ck_ref[...] + kr[:, HEAD_PAD:] * sk_ref[...]
    for hd in range(N_HEADS):
        a = hd * HEAD_PAD
        k_out[:, a:a + HEAD_PAD] = (kv[:, a:a + HEAD_PAD] + k_rope).astype(BF16)
    v_out[...] = (kv[:, hp:] + vone_ref[...]).astype(BF16)

    uf_out[...] = _dot(hb, w_ref[:, _O_UF:_O_UF + F_WIDTH]).astype(BF16)
    ur_out[...] = _dot(hb, w_ref[:, _O_UR:_O_UR + RNN_WIDTH])
    gug_out[...] = _gelu(_dot(hb, w_ref[:, _O_UG:_O_UG + RNN_WIDTH])).astype(BF16)
    for j in range(3):
        a = j * D_MODEL
        g_out[:, a:a + D_MODEL] = _sigmoid(_dot(hb, w_ref[:, _O_GL + a:_O_GL + a + D_MODEL])).astype(BF16)


def _in_call(x2d, mod_l, norm_g, w_in_p, qn, kvn, wuq, wukv, tabs, seq_len, mod_row, tm):
    rows, d = x2d.shape
    tiles_per_seq = seq_len // tm
    hp = N_HEADS * HEAD_PAD

    def row_spec(width):
        return pl.BlockSpec((tm, width), lambda i: (i, 0))

    def mod_spec(chunk):
        return pl.BlockSpec((None, None, 1, d), lambda i: (mod_row(i), chunk, 0, 0))

    tab_spec = pl.BlockSpec((tm, HEAD_PAD), lambda i: (i % tiles_per_seq, 0))
    out_shapes = (
        jax.ShapeDtypeStruct((rows, hp), BF16),
        jax.ShapeDtypeStruct((rows, hp), BF16),
        jax.ShapeDtypeStruct((rows, hp), BF16),
        jax.ShapeDtypeStruct((rows, F_WIDTH), BF16),
        jax.ShapeDtypeStruct((rows, RNN_WIDTH), F32),
        jax.ShapeDtypeStruct((rows, RNN_WIDTH), BF16),
        jax.ShapeDtypeStruct((rows, 3 * d), BF16),
    )
    return pl.pallas_call(
        _in_kernel,
        grid=(rows // tm,),
        in_specs=[
            row_spec(d),
            _resident((1, d)),
            mod_spec(0),
            mod_spec(1),
            _resident(w_in_p[0].shape), _resident(w_in_p[1].shape), _resident(w_in_p[2].shape),
            _resident((1, Q_LORA)),
            _resident((1, KV_LORA)),
            _resident(wuq.shape),
            _resident(wukv.shape),
            tab_spec, tab_spec, tab_spec, tab_spec,
            _resident((1, hp)),
        ],
        out_specs=tuple(row_spec(s.shape[1]) for s in out_shapes),
        out_shape=out_shapes,
        compiler_params=_cparams(("arbitrary",)),
        name="in_proj",
    )(x2d, norm_g, mod_l, mod_l, *w_in_p, qn, kvn, wuq, wukv, *tabs, _value_ones())


def _attn_kernel(*refs, n_seg):
    q_ref = refs[0]
    k_refs = refs[1:1 + n_seg]
    v_refs = refs[1 + n_seg:1 + 2 * n_seg]
    o_ref = refs[1 + 2 * n_seg]
    nt = (((1,), (1,)), ((), ()))
    low = lax.broadcasted_iota(jnp.int32, (q_ref.shape[0], HEAD_PAD), 1) < V_DIM
    def scores(hd):
        a = hd * HEAD_PAD
        qh = q_ref[:, a:a + HEAD_PAD]
        return [lax.dot_general(qh, k[:, a:a + HEAD_PAD], nt, preferred_element_type=F32) for k in k_refs]

    s_next = scores(0)
    for pair in range(N_HEADS // 2):
        outs = []
        for hd in (2 * pair, 2 * pair + 1):
            a = hd * HEAD_PAD
            s = s_next
            if hd + 1 < N_HEADS:
                s_next = scores(hd + 1)
            m = functools.reduce(jnp.maximum, [jnp.max(si, axis=-1, keepdims=True) for si in s])
            p = [jnp.exp2((si - m).astype(BF16)) for si in s]
            o = functools.reduce(jnp.add, [_dot(pi, v[:, a:a + HEAD_PAD]) for pi, v in zip(p, v_refs)])
            one_lane = V_DIM if hd % 2 == 0 else 0
            outs.append(o * (1.0 / o[:, one_lane:one_lane + 1]))
        o_ref[:, pair * HEAD_PAD:(pair + 1) * HEAD_PAD] = jnp.where(low, outs[0], outs[1]).astype(BF16)


def _attn_call(q, segs, q_len, tq):
    rows, hp = q.shape
    tiles = q_len // tq
    n_seg = len(segs)
    k_specs = [pl.BlockSpec((ln, hp), lambda i: (i // tiles, 0)) for _, _, ln in segs]
    return pl.pallas_call(
        functools.partial(_attn_kernel, n_seg=n_seg),
        grid=(rows // tq,),
        in_specs=[pl.BlockSpec((tq, hp), lambda i: (i, 0))] + k_specs + k_specs,
        out_specs=pl.BlockSpec((tq, N_HEADS * V_DIM), lambda i: (i, 0)),
        out_shape=jax.ShapeDtypeStruct((rows, N_HEADS * V_DIM), BF16),
        compiler_params=_cparams(("arbitrary",)),
        name="attention",
    )(q, *[k for k, _, _ in segs], *[v for _, v, _ in segs])


def _fourier_kernel(u_ref, cs_ref, dft_ref, o_ref, ab_ref, *, seq_len):
    @pl.when(pl.program_id(1) == 0)
    def _():
        ab = _dot(u_ref[...], cs_ref[...])
        ab_ref[:seq_len, :] = ab[:, :F_WIDTH].astype(BF16)
        ab_ref[seq_len:, :] = ab[:, F_WIDTH:].astype(BF16)

    o_ref[...] = _dot(dft_ref[...], ab_ref[...]).astype(BF16)


def _fourier_call(uf, cs, dft, seq_len, tmf):
    rows = uf.shape[0]
    tiles = seq_len // tmf
    return pl.pallas_call(
        functools.partial(_fourier_kernel, seq_len=seq_len),
        grid=(rows // seq_len, tiles),
        in_specs=[
            pl.BlockSpec((seq_len, F_WIDTH), lambda b, m: (b, 0)),
            _resident(cs.shape),
            pl.BlockSpec((tmf, 2 * seq_len), lambda b, m: (m, 0)),
        ],
        out_specs=pl.BlockSpec((tmf, F_WIDTH), lambda b, m: (b * tiles + m, 0)),
        out_shape=jax.ShapeDtypeStruct((rows, F_WIDTH), BF16),
        scratch_shapes=[pltpu.VMEM((2 * seq_len, F_WIDTH), BF16)],
        compiler_params=_cparams(("arbitrary", "arbitrary")),
        name="fourier",
    )(uf, cs, dft)


_RNN_HALF = RNN_WIDTH // 2
_RNN_CHUNK = 256


def _rnn_kernel(urc_ref, urx_ref, gugc_ref, gugx_ref, cw_ref, cb_ref, wg_ref, bg_ref, lam_ref,
                yc_ref, yx_ref, pad_ref, af_ref, bf_ref, ab_ref, bb_ref, *, ctx_len, seq_len):
    half = _RNN_HALF
    row8 = lax.broadcasted_iota(jnp.int32, (SUBLANES, half), 0)
    zeros8 = jnp.zeros((SUBLANES, half), F32)

    def gates(ur_ref, n, j):
        lo = j * half
        pad_ref[0:SUBLANES, :] = zeros8
        pad_ref[SUBLANES:SUBLANES + n, :] = ur_ref[:, lo:lo + half]
        pad_ref[SUBLANES + n:2 * SUBLANES + n, :] = zeros8
        cw = cw_ref[:, lo:lo + half]
        cb = cb_ref[:, lo:lo + half]
        lam = lam_ref[:, lo:lo + half]
        k = (-0.5 * LRU_C * math.log2(math.e)) * (jnp.maximum(-lam, 0.0) + jnp.log1p(jnp.exp(-jnp.abs(lam))))
        wg = wg_ref[j]
        bg = bg_ref[j]
        ext_rows = _RNN_CHUNK + 2 * SUBLANES

        def chunk(c, carry):
            r0 = pl.multiple_of(c * _RNN_CHUNK, _RNN_CHUNK)
            ext = pad_ref[pl.ds(r0, ext_rows), :]
            sl = slice(SUBLANES, SUBLANES + _RNN_CHUNK)
            uc = (cb + cw[0:1] * pltpu.roll(ext, 2, axis=0)[sl] + cw[1:2] * pltpu.roll(ext, 1, axis=0)[sl]
                  + cw[2:3] * ext[sl] + cw[3:4] * pltpu.roll(ext, ext_rows - 1, axis=0)[sl])
            g = _dot(uc.astype(BF16), wg) + bg
            uch = 0.5 * uc
            for d, (a_ref, b_ref) in enumerate(((af_ref, bf_ref), (ab_ref, bb_ref))):
                kd = k[d:d + 1]
                a = jnp.exp2(kd + kd * jnp.tanh(g[:, (2 * d) * half:(2 * d + 1) * half]))
                a_ref[pl.ds(r0, _RNN_CHUNK), :] = a
                om = 1.0 - a * a
                sq = om * lax.rsqrt(jnp.maximum(om, 1e-30))
                b_ref[pl.ds(r0, _RNN_CHUNK), :] = (sq * uch) * (1.0 + jnp.tanh(g[:, (2 * d + 1) * half:(2 * d + 2) * half]))
            return carry

        lax.fori_loop(0, n // _RNN_CHUNK, chunk, 0)

    def scan(n, hf, hb):
        nb = n // SUBLANES

        def block(i, carry):
            hf, hb = carry
            rf = pl.multiple_of(i * SUBLANES, SUBLANES)
            a = af_ref[pl.ds(rf, SUBLANES), :]
            b = bf_ref[pl.ds(rf, SUBLANES), :]
            for d in (1, 2, 4):
                ok = row8 >= d
                b = jnp.where(ok, a * pltpu.roll(b, d, axis=0) + b, b)
                a = jnp.where(ok, a * pltpu.roll(a, d, axis=0), a)
            h = a * hf + b
            bf_ref[pl.ds(rf, SUBLANES), :] = h
            hf = jnp.broadcast_to(h[SUBLANES - 1:SUBLANES, :], h.shape)

            rb = pl.multiple_of((nb - 1 - i) * SUBLANES, SUBLANES)
            a = ab_ref[pl.ds(rb, SUBLANES), :]
            b = bb_ref[pl.ds(rb, SUBLANES), :]
            for d in (1, 2, 4):
                ok = row8 < SUBLANES - d
                b = jnp.where(ok, a * pltpu.roll(b, SUBLANES - d, axis=0) + b, b)
                a = jnp.where(ok, a * pltpu.roll(a, SUBLANES - d, axis=0), a)
            h = a * hb + b
            bb_ref[pl.ds(rb, SUBLANES), :] = h
            hb = jnp.broadcast_to(h[0:1, :], h.shape)
            return hf, hb

        return lax.fori_loop(0, nb, block, (hf, hb), unroll=2)

    for j in range(RNN_WIDTH // half):
        lo = j * half
        gates(urc_ref, ctx_len, j)
        hf, hb = scan(ctx_len, zeros8, zeros8)
        yc_ref[:, lo:lo + half] = ((bf_ref[0:ctx_len, :] + bb_ref[0:ctx_len, :])
                                   * gugc_ref[:, lo:lo + half].astype(F32)).astype(BF16)
        gates(urx_ref, seq_len, j)
        scan(seq_len, hf, hb)
        yx_ref[:, lo:lo + half] = ((bf_ref[0:seq_len, :] + bb_ref[0:seq_len, :])
                                   * gugx_ref[:, lo:lo + half].astype(F32)).astype(BF16)


def _rnn_call(ur_c, ur_x, gug_c, gug_x, cw, cb, wg, bg, lam, ctx_len, seq_len):
    batch = ur_x.shape[0] // seq_len
    w = RNN_WIDTH

    def seq_spec(n):
        return pl.BlockSpec((n, w), lambda b: (b, 0))

    return pl.pallas_call(
        functools.partial(_rnn_kernel, ctx_len=ctx_len, seq_len=seq_len),
        grid=(batch,),
        in_specs=[seq_spec(ctx_len), seq_spec(seq_len), seq_spec(ctx_len), seq_spec(seq_len),
                  _resident(cw.shape), _resident(cb.shape), _resident(wg.shape), _resident(bg.shape),
                  _resident(lam.shape)],
        out_specs=(seq_spec(ctx_len), seq_spec(seq_len)),
        out_shape=(jax.ShapeDtypeStruct((batch * ctx_len, w), BF16),
                   jax.ShapeDtypeStruct((batch * seq_len, w), BF16)),
        scratch_shapes=[pltpu.VMEM((seq_len + 2 * SUBLANES, _RNN_HALF), F32)]
        + [pltpu.VMEM((seq_len, _RNN_HALF), F32)] * 4,
        compiler_params=_cparams(("arbitrary",)),
        name="rglru",
    )(ur_c, ur_x, gug_c, gug_x, cw, cb, wg, bg, lam)


def _merge_kernel(x_ref, att_ref, yf_ref, yr_ref, g_ref, gate_ref, woa_ref, wof_ref, wor_ref, wout_ref, o_ref):
    d = D_MODEL
    m = (g_ref[:, 0:d].astype(F32) * _dot(att_ref[...], woa_ref[...])
         + g_ref[:, d:2 * d].astype(F32) * _dot(yf_ref[...], wof_ref[...])
         + g_ref[:, 2 * d:3 * d].astype(F32) * _dot(yr_ref[...], wor_ref[...]))
    o_ref[...] = x_ref[...] + gate_ref[...] * _dot(m.astype(BF16), wout_ref[...])


def _merge_call(x2d, att, yf, yr, g, mod_l, woa, wof, wor, wout, mod_row, tm):
    rows, d = x2d.shape

    def row_spec(width):
        return pl.BlockSpec((tm, width), lambda i: (i, 0))

    return pl.pallas_call(
        _merge_kernel,
        grid=(rows // tm,),
        in_specs=[row_spec(d), row_spec(att.shape[1]), row_spec(yf.shape[1]), row_spec(yr.shape[1]),
                  row_spec(3 * d),
                  pl.BlockSpec((None, None, 1, d), lambda i: (mod_row(i), 2, 0, 0)),
                  _resident(woa.shape), _resident(wof.shape), _resident(wor.shape), _resident(wout.shape)],
        out_specs=row_spec(d),
        out_shape=jax.ShapeDtypeStruct((rows, d), F32),
        compiler_params=_cparams(("arbitrary",)),
        name="merge",
    )(x2d, att, yf, yr, g, mod_l, woa, wof, wor, wout)


_FF_CHUNK = 256


def _ffn_kernel(x_ref, xp_ref, xn_ref, g_ref, shift_ref, scale_ref, gate_ref,
                wu_ref, cw_ref, cb_ref, wd_ref, fin_ref, o_ref, h_ref, u0_ref, u1_ref, a0_ref, a1_ref, *, tm, tiles_per_seq, final_norm):
    i = pl.program_id(0)
    ext = tm + 2 * SUBLANES
    n_chunks = D_FF // _FF_CHUNK
    fc = _FF_CHUNK

    def prep(x):
        return (_rms(x, g_ref[...]) * (1.0 + scale_ref[...]) + shift_ref[...]).astype(BF16)

    first = (i % tiles_per_seq) == 0
    last = (i % tiles_per_seq) == tiles_per_seq - 1
    hp = prep(xp_ref[...])
    hn = prep(xn_ref[...])
    h_ref[0:SUBLANES, :] = jnp.where(first, jnp.zeros_like(hp), hp)
    h_ref[SUBLANES:SUBLANES + tm, :] = prep(x_ref[...])
    h_ref[SUBLANES + tm:, :] = jnp.where(last, jnp.zeros_like(hn), hn)

    sl = slice(SUBLANES, SUBLANES + tm)

    def cols(c):
        return pl.ds(pl.multiple_of(c * fc, fc), fc)

    hm = tm // 2
    hme = hm + 2 * SUBLANES
    slh = slice(SUBLANES, SUBLANES + hm)

    def conv(u, cw, cb):
        return cb + cw[0:1] * pltpu.roll(u, 1, axis=0)[slh] + cw[1:2] * u[slh] + cw[2:3] * pltpu.roll(u, hme - 1, axis=0)[slh]

    def activate(c, u_ref, a_ref, r):
        rows = slice(r * hm, r * hm + hme)
        up = conv(u_ref[rows, 0:fc], cw_ref[:, cols(c)], cb_ref[:, cols(c)])
        gt = conv(u_ref[rows, fc:2 * fc], cw_ref[:, cols(n_chunks + c)], cb_ref[:, cols(n_chunks + c)])
        a_ref[r * hm:(r + 1) * hm, :] = (_gelu(gt) * up).astype(BF16)

    def down_proj(c, a_ref, r):
        rows = slice(r * hm, (r + 1) * hm)
        o_ref[rows, :] += _dot(a_ref[rows, :], wd_ref[cols(c), :])

    u_bufs = (u0_ref, u1_ref)
    a_bufs = (a0_ref, a1_ref)

    def step(t, parity, do_up=True, do_act=True, do_down=True):
        for r in range(2):
            if do_up:
                u_bufs[parity][:, r * fc:(r + 1) * fc] = _dot(h_ref[...], wu_ref[:, cols(r * n_chunks + t)])
            if do_down:
                down_proj(t - 2, a_bufs[parity], r)
            if do_act:
                activate(t - 1, u_bufs[1 - parity], a_bufs[1 - parity], r)

    assert n_chunks % 2 == 1 and n_chunks >= 5
    o_ref[...] = jnp.zeros_like(o_ref)
    step(0, 0, do_act=False, do_down=False)
    step(1, 1, do_down=False)

    def body(k, carry):
        t = 2 * k + 2
        step(t, 0)
        step(t + 1, 1)
        return carry

    lax.fori_loop(0, (n_chunks - 3) // 2, body, 0)
    step(n_chunks - 1, 0)
    step(n_chunks, 1, do_up=False)
    step(n_chunks + 1, 0, do_up=False, do_act=False)
    y = x_ref[...] + gate_ref[...] * o_ref[...]
    if final_norm:
        y = _rms(y, fin_ref[...])
    o_ref[...] = y


def _ffn_call(x2d, mod_l, norm_g, wu, cw, cb, wd, fin_g, seq_len, mod_row, tm, final_norm):
    rows, d = x2d.shape
    tiles_per_seq = seq_len // tm
    blk8 = tm // SUBLANES
    n_blk8 = rows // SUBLANES

    def mod_spec(chunk):
        return pl.BlockSpec((None, None, 1, d), lambda i: (mod_row(i), chunk, 0, 0))

    return pl.pallas_call(
        functools.partial(_ffn_kernel, tm=tm, tiles_per_seq=tiles_per_seq, final_norm=final_norm),
        grid=(rows // tm,),
        in_specs=[
            pl.BlockSpec((tm, d), lambda i: (i, 0)),
            pl.BlockSpec((SUBLANES, d), lambda i: (jnp.maximum(i * blk8 - 1, 0), 0)),
            pl.BlockSpec((SUBLANES, d), lambda i: (jnp.minimum((i + 1) * blk8, n_blk8 - 1), 0)),
            _resident((1, d)),
            mod_spec(3), mod_spec(4), mod_spec(5),
            _resident(wu.shape), _resident(cw.shape), _resident(cb.shape), _resident(wd.shape),
            _resident((1, d)),
        ],
        out_specs=pl.BlockSpec((tm, d), lambda i: (i, 0)),
        out_shape=jax.ShapeDtypeStruct((rows, d), F32),
        scratch_shapes=[pltpu.VMEM((tm + 2 * SUBLANES, d), BF16),
                        pltpu.VMEM((tm + 2 * SUBLANES, 2 * _FF_CHUNK), F32),
                        pltpu.VMEM((tm + 2 * SUBLANES, 2 * _FF_CHUNK), F32),
                        pltpu.VMEM((tm, _FF_CHUNK), BF16),
                        pltpu.VMEM((tm, _FF_CHUNK), BF16)],
        compiler_params=_cparams(("arbitrary",), flags=_INTERLEAVE),
        name="conv_ffn",
    )(x2d, x2d, x2d, norm_g, mod_l, mod_l, mod_l, wu, cw, cb, wd, fin_g)


def _prep_ffn(w_up, cw, cb, w_down):
    return w_up.astype(BF16), cw, cb[None, :], w_down.astype(BF16)


def _rope_tables(seq_len, rope):
    scale = QK_DIM ** -0.5 * math.log2(math.e)
    if rope:
        n_rows = seq_len // GRID_W
        row = jnp.repeat(jnp.arange(n_rows, dtype=F32), GRID_W)
        col = jnp.tile(jnp.arange(GRID_W, dtype=F32), n_rows)
        n_freq = QK_ROPE // 4
        inv = ROPE_BASE ** (-jnp.arange(n_freq, dtype=F32) / n_freq)
        ang = jnp.concatenate([row[:, None] * inv, col[:, None] * inv], axis=-1)
        cos, sin = jnp.cos(ang), jnp.sin(ang)
    else:
        cos = jnp.ones((seq_len, QK_ROPE // 2), F32)
        sin = jnp.zeros((seq_len, QK_ROPE // 2), F32)
    ones = jnp.ones((seq_len, QK_NOPE), F32)
    zeros = jnp.zeros((seq_len, QK_NOPE), F32)
    tail = jnp.zeros((seq_len, HEAD_PAD - QK_DIM), F32)
    cos2 = jnp.concatenate([cos, cos], axis=-1)
    sin2 = jnp.concatenate([-sin, sin], axis=-1)
    cq = scale * jnp.concatenate([ones, cos2, tail], axis=-1)
    sq = scale * jnp.concatenate([zeros, sin2, tail], axis=-1)
    ck = jnp.concatenate([zeros, cos2, tail], axis=-1)
    sk = jnp.concatenate([zeros, sin2, tail], axis=-1)
    return cq, sq, ck, sk


def _value_ones():
    lane = np.arange(N_HEADS * HEAD_PAD) % HEAD_PAD
    head = np.arange(N_HEADS * HEAD_PAD) // HEAD_PAD
    return jnp.asarray(np.where(head % 2 == 0, lane == V_DIM, lane == 0)[None, :], F32)


def _swap_halves(w):
    half = w.shape[-1] // 2
    return jnp.concatenate([w[..., half:], w[..., :half]], axis=-1)


def _prep_w_in(w):
    d = w.shape[0]
    n_front = Q_LORA + KV_LORA
    kr = w[:, n_front:n_front + QK_ROPE]
    z_lo = jnp.zeros((d, QK_NOPE), w.dtype)
    z_hi = jnp.zeros((d, HEAD_PAD - QK_DIM), w.dtype)
    w_kr = jnp.concatenate([z_lo, kr, z_hi, z_lo, _swap_halves(kr), z_hi], axis=-1)
    return w[:, :n_front].astype(BF16), w_kr.astype(BF16), w[:, n_front + QK_ROPE:].astype(BF16)


def _prep_w_uq(w):
    r, h, _ = w.shape
    pad = jnp.zeros((r, h, HEAD_PAD - QK_DIM), w.dtype)
    y1 = jnp.concatenate([w, pad], axis=-1)
    y2 = jnp.concatenate([jnp.zeros((r, h, QK_NOPE), w.dtype), _swap_halves(w[..., QK_NOPE:]), pad], axis=-1)
    return jnp.concatenate([y1.reshape(r, h * HEAD_PAD), y2.reshape(r, h * HEAD_PAD)], axis=-1).astype(BF16)


def _prep_w_ukv(w):
    r, h, _ = w.shape
    k = jnp.concatenate([w[..., :QK_NOPE], jnp.zeros((r, h, HEAD_PAD - QK_NOPE), w.dtype)], axis=-1)
    v = w[..., QK_NOPE:]
    z = jnp.zeros_like(v)
    odd = (jnp.arange(h) % 2 == 1)[None, :, None]
    vp = jnp.concatenate([jnp.where(odd, z, v), jnp.where(odd, v, z)], axis=-1)
    return jnp.concatenate([k.reshape(r, h * HEAD_PAD), vp.reshape(r, h * HEAD_PAD)], axis=-1).astype(BF16)


def _prep_gates(w_a, b_a, w_x, b_x):
    half = _RNN_HALF
    heads_per_half = half // RNN_BLOCK

    def blockdiag(w):
        eye = jnp.eye(heads_per_half, dtype=w.dtype)
        return jnp.einsum('hij,hk->hikj', w, eye).reshape(half, half)

    ws, bs = [], []
    for j in range(RNN_WIDTH // half):
        hs = slice(j * heads_per_half, (j + 1) * heads_per_half)
        ls = slice(j * half, (j + 1) * half)
        ws.append(jnp.concatenate([blockdiag(w_a[0, hs]), blockdiag(w_x[0, hs]),
                                   blockdiag(w_a[1, hs]), blockdiag(w_x[1, hs])], axis=-1))
        bs.append(jnp.concatenate([b_a[0, ls], b_x[0, ls], b_a[1, ls], b_x[1, ls]], axis=-1)[None, :])
    return (0.5 * jnp.stack(ws)).astype(BF16), 0.5 * jnp.stack(bs)


def _dft_tables(seq_len):
    lo_n = 32 if seq_len % 32 == 0 else 1
    t = jnp.arange(seq_len, dtype=jnp.int32)

    def table(rows):
        ang = (2.0 * math.pi / seq_len) * ((rows[:, None] * t[None, :]) % seq_len).astype(F32)
        return jnp.cos(ang), jnp.sin(ang)

    c_hi, s_hi = table(jnp.arange(seq_len // lo_n, dtype=jnp.int32) * lo_n)
    c_lo, s_lo = table(jnp.arange(lo_n, dtype=jnp.int32))
    s = seq_len ** -0.5
    cos = (c_hi[:, None, :] * c_lo[None, :, :] - s_hi[:, None, :] * s_lo[None, :, :]).reshape(seq_len, seq_len)
    sin = (s_hi[:, None, :] * c_lo[None, :, :] + c_hi[:, None, :] * s_lo[None, :, :]).reshape(seq_len, seq_len)
    return jnp.concatenate([cos * s, sin * s], axis=-1).astype(BF16)


def _channel_dft():
    k = jnp.arange(F_GROUP_W, dtype=jnp.int32)
    ang = (2.0 * math.pi / F_GROUP_W) * ((k[:, None] * k[None, :]) % F_GROUP_W).astype(F32)
    s = F_GROUP_W ** -0.5
    eye = jnp.eye(F_GROUPS, dtype=F32)
    c = jnp.kron(eye, jnp.cos(ang) * s)
    sn = jnp.kron(eye, jnp.sin(ang) * s)
    return jnp.concatenate([c, -sn], axis=-1).astype(BF16)


def kernel(x, c, ctx, c_ctx, w_ada, b_ada, norm_mix, norm_ffn, w_in, q_norm, kv_norm, w_uq, w_ukv, w_o_attn,
           w_o_fourier, rnn_conv_w, rnn_conv_b, rg_w_a, rg_b_a, rg_w_x, rg_b_x, rg_lambda, w_o_rnn, w_out,
           w_up, ffn_conv_w, ffn_conv_b, w_down, final_norm):
    batch, seq_len, d = x.shape
    ctx_len = ctx.shape[1]
    depth = w_ada.shape[0]
    assert batch + 1 <= MOD_ROWS and d == D_MODEL

    cc = jnp.concatenate([c, c_ctx[None, :], jnp.zeros((MOD_ROWS - batch - 1, d), F32)], axis=0)
    mod = _ada_call(cc, w_ada, b_ada).reshape(depth, MOD_ROWS, N_MOD, 1, d)

    tabs_x = _rope_tables(seq_len, True)
    tabs_c = _rope_tables(ctx_len, False)
    dft_x = _dft_tables(seq_len)
    dft_c = _dft_tables(ctx_len)
    cs = _channel_dft()

    tm_in = min(512, seq_len)
    tm_x = 256
    tm_c = min(256, ctx_len)
    tq = 256
    tm_ffn = min(1024, seq_len)

    def x_row(tm):
        return lambda i: i // (seq_len // tm)

    c_row = lambda i: batch

    xs = x.reshape(batch * seq_len, d)
    cs_tok = ctx.reshape(batch * ctx_len, d)

    for l in range(depth):
        last = l == depth - 1
        mod_l = mod[l]
        w_in_p = _prep_w_in(w_in[l])
        wuq = _prep_w_uq(w_uq[l])
        wukv = _prep_w_ukv(w_ukv[l])
        wg, bg = _prep_gates(rg_w_a[l], rg_b_a[l], rg_w_x[l], rg_b_x[l])
        nm = norm_mix[l][None, :]
        nf = norm_ffn[l][None, :]
        qn = q_norm[l][None, :]
        kvn = kv_norm[l][None, :]
        woa = w_o_attn[l].astype(BF16)
        wof = w_o_fourier[l].astype(BF16)
        wor = w_o_rnn[l].astype(BF16)
        wout = w_out[l].astype(BF16)
        wup, fcw, fcb, wdn = _prep_ffn(w_up[l], ffn_conv_w[l], ffn_conv_b[l], w_down[l])
        rcw = rnn_conv_w[l]
        rcb = rnn_conv_b[l][None, :]
        fin = final_norm[None, :]

        q_x, k_x, v_x, uf_x, ur_x, gug_x, g_x = _in_call(xs, mod_l, nm, w_in_p, qn, kvn, wuq, wukv, tabs_x,
                                                          seq_len, x_row(tm_in), tm_in)
        q_c, k_c, v_c, uf_c, ur_c, gug_c, g_c = _in_call(cs_tok, mod_l, nm, w_in_p, qn, kvn, wuq, wukv, tabs_c,
                                                          ctx_len, c_row, tm_c)

        att_x = _attn_call(q_x, [(k_c, v_c, ctx_len), (k_x, v_x, seq_len)], seq_len, tq)
        yf_x = _fourier_call(uf_x, cs, dft_x, seq_len, min(512, seq_len))
        yr_c, yr_x = _rnn_call(ur_c, ur_x, gug_c, gug_x, rcw, rcb, wg, bg, rg_lambda[l], ctx_len, seq_len)

        xs = _merge_call(xs, att_x, yf_x, yr_x, g_x, mod_l, woa, wof, wor, wout, x_row(tm_x), tm_x)
        xs = _ffn_call(xs, mod_l, nf, wup, fcw, fcb, wdn, fin, seq_len, x_row(tm_ffn), tm_ffn, last)

        if not last:
            att_c = _attn_call(q_c, [(k_c, v_c, ctx_len)], ctx_len, min(tq, ctx_len))
            yf_c = _fourier_call(uf_c, cs, dft_c, ctx_len, ctx_len)
            cs_tok = _merge_call(cs_tok, att_c, yf_c, yr_c, g_c, mod_l, woa, wof, wor, wout, c_row, tm_c)
            cs_tok = _ffn_call(cs_tok, mod_l, nf, wup, fcw, fcb, wdn, fin, ctx_len, c_row, ctx_len, False)

    return xs.reshape(batch, seq_len, d)
```

```python
import functools
import math

import jax
import jax.numpy as jnp
import numpy as np
from jax import lax
from jax.experimental import pallas as pl
from jax.experimental.pallas import tpu as pltpu

F32 = jnp.float32
BF16 = jnp.bfloat16

D_MODEL = 1024
GRID_W = 64
N_HEADS = 8
Q_LORA = 384
KV_LORA = 256
QK_NOPE = 64
QK_ROPE = 32
V_DIM = 64
QK_DIM = QK_NOPE + QK_ROPE
ROPE_BASE = 10000.0
F_GROUPS = 4
F_GROUP_W = 128
F_WIDTH = F_GROUPS * F_GROUP_W
RNN_WIDTH = 512
RNN_HEADS = 8
RNN_BLOCK = RNN_WIDTH // RNN_HEADS
RNN_CONV = 4
LRU_C = 8.0
D_FF = 2816
FFN_CONV = 3
N_MOD = 6
EPS = 1e-6

LANES = 128
SUBLANES = 8
HEAD_PAD = 128
MOD_ROWS = 16

_O_UF = 0
_O_UR = _O_UF + F_WIDTH
_O_UG = _O_UR + RNN_WIDTH
_O_GL = _O_UG + RNN_WIDTH

_VMEM_LIMIT = 56 * 2 ** 20


_INTERLEAVE = None


def _cparams(sem, vmem=_VMEM_LIMIT, flags=None):
    return pltpu.CompilerParams(dimension_semantics=sem, vmem_limit_bytes=vmem, flags=flags)


def _resident(shape):
    nd = len(shape)
    return pl.BlockSpec(shape, lambda *_: (0,) * nd, pipeline_mode=pl.Buffered(1))


def _dot(a, b):
    return jnp.dot(a, b, preferred_element_type=F32)


def _sigmoid(x):
    return 0.5 * (1.0 + jnp.tanh(0.5 * x))


def _gelu(x):
    return 0.5 * x * (1.0 + jnp.tanh(math.sqrt(2.0 / math.pi) * (x + 0.044715 * (x * x * x))))


def _rms(x, g):
    return x * lax.rsqrt(jnp.mean(x * x, axis=-1, keepdims=True) + EPS) * g


def _ada_kernel(cc_ref, w_ref, b_ref, o_ref):
    cc = cc_ref[...]
    s = cc * _sigmoid(cc)
    o_ref[...] = _dot(s.astype(BF16), w_ref[...].astype(BF16)) + b_ref[...]


def _ada_call(cc, w_ada, b_ada):
    depth, d, n = w_ada.shape
    tn = 1536
    return pl.pallas_call(
        _ada_kernel,
        grid=(depth, n // tn),
        in_specs=[
            pl.BlockSpec((MOD_ROWS, d), lambda l, j: (0, 0)),
            pl.BlockSpec((None, d, tn), lambda l, j: (l, 0, j)),
            pl.BlockSpec((None, 1, tn), lambda l, j: (l, 0, j)),
        ],
        out_specs=pl.BlockSpec((None, MOD_ROWS, tn), lambda l, j: (l, 0, j)),
        out_shape=jax.ShapeDtypeStruct((depth, MOD_ROWS, n), F32),
        compiler_params=_cparams(("arbitrary", "arbitrary")),
        name="adaln",
    )(cc, w_ada, b_ada.reshape(depth, 1, n))


def _in_kernel(x_ref, g_ref, shift_ref, scale_ref, wf_ref, wkr_ref, w_ref, qn_ref, kvn_ref, wuq_ref, wukv_ref,
               cq_ref, sq_ref, ck_ref, sk_ref, vone_ref,
               q_out, k_out, v_out, uf_out, ur_out, gug_out, g_out):
    h = _rms(x_ref[...], g_ref[...]) * (1.0 + scale_ref[...]) + shift_ref[...]
    hb = h.astype(BF16)
    hp = N_HEADS * HEAD_PAD

    cqn = _rms(_dot(hb, wf_ref[:, 0:Q_LORA]), qn_ref[...]).astype(BF16)
    y = _dot(cqn, wuq_ref[...])
    cq_t = cq_ref[...]
    sq_t = sq_ref[...]
    for hd in range(N_HEADS):
        a = hd * HEAD_PAD
        q_out[:, a:a + HEAD_PAD] = (y[:, a:a + HEAD_PAD] * cq_t + y[:, hp + a:hp + a + HEAD_PAD] * sq_t).astype(BF16)

    ckvn = _rms(_dot(hb, wf_ref[:, Q_LORA:Q_LORA + KV_LORA]), kvn_ref[...]).astype(BF16)
    kv = _dot(ckvn, wukv_ref[...])
    kr = _dot(hb, wkr_ref[...])
    k_rope = kr[:, :HEAD_PAD] * ck_ref[...] + kr[:, HEAD_PAD:] * sk_ref[...]
    for hd in range(N_HEADS):
        a = hd * HEAD_PAD
        k_out[:, a:a + HEAD_PAD] = (kv[:, a:a + HEAD_PAD] + k_rope).astype(BF16)
    v_out[...] = (kv[:, hp:] + vone_ref[...]).astype(BF16)

    uf_out[...] = _dot(hb, w_ref[:, _O_UF:_O_UF + F_WIDTH]).astype(BF16)
    ur_out[...] = _dot(hb, w_ref[:, _O_UR:_O_UR + RNN_WIDTH])
    gug_out[...] = _gelu(_dot(hb, w_ref[:, _O_UG:_O_UG + RNN_WIDTH])).astype(BF16)
    for j in range(3):
        a = j * D_MODEL
        g_out[:, a:a + D_MODEL] = _sigmoid(_dot(hb, w_ref[:, _O_GL + a:_O_GL + a + D_MODEL])).astype(BF16)


def _in_call(x2d, mod_l, norm_g, w_in_p, qn, kvn, wuq, wukv, tabs, seq_len, mod_row, tm):
    rows, d = x2d.shape
    tiles_per_seq = seq_len // tm
    hp = N_HEADS * HEAD_PAD

    def row_spec(width):
        return pl.BlockSpec((tm, width), lambda i: (i, 0))

    def mod_spec(chunk):
        return pl.BlockSpec((None, None, 1, d), lambda i: (mod_row(i), chunk, 0, 0))

    tab_spec = pl.BlockSpec((tm, HEAD_PAD), lambda i: (i % tiles_per_seq, 0))
    out_shapes = (
        jax.ShapeDtypeStruct((rows, hp), BF16),
        jax.ShapeDtypeStruct((rows, hp), BF16),
        jax.ShapeDtypeStruct((rows, hp), BF16),
        jax.ShapeDtypeStruct((rows, F_WIDTH), BF16),
        jax.ShapeDtypeStruct((rows, RNN_WIDTH), F32),
        jax.ShapeDtypeStruct((rows, RNN_WIDTH), BF16),
        jax.ShapeDtypeStruct((rows, 3 * d), BF16),
    )
    return pl.pallas_call(
        _in_kernel,
        grid=(rows // tm,),
        in_specs=[
            row_spec(d),
            _resident((1, d)),
            mod_spec(0),
            mod_spec(1),
            _resident(w_in_p[0].shape), _resident(w_in_p[1].shape), _resident(w_in_p[2].shape),
            _resident((1, Q_LORA)),
            _resident((1, KV_LORA)),
            _resident(wuq.shape),
            _resident(wukv.shape),
            tab_spec, tab_spec, tab_spec, tab_spec,
            _resident((1, hp)),
        ],
        out_specs=tuple(row_spec(s.shape[1]) for s in out_shapes),
        out_shape=out_shapes,
        compiler_params=_cparams(("arbitrary",)),
        name="in_proj",
    )(x2d, norm_g, mod_l, mod_l, *w_in_p, qn, kvn, wuq, wukv, *tabs, _value_ones())


_ATTN_SUB_ROWS = 256


def _attn_kernel(*refs, n_seg):
    q_ref = refs[0]
    k_refs = refs[1:1 + n_seg]
    v_refs = refs[1 + n_seg:1 + 2 * n_seg]
    o_ref = refs[1 + 2 * n_seg]
    nt = (((1,), (1,)), ((), ()))
    sub = _ATTN_SUB_ROWS
    low = lax.broadcasted_iota(jnp.int32, (sub, HEAD_PAD), 1) < V_DIM

    def scores(item):
        r0, hd = item
        a = hd * HEAD_PAD
        qh = q_ref[r0:r0 + sub, a:a + HEAD_PAD]
        return [lax.dot_general(qh, k[:, a:a + HEAD_PAD], nt, preferred_element_type=F32) for k in k_refs]

    items = [(r0, hd) for r0 in range(0, q_ref.shape[0], sub) for hd in range(N_HEADS)]
    outs = {}
    s_next = scores(items[0])
    for idx, (r0, hd) in enumerate(items):
        a = hd * HEAD_PAD
        s = s_next
        if idx + 1 < len(items):
            s_next = scores(items[idx + 1])
        m = functools.reduce(jnp.maximum, [jnp.max(si, axis=-1, keepdims=True) for si in s])
        p = [jnp.exp2((si - m).astype(BF16)) for si in s]
        o = functools.reduce(jnp.add, [_dot(pi, v[:, a:a + HEAD_PAD]) for pi, v in zip(p, v_refs)])
        one_lane = V_DIM if hd % 2 == 0 else 0
        outs[hd] = o * (1.0 / o[:, one_lane:one_lane + 1])
        if hd % 2 == 1:
            o_ref[r0:r0 + sub, (hd - 1) * V_DIM:(hd + 1) * V_DIM] = jnp.where(low, outs.pop(hd - 1), outs.pop(hd)).astype(BF16)


def _attn_call(q, segs, q_len, tq):
    rows, hp = q.shape
    tiles = q_len // tq
    n_seg = len(segs)
    k_specs = [pl.BlockSpec((ln, hp), lambda i: (i // tiles, 0)) for _, _, ln in segs]
    return pl.pallas_call(
        functools.partial(_attn_kernel, n_seg=n_seg),
        grid=(rows // tq,),
        in_specs=[pl.BlockSpec((tq, hp), lambda i: (i, 0))] + k_specs + k_specs,
        out_specs=pl.BlockSpec((tq, N_HEADS * V_DIM), lambda i: (i, 0)),
        out_shape=jax.ShapeDtypeStruct((rows, N_HEADS * V_DIM), BF16),
        compiler_params=_cparams(("arbitrary",)),
        name="attention",
    )(q, *[k for k, _, _ in segs], *[v for _, v, _ in segs])


def _fourier_kernel(u_ref, cs_ref, dft_ref, o_ref, ab_ref, *, seq_len):
    @pl.when(pl.program_id(1) == 0)
    def _():
        ab = _dot(u_ref[...], cs_ref[...])
        ab_ref[:seq_len, :] = ab[:, :F_WIDTH].astype(BF16)
        ab_ref[seq_len:, :] = ab[:, F_WIDTH:].astype(BF16)

    o_ref[...] = _dot(dft_ref[...], ab_ref[...]).astype(BF16)


def _fourier_call(uf, cs, dft, seq_len, tmf):
    rows = uf.shape[0]
    tiles = seq_len // tmf
    return pl.pallas_call(
        functools.partial(_fourier_kernel, seq_len=seq_len),
        grid=(rows // seq_len, tiles),
        in_specs=[
            pl.BlockSpec((seq_len, F_WIDTH), lambda b, m: (b, 0)),
            _resident(cs.shape),
            pl.BlockSpec((tmf, 2 * seq_len), lambda b, m: (m, 0)),
        ],
        out_specs=pl.BlockSpec((tmf, F_WIDTH), lambda b, m: (b * tiles + m, 0)),
        out_shape=jax.ShapeDtypeStruct((rows, F_WIDTH), BF16),
        scratch_shapes=[pltpu.VMEM((2 * seq_len, F_WIDTH), BF16)],
        compiler_params=_cparams(("arbitrary", "arbitrary")),
        name="fourier",
    )(uf, cs, dft)


_RNN_HALF = RNN_WIDTH // 2
_RNN_CHUNK = 256


def _rnn_kernel(urc_ref, urx_ref, gugc_ref, gugx_ref, cw_ref, cb_ref, wg_ref, bg_ref, lam_ref,
                yc_ref, yx_ref, pad_ref, af_ref, bf_ref, ab_ref, bb_ref, *, ctx_len, seq_len):
    half = _RNN_HALF
    row8 = lax.broadcasted_iota(jnp.int32, (SUBLANES, half), 0)
    zeros8 = jnp.zeros((SUBLANES, half), F32)

    def gates(ur_ref, n, j):
        lo = j * half
        pad_ref[0:SUBLANES, :] = zeros8
        pad_ref[SUBLANES:SUBLANES + n, :] = ur_ref[:, lo:lo + half]
        pad_ref[SUBLANES + n:2 * SUBLANES + n, :] = zeros8
        cw = cw_ref[:, lo:lo + half]
        cb = cb_ref[:, lo:lo + half]
        lam = lam_ref[:, lo:lo + half]
        k = (-0.5 * LRU_C * math.log2(math.e)) * (jnp.maximum(-lam, 0.0) + jnp.log1p(jnp.exp(-jnp.abs(lam))))
        wg = wg_ref[j]
        bg = bg_ref[j]
        ext_rows = _RNN_CHUNK + 2 * SUBLANES

        def chunk(c, carry):
            r0 = pl.multiple_of(c * _RNN_CHUNK, _RNN_CHUNK)
            ext = pad_ref[pl.ds(r0, ext_rows), :]
            sl = slice(SUBLANES, SUBLANES + _RNN_CHUNK)
            uc = (cb + cw[0:1] * pltpu.roll(ext, 2, axis=0)[sl] + cw[1:2] * pltpu.roll(ext, 1, axis=0)[sl]
                  + cw[2:3] * ext[sl] + cw[3:4] * pltpu.roll(ext, ext_rows - 1, axis=0)[sl])
            g = _dot(uc.astype(BF16), wg) + bg
            uch = 0.5 * uc
            for d, (a_ref, b_ref) in enumerate(((af_ref, bf_ref), (ab_ref, bb_ref))):
                kd = k[d:d + 1]
                a = jnp.exp2(kd + kd * jnp.tanh(g[:, (2 * d) * half:(2 * d + 1) * half]))
                a_ref[pl.ds(r0, _RNN_CHUNK), :] = a
                om = 1.0 - a * a
                sq = om * lax.rsqrt(jnp.maximum(om, 1e-30))
                b_ref[pl.ds(r0, _RNN_CHUNK), :] = (sq * uch) * (1.0 + jnp.tanh(g[:, (2 * d + 1) * half:(2 * d + 2) * half]))
            return carry

        lax.fori_loop(0, n // _RNN_CHUNK, chunk, 0)

    def scan(n, hf, hb):
        nb = n // SUBLANES

        def block(i, carry):
            hf, hb = carry
            rf = pl.multiple_of(i * SUBLANES, SUBLANES)
            a = af_ref[pl.ds(rf, SUBLANES), :]
            b = bf_ref[pl.ds(rf, SUBLANES), :]
            for d in (1, 2, 4):
                ok = row8 >= d
                b = jnp.where(ok, a * pltpu.roll(b, d, axis=0) + b, b)
                a = jnp.where(ok, a * pltpu.roll(a, d, axis=0), a)
            h = a * hf + b
            bf_ref[pl.ds(rf, SUBLANES), :] = h
            hf = jnp.broadcast_to(h[SUBLANES - 1:SUBLANES, :], h.shape)

            rb = pl.multiple_of((nb - 1 - i) * SUBLANES, SUBLANES)
            a = ab_ref[pl.ds(rb, SUBLANES), :]
            b = bb_ref[pl.ds(rb, SUBLANES), :]
            for d in (1, 2, 4):
                ok = row8 < SUBLANES - d
                b = jnp.where(ok, a * pltpu.roll(b, SUBLANES - d, axis=0) + b, b)
                a = jnp.where(ok, a * pltpu.roll(a, SUBLANES - d, axis=0), a)
            h = a * hb + b
            bb_ref[pl.ds(rb, SUBLANES), :] = h
            hb = jnp.broadcast_to(h[0:1, :], h.shape)
            return hf, hb

        return lax.fori_loop(0, nb, block, (hf, hb), unroll=2)

    for j in range(RNN_WIDTH // half):
        lo = j * half
        gates(urc_ref, ctx_len, j)
        hf, hb = scan(ctx_len, zeros8, zeros8)
        yc_ref[:, lo:lo + half] = ((bf_ref[0:ctx_len, :] + bb_ref[0:ctx_len, :])
                                   * gugc_ref[:, lo:lo + half].astype(F32)).astype(BF16)
        gates(urx_ref, seq_len, j)
        scan(seq_len, hf, hb)
        yx_ref[:, lo:lo + half] = ((bf_ref[0:seq_len, :] + bb_ref[0:seq_len, :])
                                   * gugx_ref[:, lo:lo + half].astype(F32)).astype(BF16)


def _rnn_call(ur_c, ur_x, gug_c, gug_x, cw, cb, wg, bg, lam, ctx_len, seq_len):
    batch = ur_x.shape[0] // seq_len
    w = RNN_WIDTH

    def seq_spec(n):
        return pl.BlockSpec((n, w), lambda b: (b, 0))

    return pl.pallas_call(
        functools.partial(_rnn_kernel, ctx_len=ctx_len, seq_len=seq_len),
        grid=(batch,),
        in_specs=[seq_spec(ctx_len), seq_spec(seq_len), seq_spec(ctx_len), seq_spec(seq_len),
                  _resident(cw.shape), _resident(cb.shape), _resident(wg.shape), _resident(bg.shape),
                  _resident(lam.shape)],
        out_specs=(seq_spec(ctx_len), seq_spec(seq_len)),
        out_shape=(jax.ShapeDtypeStruct((batch * ctx_len, w), BF16),
                   jax.ShapeDtypeStruct((batch * seq_len, w), BF16)),
        scratch_shapes=[pltpu.VMEM((seq_len + 2 * SUBLANES, _RNN_HALF), F32)]
        + [pltpu.VMEM((seq_len, _RNN_HALF), F32)] * 4,
        compiler_params=_cparams(("arbitrary",)),
        name="rglru",
    )(ur_c, ur_x, gug_c, gug_x, cw, cb, wg, bg, lam)


def _merge_kernel(x_ref, att_ref, yf_ref, yr_ref, g_ref, gate_ref, woa_ref, wof_ref, wor_ref, wout_ref, o_ref):
    d = D_MODEL
    m = (g_ref[:, 0:d].astype(F32) * _dot(att_ref[...], woa_ref[...])
         + g_ref[:, d:2 * d].astype(F32) * _dot(yf_ref[...], wof_ref[...])
         + g_ref[:, 2 * d:3 * d].astype(F32) * _dot(yr_ref[...], wor_ref[...]))
    o_ref[...] = x_ref[...] + gate_ref[...] * _dot(m.astype(BF16), wout_ref[...])


def _merge_call(x2d, att, yf, yr, g, mod_l, woa, wof, wor, wout, mod_row, tm):
    rows, d = x2d.shape

    def row_spec(width):
        return pl.BlockSpec((tm, width), lambda i: (i, 0))

    return pl.pallas_call(
        _merge_kernel,
        grid=(rows // tm,),
        in_specs=[row_spec(d), row_spec(att.shape[1]), row_spec(yf.shape[1]), row_spec(yr.shape[1]),
                  row_spec(3 * d),
                  pl.BlockSpec((None, None, 1, d), lambda i: (mod_row(i), 2, 0, 0)),
                  _resident(woa.shape), _resident(wof.shape), _resident(wor.shape), _resident(wout.shape)],
        out_specs=row_spec(d),
        out_shape=jax.ShapeDtypeStruct((rows, d), F32),
        compiler_params=_cparams(("arbitrary",)),
        name="merge",
    )(x2d, att, yf, yr, g, mod_l, woa, wof, wor, wout)


_FF_CHUNK = 256


def _ffn_kernel(x_ref, xp_ref, xn_ref, g_ref, shift_ref, scale_ref, gate_ref,
                wu_ref, cw_ref, cb_ref, wd_ref, fin_ref, o_ref, h_ref, u0_ref, u1_ref, *, tm, tiles_per_seq, final_norm):
    i = pl.program_id(0)
    ext = tm + 2 * SUBLANES
    n_chunks = D_FF // _FF_CHUNK
    fc = _FF_CHUNK

    def prep(x):
        return (_rms(x, g_ref[...]) * (1.0 + scale_ref[...]) + shift_ref[...]).astype(BF16)

    first = (i % tiles_per_seq) == 0
    last = (i % tiles_per_seq) == tiles_per_seq - 1
    hp = prep(xp_ref[...])
    hn = prep(xn_ref[...])
    h_ref[0:SUBLANES, :] = jnp.where(first, jnp.zeros_like(hp), hp)
    h_ref[SUBLANES:SUBLANES + tm, :] = prep(x_ref[...])
    h_ref[SUBLANES + tm:, :] = jnp.where(last, jnp.zeros_like(hn), hn)

    sl = slice(SUBLANES, SUBLANES + tm)

    def cols(c):
        return pl.ds(pl.multiple_of(c * fc, fc), fc)

    hm = tm // 2
    hme = hm + 2 * SUBLANES
    slh = slice(SUBLANES, SUBLANES + hm)

    def conv(u, cw, cb):
        return cb + cw[0:1] * pltpu.roll(u, 1, axis=0)[slh] + cw[1:2] * u[slh] + cw[2:3] * pltpu.roll(u, hme - 1, axis=0)[slh]

    def act_down(c, u_ref, r):
        rows = slice(r * hm, r * hm + hme)
        up = conv(u_ref[rows, 0:fc], cw_ref[:, cols(c)], cb_ref[:, cols(c)])
        gt = conv(u_ref[rows, fc:2 * fc], cw_ref[:, cols(n_chunks + c)], cb_ref[:, cols(n_chunks + c)])
        o_ref[r * hm:(r + 1) * hm, :] += _dot((_gelu(gt) * up).astype(BF16), wd_ref[cols(c), :])

    u_bufs = (u0_ref, u1_ref)

    def step(t, parity, do_up=True, do_down=True):
        for r in range(2):
            if do_up:
                u_bufs[parity][:, r * fc:(r + 1) * fc] = _dot(h_ref[...], wu_ref[:, cols(r * n_chunks + t)])
            if do_down:
                act_down(t - 1, u_bufs[1 - parity], r)

    assert n_chunks % 2 == 1
    o_ref[...] = jnp.zeros_like(o_ref)
    step(0, 0, do_down=False)

    def body(k, carry):
        t = 2 * k + 1
        step(t, 1)
        step(t + 1, 0)
        return carry

    lax.fori_loop(0, n_chunks // 2, body, 0)
    step(n_chunks, 1, do_up=False)
    y = x_ref[...] + gate_ref[...] * o_ref[...]
    if final_norm:
        y = _rms(y, fin_ref[...])
    o_ref[...] = y


def _ffn_call(x2d, mod_l, norm_g, wu, cw, cb, wd, fin_g, seq_len, mod_row, tm, final_norm):
    rows, d = x2d.shape
    tiles_per_seq = seq_len // tm
    blk8 = tm // SUBLANES
    n_blk8 = rows // SUBLANES

    def mod_spec(chunk):
        return pl.BlockSpec((None, None, 1, d), lambda i: (mod_row(i), chunk, 0, 0))

    return pl.pallas_call(
        functools.partial(_ffn_kernel, tm=tm, tiles_per_seq=tiles_per_seq, final_norm=final_norm),
        grid=(rows // tm,),
        in_specs=[
            pl.BlockSpec((tm, d), lambda i: (i, 0)),
            pl.BlockSpec((SUBLANES, d), lambda i: (jnp.maximum(i * blk8 - 1, 0), 0)),
            pl.BlockSpec((SUBLANES, d), lambda i: (jnp.minimum((i + 1) * blk8, n_blk8 - 1), 0)),
            _resident((1, d)),
            mod_spec(3), mod_spec(4), mod_spec(5),
            _resident(wu.shape), _resident(cw.shape), _resident(cb.shape), _resident(wd.shape),
            _resident((1, d)),
        ],
        out_specs=pl.BlockSpec((tm, d), lambda i: (i, 0)),
        out_shape=jax.ShapeDtypeStruct((rows, d), F32),
        scratch_shapes=[pltpu.VMEM((tm + 2 * SUBLANES, d), BF16),
                        pltpu.VMEM((tm + 2 * SUBLANES, 2 * _FF_CHUNK), F32),
                        pltpu.VMEM((tm + 2 * SUBLANES, 2 * _FF_CHUNK), F32)],
        compiler_params=_cparams(("arbitrary",), flags=_INTERLEAVE),
        name="conv_ffn",
    )(x2d, x2d, x2d, norm_g, mod_l, mod_l, mod_l, wu, cw, cb, wd, fin_g)


def _prep_ffn(w_up, cw, cb, w_down):
    return w_up.astype(BF16), cw, cb[None, :], w_down.astype(BF16)


def _rope_tables(seq_len, rope):
    scale = QK_DIM ** -0.5 * math.log2(math.e)
    if rope:
        n_rows = seq_len // GRID_W
        row = jnp.repeat(jnp.arange(n_rows, dtype=F32), GRID_W)
        col = jnp.tile(jnp.arange(GRID_W, dtype=F32), n_rows)
        n_freq = QK_ROPE // 4
        inv = ROPE_BASE ** (-jnp.arange(n_freq, dtype=F32) / n_freq)
        ang = jnp.concatenate([row[:, None] * inv, col[:, None] * inv], axis=-1)
        cos, sin = jnp.cos(ang), jnp.sin(ang)
    else:
        cos = jnp.ones((seq_len, QK_ROPE // 2), F32)
        sin = jnp.zeros((seq_len, QK_ROPE // 2), F32)
    ones = jnp.ones((seq_len, QK_NOPE), F32)
    zeros = jnp.zeros((seq_len, QK_NOPE), F32)
    tail = jnp.zeros((seq_len, HEAD_PAD - QK_DIM), F32)
    cos2 = jnp.concatenate([cos, cos], axis=-1)
    sin2 = jnp.concatenate([-sin, sin], axis=-1)
    cq = scale * jnp.concatenate([ones, cos2, tail], axis=-1)
    sq = scale * jnp.concatenate([zeros, sin2, tail], axis=-1)
    ck = jnp.concatenate([zeros, cos2, tail], axis=-1)
    sk = jnp.concatenate([zeros, sin2, tail], axis=-1)
    return cq, sq, ck, sk


def _value_ones():
    lane = np.arange(N_HEADS * HEAD_PAD) % HEAD_PAD
    head = np.arange(N_HEADS * HEAD_PAD) // HEAD_PAD
    return jnp.asarray(np.where(head % 2 == 0, lane == V_DIM, lane == 0)[None, :], F32)


def _swap_halves(w):
    half = w.shape[-1] // 2
    return jnp.concatenate([w[..., half:], w[..., :half]], axis=-1)


def _prep_w_in(w):
    d = w.shape[0]
    n_front = Q_LORA + KV_LORA
    kr = w[:, n_front:n_front + QK_ROPE]
    z_lo = jnp.zeros((d, QK_NOPE), w.dtype)
    z_hi = jnp.zeros((d, HEAD_PAD - QK_DIM), w.dtype)
    w_kr = jnp.concatenate([z_lo, kr, z_hi, z_lo, _swap_halves(kr), z_hi], axis=-1)
    return w[:, :n_front].astype(BF16), w_kr.astype(BF16), w[:, n_front + QK_ROPE:].astype(BF16)


def _prep_w_uq(w):
    r, h, _ = w.shape
    pad = jnp.zeros((r, h, HEAD_PAD - QK_DIM), w.dtype)
    y1 = jnp.concatenate([w, pad], axis=-1)
    y2 = jnp.concatenate([jnp.zeros((r, h, QK_NOPE), w.dtype), _swap_halves(w[..., QK_NOPE:]), pad], axis=-1)
    return jnp.concatenate([y1.reshape(r, h * HEAD_PAD), y2.reshape(r, h * HEAD_PAD)], axis=-1).astype(BF16)


def _prep_w_ukv(w):
    r, h, _ = w.shape
    k = jnp.concatenate([w[..., :QK_NOPE], jnp.zeros((r, h, HEAD_PAD - QK_NOPE), w.dtype)], axis=-1)
    v = w[..., QK_NOPE:]
    z = jnp.zeros_like(v)
    odd = (jnp.arange(h) % 2 == 1)[None, :, None]
    vp = jnp.concatenate([jnp.where(odd, z, v), jnp.where(odd, v, z)], axis=-1)
    return jnp.concatenate([k.reshape(r, h * HEAD_PAD), vp.reshape(r, h * HEAD_PAD)], axis=-1).astype(BF16)


def _prep_gates(w_a, b_a, w_x, b_x):
    half = _RNN_HALF
    heads_per_half = half // RNN_BLOCK

    def blockdiag(w):
        eye = jnp.eye(heads_per_half, dtype=w.dtype)
        return jnp.einsum('hij,hk->hikj', w, eye).reshape(half, half)

    ws, bs = [], []
    for j in range(RNN_WIDTH // half):
        hs = slice(j * heads_per_half, (j + 1) * heads_per_half)
        ls = slice(j * half, (j + 1) * half)
        ws.append(jnp.concatenate([blockdiag(w_a[0, hs]), blockdiag(w_x[0, hs]),
                                   blockdiag(w_a[1, hs]), blockdiag(w_x[1, hs])], axis=-1))
        bs.append(jnp.concatenate([b_a[0, ls], b_x[0, ls], b_a[1, ls], b_x[1, ls]], axis=-1)[None, :])
    return (0.5 * jnp.stack(ws)).astype(BF16), 0.5 * jnp.stack(bs)


def _dft_tables(seq_len):
    lo_n = 32 if seq_len % 32 == 0 else 1
    t = jnp.arange(seq_len, dtype=jnp.int32)

    def table(rows):
        ang = (2.0 * math.pi / seq_len) * ((rows[:, None] * t[None, :]) % seq_len).astype(F32)
        return jnp.cos(ang), jnp.sin(ang)

    c_hi, s_hi = table(jnp.arange(seq_len // lo_n, dtype=jnp.int32) * lo_n)
    c_lo, s_lo = table(jnp.arange(lo_n, dtype=jnp.int32))
    s = seq_len ** -0.5
    cos = (c_hi[:, None, :] * c_lo[None, :, :] - s_hi[:, None, :] * s_lo[None, :, :]).reshape(seq_len, seq_len)
    sin = (s_hi[:, None, :] * c_lo[None, :, :] + c_hi[:, None, :] * s_lo[None, :, :]).reshape(seq_len, seq_len)
    return jnp.concatenate([cos * s, sin * s], axis=-1).astype(BF16)


def _channel_dft():
    k = jnp.arange(F_GROUP_W, dtype=jnp.int32)
    ang = (2.0 * math.pi / F_GROUP_W) * ((k[:, None] * k[None, :]) % F_GROUP_W).astype(F32)
    s = F_GROUP_W ** -0.5
    eye = jnp.eye(F_GROUPS, dtype=F32)
    c = jnp.kron(eye, jnp.cos(ang) * s)
    sn = jnp.kron(eye, jnp.sin(ang) * s)
    return jnp.concatenate([c, -sn], axis=-1).astype(BF16)


def kernel(x, c, ctx, c_ctx, w_ada, b_ada, norm_mix, norm_ffn, w_in, q_norm, kv_norm, w_uq, w_ukv, w_o_attn,
           w_o_fourier, rnn_conv_w, rnn_conv_b, rg_w_a, rg_b_a, rg_w_x, rg_b_x, rg_lambda, w_o_rnn, w_out,
           w_up, ffn_conv_w, ffn_conv_b, w_down, final_norm):
    batch, seq_len, d = x.shape
    ctx_len = ctx.shape[1]
    depth = w_ada.shape[0]
    assert batch + 1 <= MOD_ROWS and d == D_MODEL

    cc = jnp.concatenate([c, c_ctx[None, :], jnp.zeros((MOD_ROWS - batch - 1, d), F32)], axis=0)
    mod = _ada_call(cc, w_ada, b_ada).reshape(depth, MOD_ROWS, N_MOD, 1, d)

    tabs_x = _rope_tables(seq_len, True)
    tabs_c = _rope_tables(ctx_len, False)
    dft_x = _dft_tables(seq_len)
    dft_c = _dft_tables(ctx_len)
    cs = _channel_dft()

    tm_in = min(512, seq_len)
    tm_x = 256
    tm_c = min(256, ctx_len)
    tq = min(512, seq_len)
    tm_ffn = min(1024, seq_len)

    def x_row(tm):
        return lambda i: i // (seq_len // tm)

    c_row = lambda i: batch

    xs = x.reshape(batch * seq_len, d)
    cs_tok = ctx.reshape(batch * ctx_len, d)

    for l in range(depth):
        last = l == depth - 1
        mod_l = mod[l]
        w_in_p = _prep_w_in(w_in[l])
        wuq = _prep_w_uq(w_uq[l])
        wukv = _prep_w_ukv(w_ukv[l])
        wg, bg = _prep_gates(rg_w_a[l], rg_b_a[l], rg_w_x[l], rg_b_x[l])
        nm = norm_mix[l][None, :]
        nf = norm_ffn[l][None, :]
        qn = q_norm[l][None, :]
        kvn = kv_norm[l][None, :]
        woa = w_o_attn[l].astype(BF16)
        wof = w_o_fourier[l].astype(BF16)
        wor = w_o_rnn[l].astype(BF16)
        wout = w_out[l].astype(BF16)
        wup, fcw, fcb, wdn = _prep_ffn(w_up[l], ffn_conv_w[l], ffn_conv_b[l], w_down[l])
        rcw = rnn_conv_w[l]
        rcb = rnn_conv_b[l][None, :]
        fin = final_norm[None, :]

        q_x, k_x, v_x, uf_x, ur_x, gug_x, g_x = _in_call(xs, mod_l, nm, w_in_p, qn, kvn, wuq, wukv, tabs_x,
                                                          seq_len, x_row(tm_in), tm_in)
        q_c, k_c, v_c, uf_c, ur_c, gug_c, g_c = _in_call(cs_tok, mod_l, nm, w_in_p, qn, kvn, wuq, wukv, tabs_c,
                                                          ctx_len, c_row, tm_c)

        att_x = _attn_call(q_x, [(k_c, v_c, ctx_len), (k_x, v_x, seq_len)], seq_len, tq)
        yf_x = _fourier_call(uf_x, cs, dft_x, seq_len, min(512, seq_len))
        yr_c, yr_x = _rnn_call(ur_c, ur_x, gug_c, gug_x, rcw, rcb, wg, bg, rg_lambda[l], ctx_len, seq_len)

        xs = _merge_call(xs, att_x, yf_x, yr_x, g_x, mod_l, woa, wof, wor, wout, x_row(tm_x), tm_x)
        xs = _ffn_call(xs, mod_l, nf, wup, fcw, fcb, wdn, fin, seq_len, x_row(tm_ffn), tm_ffn, last)

        if not last:
            att_c = _attn_call(q_c, [(k_c, v_c, ctx_len)], ctx_len, min(_ATTN_SUB_ROWS, ctx_len))
            yf_c = _fourier_call(uf_c, cs, dft_c, ctx_len, ctx_len)
            cs_tok = _merge_call(cs_tok, att_c, yf_c, yr_c, g_c, mod_l, woa, wof, wor, wout, c_row, tm_c)
            cs_tok = _ffn_call(cs_tok, mod_l, nf, wup, fcw, fcb, wdn, fin, ctx_len, c_row, ctx_len, False)

    return xs.reshape(batch, seq_len, d)
```

```python
import functools
import math

import jax
import jax.numpy as jnp
import numpy as np
from jax import lax
from jax.experimental import pallas as pl
from jax.experimental.pallas import tpu as pltpu

F32 = jnp.float32
BF16 = jnp.bfloat16

D_MODEL = 1024
GRID_W = 64
N_HEADS = 8
Q_LORA = 384
KV_LORA = 256
QK_NOPE = 64
QK_ROPE = 32
V_DIM = 64
QK_DIM = QK_NOPE + QK_ROPE
ROPE_BASE = 10000.0
F_GROUPS = 4
F_GROUP_W = 128
F_WIDTH = F_GROUPS * F_GROUP_W
RNN_WIDTH = 512
RNN_HEADS = 8
RNN_BLOCK = RNN_WIDTH // RNN_HEADS
RNN_CONV = 4
LRU_C = 8.0
D_FF = 2816
FFN_CONV = 3
N_MOD = 6
EPS = 1e-6

LANES = 128
SUBLANES = 8
HEAD_PAD = 128
MOD_ROWS = 16

_O_UF = 0
_O_UR = _O_UF + F_WIDTH
_O_UG = _O_UR + RNN_WIDTH
_O_GL = _O_UG + RNN_WIDTH

_VMEM_LIMIT = 56 * 2 ** 20


_INTERLEAVE = None


def _cparams(sem, vmem=_VMEM_LIMIT, flags=None):
    return pltpu.CompilerParams(dimension_semantics=sem, vmem_limit_bytes=vmem, flags=flags)


def _resident(shape):
    nd = len(shape)
    return pl.BlockSpec(shape, lambda *_: (0,) * nd, pipeline_mode=pl.Buffered(1))


def _dot(a, b):
    return jnp.dot(a, b, preferred_element_type=F32)


def _sigmoid(x):
    return 0.5 * (1.0 + jnp.tanh(0.5 * x))


def _gelu(x):
    return 0.5 * x * (1.0 + jnp.tanh(math.sqrt(2.0 / math.pi) * (x + 0.044715 * (x * x * x))))


def _rms(x, g):
    return x * lax.rsqrt(jnp.mean(x * x, axis=-1, keepdims=True) + EPS) * g


def _ada_kernel(cc_ref, w_ref, b_ref, o_ref):
    cc = cc_ref[...]
    s = cc * _sigmoid(cc)
    o_ref[...] = _dot(s.astype(BF16), w_ref[...].astype(BF16)) + b_ref[...]


def _ada_call(cc, w_ada, b_ada):
    depth, d, n = w_ada.shape
    tn = 1536
    return pl.pallas_call(
        _ada_kernel,
        grid=(depth, n // tn),
        in_specs=[
            pl.BlockSpec((MOD_ROWS, d), lambda l, j: (0, 0)),
            pl.BlockSpec((None, d, tn), lambda l, j: (l, 0, j)),
            pl.BlockSpec((None, 1, tn), lambda l, j: (l, 0, j)),
        ],
        out_specs=pl.BlockSpec((None, MOD_ROWS, tn), lambda l, j: (l, 0, j)),
        out_shape=jax.ShapeDtypeStruct((depth, MOD_ROWS, n), F32),
        compiler_params=_cparams(("arbitrary", "arbitrary")),
        name="adaln",
    )(cc, w_ada, b_ada.reshape(depth, 1, n))


def _in_kernel(x_ref, g_ref, shift_ref, scale_ref, wf_ref, wkr_ref, w_ref, qn_ref, kvn_ref, wuq_ref, wukv_ref,
               cq_ref, sq_ref, ck_ref, sk_ref, vone_ref,
               q_out, k_out, v_out, uf_out, ur_out, gug_out, g_out):
    h = _rms(x_ref[...], g_ref[...]) * (1.0 + scale_ref[...]) + shift_ref[...]
    hb = h.astype(BF16)
    hp = N_HEADS * HEAD_PAD
    first_half = lax.broadcasted_iota(jnp.int32, (x_ref.shape[0], HEAD_PAD), 1) < QK_NOPE + QK_ROPE // 2

    def swap_rot(y):
        fwd = pltpu.roll(y, QK_ROPE // 2, axis=1)
        bwd = pltpu.roll(y, y.shape[1] - QK_ROPE // 2, axis=1)
        return [jnp.where(first_half, bwd[:, a:a + HEAD_PAD], fwd[:, a:a + HEAD_PAD])
                for a in range(0, y.shape[1], HEAD_PAD)]

    cqn = _rms(_dot(hb, wf_ref[:, 0:Q_LORA]), qn_ref[...]).astype(BF16)
    y = _dot(cqn, wuq_ref[...])
    y_sw = swap_rot(y)
    cq_t = cq_ref[...]
    sq_t = sq_ref[...]
    for hd in range(N_HEADS):
        a = hd * HEAD_PAD
        q_out[:, a:a + HEAD_PAD] = (y[:, a:a + HEAD_PAD] * cq_t + y_sw[hd] * sq_t).astype(BF16)

    ckvn = _rms(_dot(hb, wf_ref[:, Q_LORA:Q_LORA + KV_LORA]), kvn_ref[...]).astype(BF16)
    kv = _dot(ckvn, wukv_ref[...])
    kr = _dot(hb, wkr_ref[...])
    k_rope = kr * ck_ref[...] + swap_rot(kr)[0] * sk_ref[...]
    for hd in range(N_HEADS):
        a = hd * HEAD_PAD
        k_out[:, a:a + HEAD_PAD] = (kv[:, a:a + HEAD_PAD] + k_rope).astype(BF16)
    v_out[...] = (kv[:, hp:] + vone_ref[...]).astype(BF16)

    uf_out[...] = _dot(hb, w_ref[:, _O_UF:_O_UF + F_WIDTH]).astype(BF16)
    ur_out[...] = _dot(hb, w_ref[:, _O_UR:_O_UR + RNN_WIDTH])
    gug_out[...] = _gelu(_dot(hb, w_ref[:, _O_UG:_O_UG + RNN_WIDTH])).astype(BF16)
    for j in range(3):
        a = j * D_MODEL
        g_out[:, a:a + D_MODEL] = _sigmoid(_dot(hb, w_ref[:, _O_GL + a:_O_GL + a + D_MODEL])).astype(BF16)


def _in_call(x2d, mod_l, norm_g, w_in_p, qn, kvn, wuq, wukv, tabs, seq_len, mod_row, tm):
    rows, d = x2d.shape
    tiles_per_seq = seq_len // tm
    hp = N_HEADS * HEAD_PAD

    def row_spec(width):
        return pl.BlockSpec((tm, width), lambda i: (i, 0))

    def mod_spec(chunk):
        return pl.BlockSpec((None, None, 1, d), lambda i: (mod_row(i), chunk, 0, 0))

    tab_spec = pl.BlockSpec((tm, HEAD_PAD), lambda i: (i % tiles_per_seq, 0))
    out_shapes = (
        jax.ShapeDtypeStruct((rows, hp), BF16),
        jax.ShapeDtypeStruct((rows, hp), BF16),
        jax.ShapeDtypeStruct((rows, hp), BF16),
        jax.ShapeDtypeStruct((rows, F_WIDTH), BF16),
        jax.ShapeDtypeStruct((rows, RNN_WIDTH), F32),
        jax.ShapeDtypeStruct((rows, RNN_WIDTH), BF16),
        jax.ShapeDtypeStruct((rows, 3 * d), BF16),
    )
    return pl.pallas_call(
        _in_kernel,
        grid=(rows // tm,),
        in_specs=[
            row_spec(d),
            _resident((1, d)),
            mod_spec(0),
            mod_spec(1),
            _resident(w_in_p[0].shape), _resident(w_in_p[1].shape), _resident(w_in_p[2].shape),
            _resident((1, Q_LORA)),
            _resident((1, KV_LORA)),
            _resident(wuq.shape),
            _resident(wukv.shape),
            tab_spec, tab_spec, tab_spec, tab_spec,
            _resident((1, hp)),
        ],
        out_specs=tuple(row_spec(s.shape[1]) for s in out_shapes),
        out_shape=out_shapes,
        compiler_params=_cparams(("arbitrary",)),
        name="in_proj",
    )(x2d, norm_g, mod_l, mod_l, *w_in_p, qn, kvn, wuq, wukv, *tabs, _value_ones())


_ATTN_SUB_ROWS = 256


def _attn_kernel(*refs, n_seg):
    q_ref = refs[0]
    k_refs = refs[1:1 + n_seg]
    v_refs = refs[1 + n_seg:1 + 2 * n_seg]
    o_ref = refs[1 + 2 * n_seg]
    nt = (((1,), (1,)), ((), ()))
    sub = _ATTN_SUB_ROWS
    low = lax.broadcasted_iota(jnp.int32, (sub, HEAD_PAD), 1) < V_DIM

    def scores(item):
        r0, hd = item
        a = hd * HEAD_PAD
        qh = q_ref[r0:r0 + sub, a:a + HEAD_PAD]
        return [lax.dot_general(qh, k[:, a:a + HEAD_PAD], nt, preferred_element_type=F32) for k in k_refs]

    items = [(r0, hd) for r0 in range(0, q_ref.shape[0], sub) for hd in range(N_HEADS)]
    outs = {}
    s_next = scores(items[0])
    for idx, (r0, hd) in enumerate(items):
        a = hd * HEAD_PAD
        s = s_next
        if idx + 1 < len(items):
            s_next = scores(items[idx + 1])
        m = functools.reduce(jnp.maximum, [jnp.max(si, axis=-1, keepdims=True) for si in s])
        p = [jnp.exp2((si - m).astype(BF16)) for si in s]
        o = functools.reduce(jnp.add, [_dot(pi, v[:, a:a + HEAD_PAD]) for pi, v in zip(p, v_refs)])
        one_lane = V_DIM if hd % 2 == 0 else 0
        outs[hd] = o * (1.0 / o[:, one_lane:one_lane + 1])
        if hd % 2 == 1:
            o_ref[r0:r0 + sub, (hd - 1) * V_DIM:(hd + 1) * V_DIM] = jnp.where(low, outs.pop(hd - 1), outs.pop(hd)).astype(BF16)


def _attn_call(q, segs, q_len, tq):
    rows, hp = q.shape
    tiles = q_len // tq
    n_seg = len(segs)
    k_specs = [pl.BlockSpec((ln, hp), lambda i: (i // tiles, 0)) for _, _, ln in segs]
    return pl.pallas_call(
        functools.partial(_attn_kernel, n_seg=n_seg),
        grid=(rows // tq,),
        in_specs=[pl.BlockSpec((tq, hp), lambda i: (i, 0))] + k_specs + k_specs,
        out_specs=pl.BlockSpec((tq, N_HEADS * V_DIM), lambda i: (i, 0)),
        out_shape=jax.ShapeDtypeStruct((rows, N_HEADS * V_DIM), BF16),
        compiler_params=_cparams(("arbitrary",)),
        name="attention",
    )(q, *[k for k, _, _ in segs], *[v for _, v, _ in segs])


def _fourier_kernel(u_ref, cs_ref, dft_ref, o_ref, ab_ref, *, seq_len):
    @pl.when(pl.program_id(1) == 0)
    def _():
        ab = _dot(u_ref[...], cs_ref[...])
        ab_ref[:seq_len, :] = ab[:, :F_WIDTH].astype(BF16)
        ab_ref[seq_len:, :] = ab[:, F_WIDTH:].astype(BF16)

    o_ref[...] = _dot(dft_ref[...], ab_ref[...]).astype(BF16)


def _fourier_call(uf, cs, dft, seq_len, tmf):
    rows = uf.shape[0]
    tiles = seq_len // tmf
    return pl.pallas_call(
        functools.partial(_fourier_kernel, seq_len=seq_len),
        grid=(rows // seq_len, tiles),
        in_specs=[
            pl.BlockSpec((seq_len, F_WIDTH), lambda b, m: (b, 0)),
            _resident(cs.shape),
            pl.BlockSpec((tmf, 2 * seq_len), lambda b, m: (m, 0)),
        ],
        out_specs=pl.BlockSpec((tmf, F_WIDTH), lambda b, m: (b * tiles + m, 0)),
        out_shape=jax.ShapeDtypeStruct((rows, F_WIDTH), BF16),
        scratch_shapes=[pltpu.VMEM((2 * seq_len, F_WIDTH), BF16)],
        compiler_params=_cparams(("arbitrary", "arbitrary")),
        name="fourier",
    )(uf, cs, dft)


_RNN_HALF = RNN_WIDTH // 2
_RNN_CHUNK = 256


def _rnn_kernel(urc_ref, urx_ref, gugc_ref, gugx_ref, cw_ref, cb_ref, wg_ref, bg_ref, lam_ref,
                yc_ref, yx_ref, pad_ref, af_ref, bf_ref, ab_ref, bb_ref, *, ctx_len, seq_len):
    half = _RNN_HALF
    row8 = lax.broadcasted_iota(jnp.int32, (SUBLANES, half), 0)
    zeros8 = jnp.zeros((SUBLANES, half), F32)

    def gates(ur_ref, n, j):
        lo = j * half
        pad_ref[0:SUBLANES, :] = zeros8
        pad_ref[SUBLANES:SUBLANES + n, :] = ur_ref[:, lo:lo + half]
        pad_ref[SUBLANES + n:2 * SUBLANES + n, :] = zeros8
        cw = cw_ref[:, lo:lo + half]
        cb = cb_ref[:, lo:lo + half]
        lam = lam_ref[:, lo:lo + half]
        k = (-0.5 * LRU_C * math.log2(math.e)) * (jnp.maximum(-lam, 0.0) + jnp.log1p(jnp.exp(-jnp.abs(lam))))
        wg = wg_ref[j]
        bg = bg_ref[j]
        ext_rows = _RNN_CHUNK + 2 * SUBLANES

        def chunk(c, carry):
            r0 = pl.multiple_of(c * _RNN_CHUNK, _RNN_CHUNK)
            ext = pad_ref[pl.ds(r0, ext_rows), :]
            sl = slice(SUBLANES, SUBLANES + _RNN_CHUNK)
            uc = (cb + cw[0:1] * pltpu.roll(ext, 2, axis=0)[sl] + cw[1:2] * pltpu.roll(ext, 1, axis=0)[sl]
                  + cw[2:3] * ext[sl] + cw[3:4] * pltpu.roll(ext, ext_rows - 1, axis=0)[sl])
            g = _dot(uc.astype(BF16), wg) + bg
            uch = 0.5 * uc
            for d, (a_ref, b_ref) in enumerate(((af_ref, bf_ref), (ab_ref, bb_ref))):
                kd = k[d:d + 1]
                a = jnp.exp2(kd + kd * jnp.tanh(g[:, (2 * d) * half:(2 * d + 1) * half]))
                a_ref[pl.ds(r0, _RNN_CHUNK), :] = a
                om = 1.0 - a * a
                sq = om * lax.rsqrt(jnp.maximum(om, 1e-30))
                b_ref[pl.ds(r0, _RNN_CHUNK), :] = (sq * uch) * (1.0 + jnp.tanh(g[:, (2 * d + 1) * half:(2 * d + 2) * half]))
            return carry

        lax.fori_loop(0, n // _RNN_CHUNK, chunk, 0)

    def scan(n, hf, hb):
        nb = n // SUBLANES

        def block(i, carry):
            hf, hb = carry
            rf = pl.multiple_of(i * SUBLANES, SUBLANES)
            a = af_ref[pl.ds(rf, SUBLANES), :]
            b = bf_ref[pl.ds(rf, SUBLANES), :]
            for d in (1, 2, 4):
                ok = row8 >= d
                b = jnp.where(ok, a * pltpu.roll(b, d, axis=0) + b, b)
                a = jnp.where(ok, a * pltpu.roll(a, d, axis=0), a)
            h = a * hf + b
            bf_ref[pl.ds(rf, SUBLANES), :] = h
            hf = jnp.broadcast_to(h[SUBLANES - 1:SUBLANES, :], h.shape)

            rb = pl.multiple_of((nb - 1 - i) * SUBLANES, SUBLANES)
            a = ab_ref[pl.ds(rb, SUBLANES), :]
            b = bb_ref[pl.ds(rb, SUBLANES), :]
            for d in (1, 2, 4):
                ok = row8 < SUBLANES - d
                b = jnp.where(ok, a * pltpu.roll(b, SUBLANES - d, axis=0) + b, b)
                a = jnp.where(ok, a * pltpu.roll(a, SUBLANES - d, axis=0), a)
            h = a * hb + b
            bb_ref[pl.ds(rb, SUBLANES), :] = h
            hb = jnp.broadcast_to(h[0:1, :], h.shape)
            return hf, hb

        return lax.fori_loop(0, nb, block, (hf, hb), unroll=2)

    for j in range(RNN_WIDTH // half):
        lo = j * half
        gates(urc_ref, ctx_len, j)
        hf, hb = scan(ctx_len, zeros8, zeros8)
        yc_ref[:, lo:lo + half] = ((bf_ref[0:ctx_len, :] + bb_ref[0:ctx_len, :])
                                   * gugc_ref[:, lo:lo + half].astype(F32)).astype(BF16)
        gates(urx_ref, seq_len, j)
        scan(seq_len, hf, hb)
        yx_ref[:, lo:lo + half] = ((bf_ref[0:seq_len, :] + bb_ref[0:seq_len, :])
                                   * gugx_ref[:, lo:lo + half].astype(F32)).astype(BF16)


def _rnn_call(ur_c, ur_x, gug_c, gug_x, cw, cb, wg, bg, lam, ctx_len, seq_len):
    batch = ur_x.shape[0] // seq_len
    w = RNN_WIDTH

    def seq_spec(n):
        return pl.BlockSpec((n, w), lambda b: (b, 0))

    return pl.pallas_call(
        functools.partial(_rnn_kernel, ctx_len=ctx_len, seq_len=seq_len),
        grid=(batch,),
        in_specs=[seq_spec(ctx_len), seq_spec(seq_len), seq_spec(ctx_len), seq_spec(seq_len),
                  _resident(cw.shape), _resident(cb.shape), _resident(wg.shape), _resident(bg.shape),
                  _resident(lam.shape)],
        out_specs=(seq_spec(ctx_len), seq_spec(seq_len)),
        out_shape=(jax.ShapeDtypeStruct((batch * ctx_len, w), BF16),
                   jax.ShapeDtypeStruct((batch * seq_len, w), BF16)),
        scratch_shapes=[pltpu.VMEM((seq_len + 2 * SUBLANES, _RNN_HALF), F32)]
        + [pltpu.VMEM((seq_len, _RNN_HALF), F32)] * 4,
        compiler_params=_cparams(("arbitrary",)),
        name="rglru",
    )(ur_c, ur_x, gug_c, gug_x, cw, cb, wg, bg, lam)


def _merge_kernel(x_ref, att_ref, yf_ref, yr_ref, g_ref, gate_ref, woa_ref, wof_ref, wor_ref, wout_ref, o_ref):
    d = D_MODEL
    m = (g_ref[:, 0:d].astype(F32) * _dot(att_ref[...], woa_ref[...])
         + g_ref[:, d:2 * d].astype(F32) * _dot(yf_ref[...], wof_ref[...])
         + g_ref[:, 2 * d:3 * d].astype(F32) * _dot(yr_ref[...], wor_ref[...]))
    o_ref[...] = x_ref[...] + gate_ref[...] * _dot(m.astype(BF16), wout_ref[...])


def _merge_call(x2d, att, yf, yr, g, mod_l, woa, wof, wor, wout, mod_row, tm):
    rows, d = x2d.shape

    def row_spec(width):
        return pl.BlockSpec((tm, width), lambda i: (i, 0))

    return pl.pallas_call(
        _merge_kernel,
        grid=(rows // tm,),
        in_specs=[row_spec(d), row_spec(att.shape[1]), row_spec(yf.shape[1]), row_spec(yr.shape[1]),
                  row_spec(3 * d),
                  pl.BlockSpec((None, None, 1, d), lambda i: (mod_row(i), 2, 0, 0)),
                  _resident(woa.shape), _resident(wof.shape), _resident(wor.shape), _resident(wout.shape)],
        out_specs=row_spec(d),
        out_shape=jax.ShapeDtypeStruct((rows, d), F32),
        compiler_params=_cparams(("arbitrary",)),
        name="merge",
    )(x2d, att, yf, yr, g, mod_l, woa, wof, wor, wout)


_FF_CHUNK = 256


def _ffn_kernel(x_ref, xp_ref, xn_ref, g_ref, shift_ref, scale_ref, gate_ref,
                wu_ref, cw_ref, cb_ref, wd_ref, fin_ref, o_ref, h_ref, u0_ref, u1_ref, a_ref, *, tm, tiles_per_seq, final_norm):
    i = pl.program_id(0)
    ext = tm + 2 * SUBLANES
    n_chunks = D_FF // _FF_CHUNK
    fc = _FF_CHUNK

    def prep(x):
        return (_rms(x, g_ref[...]) * (1.0 + scale_ref[...]) + shift_ref[...]).astype(BF16)

    first = (i % tiles_per_seq) == 0
    last = (i % tiles_per_seq) == tiles_per_seq - 1
    hp = prep(xp_ref[...])
    hn = prep(xn_ref[...])
    h_ref[0:SUBLANES, :] = jnp.where(first, jnp.zeros_like(hp), hp)
    h_ref[SUBLANES:SUBLANES + tm, :] = prep(x_ref[...])
    h_ref[SUBLANES + tm:, :] = jnp.where(last, jnp.zeros_like(hn), hn)

    sl = slice(SUBLANES, SUBLANES + tm)

    def cols(c):
        return pl.ds(pl.multiple_of(c * fc, fc), fc)

    hm = tm // 2
    hme = hm + 2 * SUBLANES
    slh = slice(SUBLANES, SUBLANES + hm)

    def conv(u, cw, cb):
        return cb + cw[0:1] * pltpu.roll(u, 1, axis=0)[slh] + cw[1:2] * u[slh] + cw[2:3] * pltpu.roll(u, hme - 1, axis=0)[slh]

    def activate(c, u_ref, r):
        rows = slice(r * hm, r * hm + hme)
        up = conv(u_ref[rows, 0:fc], cw_ref[:, cols(c)], cb_ref[:, cols(c)])
        gt = conv(u_ref[rows, fc:2 * fc], cw_ref[:, cols(n_chunks + c)], cb_ref[:, cols(n_chunks + c)])
        a_ref[r * hm:(r + 1) * hm, cols(c)] = _gelu(gt.astype(BF16)) * up.astype(BF16)

    u_bufs = (u0_ref, u1_ref)

    def step(t, parity, do_up=True, do_act=True):
        for r in range(2):
            if do_up:
                u_bufs[parity][:, r * fc:(r + 1) * fc] = _dot(h_ref[...], wu_ref[:, cols(r * n_chunks + t)])
            if do_act:
                activate(t - 1, u_bufs[1 - parity], r)

    assert n_chunks % 2 == 1
    step(0, 0, do_act=False)

    def body(k, carry):
        t = 2 * k + 1
        step(t, 1)
        step(t + 1, 0)
        return carry

    lax.fori_loop(0, n_chunks // 2, body, 0)
    step(n_chunks, 1, do_up=False)
    y = x_ref[...] + gate_ref[...] * _dot(a_ref[...], wd_ref[...])
    if final_norm:
        y = _rms(y, fin_ref[...])
    o_ref[...] = y


def _ffn_call(x2d, mod_l, norm_g, wu, cw, cb, wd, fin_g, seq_len, mod_row, tm, final_norm):
    rows, d = x2d.shape
    tiles_per_seq = seq_len // tm
    blk8 = tm // SUBLANES
    n_blk8 = rows // SUBLANES

    def mod_spec(chunk):
        return pl.BlockSpec((None, None, 1, d), lambda i: (mod_row(i), chunk, 0, 0))

    return pl.pallas_call(
        functools.partial(_ffn_kernel, tm=tm, tiles_per_seq=tiles_per_seq, final_norm=final_norm),
        grid=(rows // tm,),
        in_specs=[
            pl.BlockSpec((tm, d), lambda i: (i, 0)),
            pl.BlockSpec((SUBLANES, d), lambda i: (jnp.maximum(i * blk8 - 1, 0), 0)),
            pl.BlockSpec((SUBLANES, d), lambda i: (jnp.minimum((i + 1) * blk8, n_blk8 - 1), 0)),
            _resident((1, d)),
            mod_spec(3), mod_spec(4), mod_spec(5),
            _resident(wu.shape), _resident(cw.shape), _resident(cb.shape), _resident(wd.shape),
            _resident((1, d)),
        ],
        out_specs=pl.BlockSpec((tm, d), lambda i: (i, 0)),
        out_shape=jax.ShapeDtypeStruct((rows, d), F32),
        scratch_shapes=[pltpu.VMEM((tm + 2 * SUBLANES, d), BF16),
                        pltpu.VMEM((tm + 2 * SUBLANES, 2 * _FF_CHUNK), F32),
                        pltpu.VMEM((tm + 2 * SUBLANES, 2 * _FF_CHUNK), F32),
                        pltpu.VMEM((tm, D_FF), BF16)],
        compiler_params=_cparams(("arbitrary",), flags=_INTERLEAVE),
        name="conv_ffn",
    )(x2d, x2d, x2d, norm_g, mod_l, mod_l, mod_l, wu, cw, cb, wd, fin_g)


def _prep_ffn(w_up, cw, cb, w_down):
    return w_up.astype(BF16), cw, cb[None, :], w_down.astype(BF16)


def _rope_tables(seq_len, rope):
    scale = QK_DIM ** -0.5 * math.log2(math.e)
    if rope:
        n_rows = seq_len // GRID_W
        row = jnp.repeat(jnp.arange(n_rows, dtype=F32), GRID_W)
        col = jnp.tile(jnp.arange(GRID_W, dtype=F32), n_rows)
        n_freq = QK_ROPE // 4
        inv = ROPE_BASE ** (-jnp.arange(n_freq, dtype=F32) / n_freq)
        ang = jnp.concatenate([row[:, None] * inv, col[:, None] * inv], axis=-1)
        cos, sin = jnp.cos(ang), jnp.sin(ang)
    else:
        cos = jnp.ones((seq_len, QK_ROPE // 2), F32)
        sin = jnp.zeros((seq_len, QK_ROPE // 2), F32)
    ones = jnp.ones((seq_len, QK_NOPE), F32)
    zeros = jnp.zeros((seq_len, QK_NOPE), F32)
    tail = jnp.zeros((seq_len, HEAD_PAD - QK_DIM), F32)
    cos2 = jnp.concatenate([cos, cos], axis=-1)
    sin2 = jnp.concatenate([-sin, sin], axis=-1)
    cq = scale * jnp.concatenate([ones, cos2, tail], axis=-1)
    sq = scale * jnp.concatenate([zeros, sin2, tail], axis=-1)
    ck = jnp.concatenate([zeros, cos2, tail], axis=-1)
    sk = jnp.concatenate([zeros, sin2, tail], axis=-1)
    return cq, sq, ck, sk


def _value_ones():
    lane = np.arange(N_HEADS * HEAD_PAD) % HEAD_PAD
    head = np.arange(N_HEADS * HEAD_PAD) // HEAD_PAD
    return jnp.asarray(np.where(head % 2 == 0, lane == V_DIM, lane == 0)[None, :], F32)


def _prep_w_in(w):
    d = w.shape[0]
    n_front = Q_LORA + KV_LORA
    kr = w[:, n_front:n_front + QK_ROPE]
    z_lo = jnp.zeros((d, QK_NOPE), w.dtype)
    z_hi = jnp.zeros((d, HEAD_PAD - QK_DIM), w.dtype)
    w_kr = jnp.concatenate([z_lo, kr, z_hi], axis=-1)
    return w[:, :n_front].astype(BF16), w_kr.astype(BF16), w[:, n_front + QK_ROPE:].astype(BF16)


def _prep_w_uq(w):
    r, h, _ = w.shape
    pad = jnp.zeros((r, h, HEAD_PAD - QK_DIM), w.dtype)
    y1 = jnp.concatenate([w, pad], axis=-1)
    return y1.reshape(r, h * HEAD_PAD).astype(BF16)


def _prep_w_ukv(w):
    r, h, _ = w.shape
    k = jnp.concatenate([w[..., :QK_NOPE], jnp.zeros((r, h, HEAD_PAD - QK_NOPE), w.dtype)], axis=-1)
    v = w[..., QK_NOPE:]
    z = jnp.zeros_like(v)
    odd = (jnp.arange(h) % 2 == 1)[None, :, None]
    vp = jnp.concatenate([jnp.where(odd, z, v), jnp.where(odd, v, z)], axis=-1)
    return jnp.concatenate([k.reshape(r, h * HEAD_PAD), vp.reshape(r, h * HEAD_PAD)], axis=-1).astype(BF16)


def _prep_gates(w_a, b_a, w_x, b_x):
    half = _RNN_HALF
    heads_per_half = half // RNN_BLOCK

    def blockdiag(w):
        eye = jnp.eye(heads_per_half, dtype=w.dtype)
        return jnp.einsum('hij,hk->hikj', w, eye).reshape(half, half)

    ws, bs = [], []
    for j in range(RNN_WIDTH // half):
        hs = slice(j * heads_per_half, (j + 1) * heads_per_half)
        ls = slice(j * half, (j + 1) * half)
        ws.append(jnp.concatenate([blockdiag(w_a[0, hs]), blockdiag(w_x[0, hs]),
                                   blockdiag(w_a[1, hs]), blockdiag(w_x[1, hs])], axis=-1))
        bs.append(jnp.concatenate([b_a[0, ls], b_x[0, ls], b_a[1, ls], b_x[1, ls]], axis=-1)[None, :])
    return (0.5 * jnp.stack(ws)).astype(BF16), 0.5 * jnp.stack(bs)


def _dft_tables(seq_len):
    lo_n = 32 if seq_len % 32 == 0 else 1
    t = jnp.arange(seq_len, dtype=jnp.int32)

    def table(rows):
        ang = (2.0 * math.pi / seq_len) * ((rows[:, None] * t[None, :]) % seq_len).astype(F32)
        return jnp.cos(ang), jnp.sin(ang)

    c_hi, s_hi = table(jnp.arange(seq_len // lo_n, dtype=jnp.int32) * lo_n)
    c_lo, s_lo = table(jnp.arange(lo_n, dtype=jnp.int32))
    s = seq_len ** -0.5
    cos = (c_hi[:, None, :] * c_lo[None, :, :] - s_hi[:, None, :] * s_lo[None, :, :]).reshape(seq_len, seq_len)
    sin = (s_hi[:, None, :] * c_lo[None, :, :] + c_hi[:, None, :] * s_lo[None, :, :]).reshape(seq_len, seq_len)
    return jnp.concatenate([cos * s, sin * s], axis=-1).astype(BF16)


def _channel_dft():
    k = jnp.arange(F_GROUP_W, dtype=jnp.int32)
    ang = (2.0 * math.pi / F_GROUP_W) * ((k[:, None] * k[None, :]) % F_GROUP_W).astype(F32)
    s = F_GROUP_W ** -0.5
    eye = jnp.eye(F_GROUPS, dtype=F32)
    c = jnp.kron(eye, jnp.cos(ang) * s)
    sn = jnp.kron(eye, jnp.sin(ang) * s)
    return jnp.concatenate([c, -sn], axis=-1).astype(BF16)


def kernel(x, c, ctx, c_ctx, w_ada, b_ada, norm_mix, norm_ffn, w_in, q_norm, kv_norm, w_uq, w_ukv, w_o_attn,
           w_o_fourier, rnn_conv_w, rnn_conv_b, rg_w_a, rg_b_a, rg_w_x, rg_b_x, rg_lambda, w_o_rnn, w_out,
           w_up, ffn_conv_w, ffn_conv_b, w_down, final_norm):
    batch, seq_len, d = x.shape
    ctx_len = ctx.shape[1]
    depth = w_ada.shape[0]
    assert batch + 1 <= MOD_ROWS and d == D_MODEL

    cc = jnp.concatenate([c, c_ctx[None, :], jnp.zeros((MOD_ROWS - batch - 1, d), F32)], axis=0)
    mod = _ada_call(cc, w_ada, b_ada).reshape(depth, MOD_ROWS, N_MOD, 1, d)

    tabs_x = _rope_tables(seq_len, True)
    tabs_c = _rope_tables(ctx_len, False)
    dft_x = _dft_tables(seq_len)
    dft_c = _dft_tables(ctx_len)
    cs = _channel_dft()

    tm_in = min(512, seq_len)
    tm_x = 256
    tm_c = min(256, ctx_len)
    tq = min(512, seq_len)
    tm_ffn = min(1024, seq_len)

    def x_row(tm):
        return lambda i: i // (seq_len // tm)

    c_row = lambda i: batch

    xs = x.reshape(batch * seq_len, d)
    cs_tok = ctx.reshape(batch * ctx_len, d)

    for l in range(depth):
        last = l == depth - 1
        mod_l = mod[l]
        w_in_p = _prep_w_in(w_in[l])
        wuq = _prep_w_uq(w_uq[l])
        wukv = _prep_w_ukv(w_ukv[l])
        wg, bg = _prep_gates(rg_w_a[l], rg_b_a[l], rg_w_x[l], rg_b_x[l])
        nm = norm_mix[l][None, :]
        nf = norm_ffn[l][None, :]
        qn = q_norm[l][None, :]
        kvn = kv_norm[l][None, :]
        woa = w_o_attn[l].astype(BF16)
        wof = w_o_fourier[l].astype(BF16)
        wor = w_o_rnn[l].astype(BF16)
        wout = w_out[l].astype(BF16)
        wup, fcw, fcb, wdn = _prep_ffn(w_up[l], ffn_conv_w[l], ffn_conv_b[l], w_down[l])
        rcw = rnn_conv_w[l]
        rcb = rnn_conv_b[l][None, :]
        fin = final_norm[None, :]

        q_x, k_x, v_x, uf_x, ur_x, gug_x, g_x = _in_call(xs, mod_l, nm, w_in_p, qn, kvn, wuq, wukv, tabs_x,
                                                          seq_len, x_row(tm_in), tm_in)
        q_c, k_c, v_c, uf_c, ur_c, gug_c, g_c = _in_call(cs_tok, mod_l, nm, w_in_p, qn, kvn, wuq, wukv, tabs_c,
                                                          ctx_len, c_row, tm_c)

        att_x = _attn_call(q_x, [(k_c, v_c, ctx_len), (k_x, v_x, seq_len)], seq_len, tq)
        yf_x = _fourier_call(uf_x, cs, dft_x, seq_len, min(512, seq_len))
        yr_c, yr_x = _rnn_call(ur_c, ur_x, gug_c, gug_x, rcw, rcb, wg, bg, rg_lambda[l], ctx_len, seq_len)

        xs = _merge_call(xs, att_x, yf_x, yr_x, g_x, mod_l, woa, wof, wor, wout, x_row(tm_x), tm_x)
        xs = _ffn_call(xs, mod_l, nf, wup, fcw, fcb, wdn, fin, seq_len, x_row(tm_ffn), tm_ffn, last)

        if not last:
            att_c = _attn_call(q_c, [(k_c, v_c, ctx_len)], ctx_len, min(_ATTN_SUB_ROWS, ctx_len))
            yf_c = _fourier_call(uf_c, cs, dft_c, ctx_len, ctx_len)
            cs_tok = _merge_call(cs_tok, att_c, yf_c, yr_c, g_c, mod_l, woa, wof, wor, wout, c_row, tm_c)
            cs_tok = _ffn_call(cs_tok, mod_l, nf, wup, fcw, fcb, wdn, fin, ctx_len, c_row, ctx_len, False)

    return xs.reshape(batch, seq_len, d)
```

```python
import functools
import math

import jax
import jax.numpy as jnp
import numpy as np
from jax import lax
from jax.experimental import pallas as pl
from jax.experimental.pallas import tpu as pltpu

F32 = jnp.float32
BF16 = jnp.bfloat16

D_MODEL = 1024
GRID_W = 64
N_HEADS = 8
Q_LORA = 384
KV_LORA = 256
QK_NOPE = 64
QK_ROPE = 32
V_DIM = 64
QK_DIM = QK_NOPE + QK_ROPE
ROPE_BASE = 10000.0
F_GROUPS = 4
F_GROUP_W = 128
F_WIDTH = F_GROUPS * F_GROUP_W
RNN_WIDTH = 512
RNN_HEADS = 8
RNN_BLOCK = RNN_WIDTH // RNN_HEADS
RNN_CONV = 4
LRU_C = 8.0
D_FF = 2816
FFN_CONV = 3
N_MOD = 6
EPS = 1e-6

LANES = 128
SUBLANES = 8
HEAD_PAD = 128
MOD_ROWS = 16

_O_UF = 0
_O_UR = _O_UF + F_WIDTH
_O_UG = _O_UR + RNN_WIDTH
_O_GL = _O_UG + RNN_WIDTH

_VMEM_LIMIT = 56 * 2 ** 20


_INTERLEAVE = None


def _cparams(sem, vmem=_VMEM_LIMIT, flags=None):
    return pltpu.CompilerParams(dimension_semantics=sem, vmem_limit_bytes=vmem, flags=flags)


def _resident(shape):
    nd = len(shape)
    return pl.BlockSpec(shape, lambda *_: (0,) * nd, pipeline_mode=pl.Buffered(1))


def _dot(a, b):
    return jnp.dot(a, b, preferred_element_type=F32)


def _sigmoid(x):
    return 0.5 * (1.0 + jnp.tanh(0.5 * x))


def _gelu(x):
    return 0.5 * x * (1.0 + jnp.tanh(math.sqrt(2.0 / math.pi) * (x + 0.044715 * (x * x * x))))


def _rms(x, g):
    return x * lax.rsqrt(jnp.mean(x * x, axis=-1, keepdims=True) + EPS) * g


def _ada_kernel(cc_ref, w_ref, b_ref, o_ref):
    cc = cc_ref[...]
    s = cc * _sigmoid(cc)
    o_ref[...] = _dot(s.astype(BF16), w_ref[...].astype(BF16)) + b_ref[...]


def _ada_call(cc, w_ada, b_ada):
    depth, d, n = w_ada.shape
    tn = 1536
    return pl.pallas_call(
        _ada_kernel,
        grid=(depth, n // tn),
        in_specs=[
            pl.BlockSpec((MOD_ROWS, d), lambda l, j: (0, 0)),
            pl.BlockSpec((None, d, tn), lambda l, j: (l, 0, j)),
            pl.BlockSpec((None, 1, tn), lambda l, j: (l, 0, j)),
        ],
        out_specs=pl.BlockSpec((None, MOD_ROWS, tn), lambda l, j: (l, 0, j)),
        out_shape=jax.ShapeDtypeStruct((depth, MOD_ROWS, n), F32),
        compiler_params=_cparams(("arbitrary", "arbitrary")),
        name="adaln",
    )(cc, w_ada, b_ada.reshape(depth, 1, n))


def _in_kernel(x_ref, g_ref, shift_ref, scale_ref, wf_ref, wkr_ref, w_ref, qn_ref, kvn_ref, wuq_ref, wukv_ref,
               cq_ref, sq_ref, ck_ref, sk_ref, vone_ref,
               q_out, k_out, v_out, uf_out, ur_out, gug_out, g_out):
    h = _rms(x_ref[...], g_ref[...]) * (1.0 + scale_ref[...]) + shift_ref[...]
    hb = h.astype(BF16)
    hp = N_HEADS * HEAD_PAD
    first_half = lax.broadcasted_iota(jnp.int32, (x_ref.shape[0], HEAD_PAD), 1) < QK_NOPE + QK_ROPE // 2

    def swap_rot(y):
        fwd = pltpu.roll(y, QK_ROPE // 2, axis=1)
        bwd = pltpu.roll(y, y.shape[1] - QK_ROPE // 2, axis=1)
        return [jnp.where(first_half, bwd[:, a:a + HEAD_PAD], fwd[:, a:a + HEAD_PAD])
                for a in range(0, y.shape[1], HEAD_PAD)]

    def gate_dot(j):
        return _dot(hb, w_ref[:, _O_GL + j * D_MODEL:_O_GL + (j + 1) * D_MODEL])

    def gate_out(j, z):
        g_out[:, j * D_MODEL:(j + 1) * D_MODEL] = _sigmoid(z).astype(BF16)

    cq = _dot(hb, wf_ref[:, 0:Q_LORA])
    ckv = _dot(hb, wf_ref[:, Q_LORA:Q_LORA + KV_LORA])
    kr = _dot(hb, wkr_ref[...])
    z0 = gate_dot(0)

    cqn = _rms(cq, qn_ref[...]).astype(BF16)
    ckvn = _rms(ckv, kvn_ref[...]).astype(BF16)
    y = _dot(cqn, wuq_ref[...])
    kv = _dot(ckvn, wukv_ref[...])
    gate_out(0, z0)
    z1 = gate_dot(1)

    y_sw = swap_rot(y)
    cq_t = cq_ref[...]
    sq_t = sq_ref[...]
    for hd in range(N_HEADS):
        a = hd * HEAD_PAD
        q_out[:, a:a + HEAD_PAD] = (y[:, a:a + HEAD_PAD] * cq_t + y_sw[hd] * sq_t).astype(BF16)
    z2 = gate_dot(2)

    k_rope = kr * ck_ref[...] + swap_rot(kr)[0] * sk_ref[...]
    for hd in range(N_HEADS):
        a = hd * HEAD_PAD
        k_out[:, a:a + HEAD_PAD] = (kv[:, a:a + HEAD_PAD] + k_rope).astype(BF16)
    v_out[...] = (kv[:, hp:] + vone_ref[...]).astype(BF16)
    uf = _dot(hb, w_ref[:, _O_UF:_O_UF + F_WIDTH])
    ur = _dot(hb, w_ref[:, _O_UR:_O_UR + RNN_WIDTH])
    gate_out(1, z1)
    ug = _dot(hb, w_ref[:, _O_UG:_O_UG + RNN_WIDTH])
    gate_out(2, z2)
    uf_out[...] = uf.astype(BF16)
    ur_out[...] = ur
    gug_out[...] = _gelu(ug).astype(BF16)


def _in_call(x2d, mod_l, norm_g, w_in_p, qn, kvn, wuq, wukv, tabs, seq_len, mod_row, tm):
    rows, d = x2d.shape
    tiles_per_seq = seq_len // tm
    hp = N_HEADS * HEAD_PAD

    def row_spec(width):
        return pl.BlockSpec((tm, width), lambda i: (i, 0))

    def mod_spec(chunk):
        return pl.BlockSpec((None, None, 1, d), lambda i: (mod_row(i), chunk, 0, 0))

    tab_spec = pl.BlockSpec((tm, HEAD_PAD), lambda i: (i % tiles_per_seq, 0))
    out_shapes = (
        jax.ShapeDtypeStruct((rows, hp), BF16),
        jax.ShapeDtypeStruct((rows, hp), BF16),
        jax.ShapeDtypeStruct((rows, hp), BF16),
        jax.ShapeDtypeStruct((rows, F_WIDTH), BF16),
        jax.ShapeDtypeStruct((rows, RNN_WIDTH), F32),
        jax.ShapeDtypeStruct((rows, RNN_WIDTH), BF16),
        jax.ShapeDtypeStruct((rows, 3 * d), BF16),
    )
    return pl.pallas_call(
        _in_kernel,
        grid=(rows // tm,),
        in_specs=[
            row_spec(d),
            _resident((1, d)),
            mod_spec(0),
            mod_spec(1),
            _resident(w_in_p[0].shape), _resident(w_in_p[1].shape), _resident(w_in_p[2].shape),
            _resident((1, Q_LORA)),
            _resident((1, KV_LORA)),
            _resident(wuq.shape),
            _resident(wukv.shape),
            tab_spec, tab_spec, tab_spec, tab_spec,
            _resident((1, hp)),
        ],
        out_specs=tuple(row_spec(s.shape[1]) for s in out_shapes),
        out_shape=out_shapes,
        compiler_params=_cparams(("arbitrary",)),
        name="in_proj",
    )(x2d, norm_g, mod_l, mod_l, *w_in_p, qn, kvn, wuq, wukv, *tabs, _value_ones())


_ATTN_SUB_ROWS = 256


def _attn_kernel(*refs, n_seg):
    q_ref = refs[0]
    k_refs = refs[1:1 + n_seg]
    v_refs = refs[1 + n_seg:1 + 2 * n_seg]
    o_ref = refs[1 + 2 * n_seg]
    nt = (((1,), (1,)), ((), ()))
    sub = _ATTN_SUB_ROWS
    low = lax.broadcasted_iota(jnp.int32, (sub, HEAD_PAD), 1) < V_DIM

    def scores(item):
        r0, hd = item
        a = hd * HEAD_PAD
        qh = q_ref[r0:r0 + sub, a:a + HEAD_PAD]
        return [lax.dot_general(qh, k[:, a:a + HEAD_PAD], nt, preferred_element_type=F32) for k in k_refs]

    items = [(r0, hd) for r0 in range(0, q_ref.shape[0], sub) for hd in range(N_HEADS)]
    outs = {}
    s_next = scores(items[0])
    for idx, (r0, hd) in enumerate(items):
        a = hd * HEAD_PAD
        s = s_next
        if idx + 1 < len(items):
            s_next = scores(items[idx + 1])
        m = functools.reduce(jnp.maximum, [jnp.max(si, axis=-1, keepdims=True) for si in s])
        p = [jnp.exp2((si - m).astype(BF16)) for si in s]
        o = functools.reduce(jnp.add, [_dot(pi, v[:, a:a + HEAD_PAD]) for pi, v in zip(p, v_refs)])
        one_lane = V_DIM if hd % 2 == 0 else 0
        outs[hd] = o * (1.0 / o[:, one_lane:one_lane + 1])
        if hd % 2 == 1:
            o_ref[r0:r0 + sub, (hd - 1) * V_DIM:(hd + 1) * V_DIM] = jnp.where(low, outs.pop(hd - 1), outs.pop(hd)).astype(BF16)


def _attn_call(q, segs, q_len, tq):
    rows, hp = q.shape
    tiles = q_len // tq
    n_seg = len(segs)
    k_specs = [pl.BlockSpec((ln, hp), lambda i: (i // tiles, 0)) for _, _, ln in segs]
    return pl.pallas_call(
        functools.partial(_attn_kernel, n_seg=n_seg),
        grid=(rows // tq,),
        in_specs=[pl.BlockSpec((tq, hp), lambda i: (i, 0))] + k_specs + k_specs,
        out_specs=pl.BlockSpec((tq, N_HEADS * V_DIM), lambda i: (i, 0)),
        out_shape=jax.ShapeDtypeStruct((rows, N_HEADS * V_DIM), BF16),
        compiler_params=_cparams(("arbitrary",)),
        name="attention",
    )(q, *[k for k, _, _ in segs], *[v for _, v, _ in segs])


def _fourier_kernel(u_ref, cs_ref, dft_ref, o_ref, ab_ref, *, seq_len):
    @pl.when(pl.program_id(1) == 0)
    def _():
        ab = _dot(u_ref[...], cs_ref[...])
        ab_ref[:seq_len, :] = ab[:, :F_WIDTH].astype(BF16)
        ab_ref[seq_len:, :] = ab[:, F_WIDTH:].astype(BF16)

    o_ref[...] = _dot(dft_ref[...], ab_ref[...]).astype(BF16)


def _fourier_call(uf, cs, dft, seq_len, tmf):
    rows = uf.shape[0]
    tiles = seq_len // tmf
    return pl.pallas_call(
        functools.partial(_fourier_kernel, seq_len=seq_len),
        grid=(rows // seq_len, tiles),
        in_specs=[
            pl.BlockSpec((seq_len, F_WIDTH), lambda b, m: (b, 0)),
            _resident(cs.shape),
            pl.BlockSpec((tmf, 2 * seq_len), lambda b, m: (m, 0)),
        ],
        out_specs=pl.BlockSpec((tmf, F_WIDTH), lambda b, m: (b * tiles + m, 0)),
        out_shape=jax.ShapeDtypeStruct((rows, F_WIDTH), BF16),
        scratch_shapes=[pltpu.VMEM((2 * seq_len, F_WIDTH), BF16)],
        compiler_params=_cparams(("arbitrary", "arbitrary")),
        name="fourier",
    )(uf, cs, dft)


_RNN_HALF = RNN_WIDTH // 2
_RNN_CHUNK = 256


def _rnn_kernel(urc_ref, urx_ref, gugc_ref, gugx_ref, cw_ref, cb_ref, wg_ref, bg_ref, lam_ref,
                yc_ref, yx_ref, pad_ref, af_ref, bf_ref, ab_ref, bb_ref, *, ctx_len, seq_len):
    half = _RNN_HALF
    row8 = lax.broadcasted_iota(jnp.int32, (SUBLANES, half), 0)
    zeros8 = jnp.zeros((SUBLANES, half), F32)

    def gates(ur_ref, n, j):
        lo = j * half
        pad_ref[0:SUBLANES, :] = zeros8
        pad_ref[SUBLANES:SUBLANES + n, :] = ur_ref[:, lo:lo + half]
        pad_ref[SUBLANES + n:2 * SUBLANES + n, :] = zeros8
        cw = cw_ref[:, lo:lo + half]
        cb = cb_ref[:, lo:lo + half]
        lam = lam_ref[:, lo:lo + half]
        k = (-0.5 * LRU_C * math.log2(math.e)) * (jnp.maximum(-lam, 0.0) + jnp.log1p(jnp.exp(-jnp.abs(lam))))
        wg = wg_ref[j]
        bg = bg_ref[j]
        ext_rows = _RNN_CHUNK + 2 * SUBLANES

        def chunk(c, carry):
            r0 = pl.multiple_of(c * _RNN_CHUNK, _RNN_CHUNK)
            ext = pad_ref[pl.ds(r0, ext_rows), :]
            sl = slice(SUBLANES, SUBLANES + _RNN_CHUNK)
            uc = (cb + cw[0:1] * pltpu.roll(ext, 2, axis=0)[sl] + cw[1:2] * pltpu.roll(ext, 1, axis=0)[sl]
                  + cw[2:3] * ext[sl] + cw[3:4] * pltpu.roll(ext, ext_rows - 1, axis=0)[sl])
            g = _dot(uc.astype(BF16), wg) + bg
            uch = 0.5 * uc
            for d, (a_ref, b_ref) in enumerate(((af_ref, bf_ref), (ab_ref, bb_ref))):
                kd = k[d:d + 1]
                a = jnp.exp2(kd + kd * jnp.tanh(g[:, (2 * d) * half:(2 * d + 1) * half]))
                a_ref[pl.ds(r0, _RNN_CHUNK), :] = a
                om = 1.0 - a * a
                sq = om * lax.rsqrt(jnp.maximum(om, 1e-30))
                b_ref[pl.ds(r0, _RNN_CHUNK), :] = (sq * uch) * (1.0 + jnp.tanh(g[:, (2 * d + 1) * half:(2 * d + 2) * half]))
            return carry

        lax.fori_loop(0, n // _RNN_CHUNK, chunk, 0)

    def scan(n, hf, hb):
        nb = n // SUBLANES

        def block(i, carry):
            hf, hb = carry
            rf = pl.multiple_of(i * SUBLANES, SUBLANES)
            a = af_ref[pl.ds(rf, SUBLANES), :]
            b = bf_ref[pl.ds(rf, SUBLANES), :]
            for d in (1, 2, 4):
                ok = row8 >= d
                b = jnp.where(ok, a * pltpu.roll(b, d, axis=0) + b, b)
                a = jnp.where(ok, a * pltpu.roll(a, d, axis=0), a)
            h = a * hf + b
            bf_ref[pl.ds(rf, SUBLANES), :] = h
            hf = jnp.broadcast_to(h[SUBLANES - 1:SUBLANES, :], h.shape)

            rb = pl.multiple_of((nb - 1 - i) * SUBLANES, SUBLANES)
            a = ab_ref[pl.ds(rb, SUBLANES), :]
            b = bb_ref[pl.ds(rb, SUBLANES), :]
            for d in (1, 2, 4):
                ok = row8 < SUBLANES - d
                b = jnp.where(ok, a * pltpu.roll(b, SUBLANES - d, axis=0) + b, b)
                a = jnp.where(ok, a * pltpu.roll(a, SUBLANES - d, axis=0), a)
            h = a * hb + b
            bb_ref[pl.ds(rb, SUBLANES), :] = h
            hb = jnp.broadcast_to(h[0:1, :], h.shape)
            return hf, hb

        return lax.fori_loop(0, nb, block, (hf, hb), unroll=2)

    for j in range(RNN_WIDTH // half):
        lo = j * half
        gates(urc_ref, ctx_len, j)
        hf, hb = scan(ctx_len, zeros8, zeros8)
        yc_ref[:, lo:lo + half] = ((bf_ref[0:ctx_len, :] + bb_ref[0:ctx_len, :])
                                   * gugc_ref[:, lo:lo + half].astype(F32)).astype(BF16)
        gates(urx_ref, seq_len, j)
        scan(seq_len, hf, hb)
        yx_ref[:, lo:lo + half] = ((bf_ref[0:seq_len, :] + bb_ref[0:seq_len, :])
                                   * gugx_ref[:, lo:lo + half].astype(F32)).astype(BF16)


def _rnn_call(ur_c, ur_x, gug_c, gug_x, cw, cb, wg, bg, lam, ctx_len, seq_len):
    batch = ur_x.shape[0] // seq_len
    w = RNN_WIDTH

    def seq_spec(n):
        return pl.BlockSpec((n, w), lambda b: (b, 0))

    return pl.pallas_call(
        functools.partial(_rnn_kernel, ctx_len=ctx_len, seq_len=seq_len),
        grid=(batch,),
        in_specs=[seq_spec(ctx_len), seq_spec(seq_len), seq_spec(ctx_len), seq_spec(seq_len),
                  _resident(cw.shape), _resident(cb.shape), _resident(wg.shape), _resident(bg.shape),
                  _resident(lam.shape)],
        out_specs=(seq_spec(ctx_len), seq_spec(seq_len)),
        out_shape=(jax.ShapeDtypeStruct((batch * ctx_len, w), BF16),
                   jax.ShapeDtypeStruct((batch * seq_len, w), BF16)),
        scratch_shapes=[pltpu.VMEM((seq_len + 2 * SUBLANES, _RNN_HALF), F32)]
        + [pltpu.VMEM((seq_len, _RNN_HALF), F32)] * 4,
        compiler_params=_cparams(("arbitrary",)),
        name="rglru",
    )(ur_c, ur_x, gug_c, gug_x, cw, cb, wg, bg, lam)


def _merge_kernel(x_ref, att_ref, yf_ref, yr_ref, g_ref, gate_ref, woa_ref, wof_ref, wor_ref, wout_ref, o_ref):
    d = D_MODEL
    m = (g_ref[:, 0:d].astype(F32) * _dot(att_ref[...], woa_ref[...])
         + g_ref[:, d:2 * d].astype(F32) * _dot(yf_ref[...], wof_ref[...])
         + g_ref[:, 2 * d:3 * d].astype(F32) * _dot(yr_ref[...], wor_ref[...]))
    o_ref[...] = x_ref[...] + gate_ref[...] * _dot(m.astype(BF16), wout_ref[...])


def _merge_call(x2d, att, yf, yr, g, mod_l, woa, wof, wor, wout, mod_row, tm):
    rows, d = x2d.shape

    def row_spec(width):
        return pl.BlockSpec((tm, width), lambda i: (i, 0))

    return pl.pallas_call(
        _merge_kernel,
        grid=(rows // tm,),
        in_specs=[row_spec(d), row_spec(att.shape[1]), row_spec(yf.shape[1]), row_spec(yr.shape[1]),
                  row_spec(3 * d),
                  pl.BlockSpec((None, None, 1, d), lambda i: (mod_row(i), 2, 0, 0)),
                  _resident(woa.shape), _resident(wof.shape), _resident(wor.shape), _resident(wout.shape)],
        out_specs=row_spec(d),
        out_shape=jax.ShapeDtypeStruct((rows, d), F32),
        compiler_params=_cparams(("arbitrary",)),
        name="merge",
    )(x2d, att, yf, yr, g, mod_l, woa, wof, wor, wout)


_FF_CHUNK = 256
_FF_ROW_BLOCKS = 2


def _ffn_kernel(x_ref, xp_ref, xn_ref, g_ref, shift_ref, scale_ref, gate_ref,
                wu_ref, cw_ref, cb_ref, wd_ref, fin_ref, o_ref, h_ref, u0_ref, u1_ref, a_ref, *, tm, tiles_per_seq, final_norm):
    i = pl.program_id(0)
    ext = tm + 2 * SUBLANES
    n_chunks = D_FF // _FF_CHUNK
    fc = _FF_CHUNK

    def prep(x):
        return (_rms(x, g_ref[...]) * (1.0 + scale_ref[...]) + shift_ref[...]).astype(BF16)

    first = (i % tiles_per_seq) == 0
    last = (i % tiles_per_seq) == tiles_per_seq - 1
    hp = prep(xp_ref[...])
    hn = prep(xn_ref[...])
    h_ref[0:SUBLANES, :] = jnp.where(first, jnp.zeros_like(hp), hp)
    h_ref[SUBLANES:SUBLANES + tm, :] = prep(x_ref[...])
    h_ref[SUBLANES + tm:, :] = jnp.where(last, jnp.zeros_like(hn), hn)

    sl = slice(SUBLANES, SUBLANES + tm)

    def cols(c):
        return pl.ds(pl.multiple_of(c * fc, fc), fc)

    n_blk = _FF_ROW_BLOCKS
    hm = tm // n_blk
    hme = hm + 2 * SUBLANES
    slh = slice(SUBLANES, SUBLANES + hm)

    def conv(u, cw, cb):
        return cb + cw[0:1] * pltpu.roll(u, 1, axis=0)[slh] + cw[1:2] * u[slh] + cw[2:3] * pltpu.roll(u, hme - 1, axis=0)[slh]

    def activate(c, u_ref, r):
        rows = slice(r * hm, r * hm + hme)
        up = conv(u_ref[rows, 0:fc], cw_ref[:, cols(c)], cb_ref[:, cols(c)])
        gt = conv(u_ref[rows, fc:2 * fc], cw_ref[:, cols(n_chunks + c)], cb_ref[:, cols(n_chunks + c)])
        a_ref[r * hm:(r + 1) * hm, cols(c)] = _gelu(gt.astype(BF16)) * up.astype(BF16)

    u_bufs = (u0_ref, u1_ref)

    def step(t, parity, do_up=True, do_act=True):
        for r in range(n_blk):
            if do_up and r % (n_blk // 2) == 0:
                j = r // (n_blk // 2)
                u_bufs[parity][:, j * fc:(j + 1) * fc] = _dot(h_ref[...], wu_ref[:, cols(j * n_chunks + t)])
            if do_act:
                activate(t - 1, u_bufs[1 - parity], r)

    assert n_chunks % 2 == 1
    step(0, 0, do_act=False)

    def body(k, carry):
        t = 2 * k + 1
        step(t, 1)
        step(t + 1, 0)
        return carry

    lax.fori_loop(0, n_chunks // 2, body, 0)
    step(n_chunks, 1, do_up=False)
    y = x_ref[...] + gate_ref[...] * _dot(a_ref[...], wd_ref[...])
    if final_norm:
        y = _rms(y, fin_ref[...])
    o_ref[...] = y


def _ffn_call(x2d, mod_l, norm_g, wu, cw, cb, wd, fin_g, seq_len, mod_row, tm, final_norm):
    rows, d = x2d.shape
    tiles_per_seq = seq_len // tm
    blk8 = tm // SUBLANES
    n_blk8 = rows // SUBLANES

    def mod_spec(chunk):
        return pl.BlockSpec((None, None, 1, d), lambda i: (mod_row(i), chunk, 0, 0))

    return pl.pallas_call(
        functools.partial(_ffn_kernel, tm=tm, tiles_per_seq=tiles_per_seq, final_norm=final_norm),
        grid=(rows // tm,),
        in_specs=[
            pl.BlockSpec((tm, d), lambda i: (i, 0)),
            pl.BlockSpec((SUBLANES, d), lambda i: (jnp.maximum(i * blk8 - 1, 0), 0)),
            pl.BlockSpec((SUBLANES, d), lambda i: (jnp.minimum((i + 1) * blk8, n_blk8 - 1), 0)),
            _resident((1, d)),
            mod_spec(3), mod_spec(4), mod_spec(5),
            _resident(wu.shape), _resident(cw.shape), _resident(cb.shape), _resident(wd.shape),
            _resident((1, d)),
        ],
        out_specs=pl.BlockSpec((tm, d), lambda i: (i, 0)),
        out_shape=jax.ShapeDtypeStruct((rows, d), F32),
        scratch_shapes=[pltpu.VMEM((tm + 2 * SUBLANES, d), BF16),
                        pltpu.VMEM((tm + 2 * SUBLANES, 2 * _FF_CHUNK), F32),
                        pltpu.VMEM((tm + 2 * SUBLANES, 2 * _FF_CHUNK), F32),
                        pltpu.VMEM((tm, D_FF), BF16)],
        compiler_params=_cparams(("arbitrary",), flags=_INTERLEAVE),
        name="conv_ffn",
    )(x2d, x2d, x2d, norm_g, mod_l, mod_l, mod_l, wu, cw, cb, wd, fin_g)


def _prep_ffn(w_up, cw, cb, w_down):
    return w_up.astype(BF16), cw, cb[None, :], w_down.astype(BF16)


def _rope_tables(seq_len, rope):
    scale = QK_DIM ** -0.5 * math.log2(math.e)
    if rope:
        n_rows = seq_len // GRID_W
        row = jnp.repeat(jnp.arange(n_rows, dtype=F32), GRID_W)
        col = jnp.tile(jnp.arange(GRID_W, dtype=F32), n_rows)
        n_freq = QK_ROPE // 4
        inv = ROPE_BASE ** (-jnp.arange(n_freq, dtype=F32) / n_freq)
        ang = jnp.concatenate([row[:, None] * inv, col[:, None] * inv], axis=-1)
        cos, sin = jnp.cos(ang), jnp.sin(ang)
    else:
        cos = jnp.ones((seq_len, QK_ROPE // 2), F32)
        sin = jnp.zeros((seq_len, QK_ROPE // 2), F32)
    ones = jnp.ones((seq_len, QK_NOPE), F32)
    zeros = jnp.zeros((seq_len, QK_NOPE), F32)
    tail = jnp.zeros((seq_len, HEAD_PAD - QK_DIM), F32)
    cos2 = jnp.concatenate([cos, cos], axis=-1)
    sin2 = jnp.concatenate([-sin, sin], axis=-1)
    cq = scale * jnp.concatenate([ones, cos2, tail], axis=-1)
    sq = scale * jnp.concatenate([zeros, sin2, tail], axis=-1)
    ck = jnp.concatenate([zeros, cos2, tail], axis=-1)
    sk = jnp.concatenate([zeros, sin2, tail], axis=-1)
    return cq, sq, ck, sk


def _value_ones():
    lane = np.arange(N_HEADS * HEAD_PAD) % HEAD_PAD
    head = np.arange(N_HEADS * HEAD_PAD) // HEAD_PAD
    return jnp.asarray(np.where(head % 2 == 0, lane == V_DIM, lane == 0)[None, :], F32)


def _prep_w_in(w):
    d = w.shape[0]
    n_front = Q_LORA + KV_LORA
    kr = w[:, n_front:n_front + QK_ROPE]
    z_lo = jnp.zeros((d, QK_NOPE), w.dtype)
    z_hi = jnp.zeros((d, HEAD_PAD - QK_DIM), w.dtype)
    w_kr = jnp.concatenate([z_lo, kr, z_hi], axis=-1)
    return w[:, :n_front].astype(BF16), w_kr.astype(BF16), w[:, n_front + QK_ROPE:].astype(BF16)


def _prep_w_uq(w):
    r, h, _ = w.shape
    pad = jnp.zeros((r, h, HEAD_PAD - QK_DIM), w.dtype)
    y1 = jnp.concatenate([w, pad], axis=-1)
    return y1.reshape(r, h * HEAD_PAD).astype(BF16)


def _prep_w_ukv(w):
    r, h, _ = w.shape
    k = jnp.concatenate([w[..., :QK_NOPE], jnp.zeros((r, h, HEAD_PAD - QK_NOPE), w.dtype)], axis=-1)
    v = w[..., QK_NOPE:]
    z = jnp.zeros_like(v)
    odd = (jnp.arange(h) % 2 == 1)[None, :, None]
    vp = jnp.concatenate([jnp.where(odd, z, v), jnp.where(odd, v, z)], axis=-1)
    return jnp.concatenate([k.reshape(r, h * HEAD_PAD), vp.reshape(r, h * HEAD_PAD)], axis=-1).astype(BF16)


def _prep_gates(w_a, b_a, w_x, b_x):
    half = _RNN_HALF
    heads_per_half = half // RNN_BLOCK

    def blockdiag(w):
        eye = jnp.eye(heads_per_half, dtype=w.dtype)
        return jnp.einsum('hij,hk->hikj', w, eye).reshape(half, half)

    ws, bs = [], []
    for j in range(RNN_WIDTH // half):
        hs = slice(j * heads_per_half, (j + 1) * heads_per_half)
        ls = slice(j * half, (j + 1) * half)
        ws.append(jnp.concatenate([blockdiag(w_a[0, hs]), blockdiag(w_x[0, hs]),
                                   blockdiag(w_a[1, hs]), blockdiag(w_x[1, hs])], axis=-1))
        bs.append(jnp.concatenate([b_a[0, ls], b_x[0, ls], b_a[1, ls], b_x[1, ls]], axis=-1)[None, :])
    return (0.5 * jnp.stack(ws)).astype(BF16), 0.5 * jnp.stack(bs)


def _dft_tables(seq_len):
    lo_n = 32 if seq_len % 32 == 0 else 1
    t = jnp.arange(seq_len, dtype=jnp.int32)

    def table(rows):
        ang = (2.0 * math.pi / seq_len) * ((rows[:, None] * t[None, :]) % seq_len).astype(F32)
        return jnp.cos(ang), jnp.sin(ang)

    c_hi, s_hi = table(jnp.arange(seq_len // lo_n, dtype=jnp.int32) * lo_n)
    c_lo, s_lo = table(jnp.arange(lo_n, dtype=jnp.int32))
    s = seq_len ** -0.5
    cos = (c_hi[:, None, :] * c_lo[None, :, :] - s_hi[:, None, :] * s_lo[None, :, :]).reshape(seq_len, seq_len)
    sin = (s_hi[:, None, :] * c_lo[None, :, :] + c_hi[:, None, :] * s_lo[None, :, :]).reshape(seq_len, seq_len)
    return jnp.concatenate([cos * s, sin * s], axis=-1).astype(BF16)


def _channel_dft():
    k = jnp.arange(F_GROUP_W, dtype=jnp.int32)
    ang = (2.0 * math.pi / F_GROUP_W) * ((k[:, None] * k[None, :]) % F_GROUP_W).astype(F32)
    s = F_GROUP_W ** -0.5
    eye = jnp.eye(F_GROUPS, dtype=F32)
    c = jnp.kron(eye, jnp.cos(ang) * s)
    sn = jnp.kron(eye, jnp.sin(ang) * s)
    return jnp.concatenate([c, -sn], axis=-1).astype(BF16)


def kernel(x, c, ctx, c_ctx, w_ada, b_ada, norm_mix, norm_ffn, w_in, q_norm, kv_norm, w_uq, w_ukv, w_o_attn,
           w_o_fourier, rnn_conv_w, rnn_conv_b, rg_w_a, rg_b_a, rg_w_x, rg_b_x, rg_lambda, w_o_rnn, w_out,
           w_up, ffn_conv_w, ffn_conv_b, w_down, final_norm):
    batch, seq_len, d = x.shape
    ctx_len = ctx.shape[1]
    depth = w_ada.shape[0]
    assert batch + 1 <= MOD_ROWS and d == D_MODEL

    cc = jnp.concatenate([c, c_ctx[None, :], jnp.zeros((MOD_ROWS - batch - 1, d), F32)], axis=0)
    mod = _ada_call(cc, w_ada, b_ada).reshape(depth, MOD_ROWS, N_MOD, 1, d)

    tabs_x = _rope_tables(seq_len, True)
    tabs_c = _rope_tables(ctx_len, False)
    dft_x = _dft_tables(seq_len)
    dft_c = _dft_tables(ctx_len)
    cs = _channel_dft()

    tm_in = min(512, seq_len)
    tm_x = min(512, seq_len)
    tm_c = min(256, ctx_len)
    tq = min(512, seq_len)
    tm_ffn = min(1024, seq_len)

    def x_row(tm):
        return lambda i: i // (seq_len // tm)

    c_row = lambda i: batch

    xs = x.reshape(batch * seq_len, d)
    cs_tok = ctx.reshape(batch * ctx_len, d)

    for l in range(depth):
        last = l == depth - 1
        mod_l = mod[l]
        w_in_p = _prep_w_in(w_in[l])
        wuq = _prep_w_uq(w_uq[l])
        wukv = _prep_w_ukv(w_ukv[l])
        wg, bg = _prep_gates(rg_w_a[l], rg_b_a[l], rg_w_x[l], rg_b_x[l])
        nm = norm_mix[l][None, :]
        nf = norm_ffn[l][None, :]
        qn = q_norm[l][None, :]
        kvn = kv_norm[l][None, :]
        woa = w_o_attn[l].astype(BF16)
        wof = w_o_fourier[l].astype(BF16)
        wor = w_o_rnn[l].astype(BF16)
        wout = w_out[l].astype(BF16)
        wup, fcw, fcb, wdn = _prep_ffn(w_up[l], ffn_conv_w[l], ffn_conv_b[l], w_down[l])
        rcw = rnn_conv_w[l]
        rcb = rnn_conv_b[l][None, :]
        fin = final_norm[None, :]

        q_x, k_x, v_x, uf_x, ur_x, gug_x, g_x = _in_call(xs, mod_l, nm, w_in_p, qn, kvn, wuq, wukv, tabs_x,
                                                          seq_len, x_row(tm_in), tm_in)
        q_c, k_c, v_c, uf_c, ur_c, gug_c, g_c = _in_call(cs_tok, mod_l, nm, w_in_p, qn, kvn, wuq, wukv, tabs_c,
                                                          ctx_len, c_row, tm_c)

        att_x = _attn_call(q_x, [(k_c, v_c, ctx_len), (k_x, v_x, seq_len)], seq_len, tq)
        yf_x = _fourier_call(uf_x, cs, dft_x, seq_len, min(512, seq_len))
        yr_c, yr_x = _rnn_call(ur_c, ur_x, gug_c, gug_x, rcw, rcb, wg, bg, rg_lambda[l], ctx_len, seq_len)

        xs = _merge_call(xs, att_x, yf_x, yr_x, g_x, mod_l, woa, wof, wor, wout, x_row(tm_x), tm_x)
        xs = _ffn_call(xs, mod_l, nf, wup, fcw, fcb, wdn, fin, seq_len, x_row(tm_ffn), tm_ffn, last)

        if not last:
            att_c = _attn_call(q_c, [(k_c, v_c, ctx_len)], ctx_len, min(_ATTN_SUB_ROWS, ctx_len))
            yf_c = _fourier_call(uf_c, cs, dft_c, ctx_len, ctx_len)
            cs_tok = _merge_call(cs_tok, att_c, yf_c, yr_c, g_c, mod_l, woa, wof, wor, wout, c_row, tm_c)
            cs_tok = _ffn_call(cs_tok, mod_l, nf, wup, fcw, fcb, wdn, fin, ctx_len, c_row, ctx_len, False)

    return xs.reshape(batch, seq_len, d)
```

```python
import functools
import math

import jax
import jax.numpy as jnp
import numpy as np
from jax import lax
from jax.experimental import pallas as pl
from jax.experimental.pallas import tpu as pltpu

F32 = jnp.float32
BF16 = jnp.bfloat16

D_MODEL = 1024
GRID_W = 64
N_HEADS = 8
Q_LORA = 384
KV_LORA = 256
QK_NOPE = 64
QK_ROPE = 32
V_DIM = 64
QK_DIM = QK_NOPE + QK_ROPE
ROPE_BASE = 10000.0
F_GROUPS = 4
F_GROUP_W = 128
F_WIDTH = F_GROUPS * F_GROUP_W
RNN_WIDTH = 512
RNN_HEADS = 8
RNN_BLOCK = RNN_WIDTH // RNN_HEADS
RNN_CONV = 4
LRU_C = 8.0
D_FF = 2816
FFN_CONV = 3
N_MOD = 6
EPS = 1e-6

LANES = 128
SUBLANES = 8
HEAD_PAD = 128
MOD_ROWS = 16

_O_UF = 0
_O_UR = _O_UF + F_WIDTH
_O_UG = _O_UR + RNN_WIDTH
_O_GL = _O_UG + RNN_WIDTH

_VMEM_LIMIT = 56 * 2 ** 20


_INTERLEAVE = None


def _cparams(sem, vmem=_VMEM_LIMIT, flags=None):
    return pltpu.CompilerParams(dimension_semantics=sem, vmem_limit_bytes=vmem, flags=flags)


def _resident(shape):
    nd = len(shape)
    return pl.BlockSpec(shape, lambda *_: (0,) * nd, pipeline_mode=pl.Buffered(1))


def _dot(a, b):
    return jnp.dot(a, b, preferred_element_type=F32)


def _sigmoid(x):
    return 0.5 * (1.0 + jnp.tanh(0.5 * x))


def _gelu(x):
    return 0.5 * x * (1.0 + jnp.tanh(math.sqrt(2.0 / math.pi) * (x + 0.044715 * (x * x * x))))


def _rms(x, g):
    return x * lax.rsqrt(jnp.mean(x * x, axis=-1, keepdims=True) + EPS) * g


def _ada_kernel(cc_ref, w_ref, b_ref, o_ref):
    cc = cc_ref[...]
    s = cc * _sigmoid(cc)
    o_ref[...] = _dot(s.astype(BF16), w_ref[...].astype(BF16)) + b_ref[...]


def _ada_call(cc, w_ada, b_ada):
    depth, d, n = w_ada.shape
    tn = 1536
    return pl.pallas_call(
        _ada_kernel,
        grid=(depth, n // tn),
        in_specs=[
            pl.BlockSpec((MOD_ROWS, d), lambda l, j: (0, 0)),
            pl.BlockSpec((None, d, tn), lambda l, j: (l, 0, j)),
            pl.BlockSpec((None, 1, tn), lambda l, j: (l, 0, j)),
        ],
        out_specs=pl.BlockSpec((None, MOD_ROWS, tn), lambda l, j: (l, 0, j)),
        out_shape=jax.ShapeDtypeStruct((depth, MOD_ROWS, n), F32),
        compiler_params=_cparams(("arbitrary", "arbitrary")),
        name="adaln",
    )(cc, w_ada, b_ada.reshape(depth, 1, n))


def _in_kernel(x_ref, g_ref, shift_ref, scale_ref, wf_ref, wkr_ref, w_ref, qn_ref, kvn_ref, wuq_ref, wukv_ref,
               cq_ref, sq_ref, ck_ref, sk_ref, vone_ref,
               q_out, k_out, v_out, uf_out, ur_out, gug_out, g_out):
    h = _rms(x_ref[...], g_ref[...]) * (1.0 + scale_ref[...]) + shift_ref[...]
    hb = h.astype(BF16)
    hp = N_HEADS * HEAD_PAD
    first_half = lax.broadcasted_iota(jnp.int32, (x_ref.shape[0], HEAD_PAD), 1) < QK_NOPE + QK_ROPE // 2

    def swap_rot(y):
        fwd = pltpu.roll(y, QK_ROPE // 2, axis=1)
        bwd = pltpu.roll(y, y.shape[1] - QK_ROPE // 2, axis=1)
        return [jnp.where(first_half, bwd[:, a:a + HEAD_PAD], fwd[:, a:a + HEAD_PAD])
                for a in range(0, y.shape[1], HEAD_PAD)]

    def gate_dot(j):
        return _dot(hb, w_ref[:, _O_GL + j * D_MODEL:_O_GL + (j + 1) * D_MODEL])

    def gate_out(j, z):
        g_out[:, j * D_MODEL:(j + 1) * D_MODEL] = _sigmoid(z).astype(BF16)

    cq = _dot(hb, wf_ref[:, 0:Q_LORA])
    ckv = _dot(hb, wf_ref[:, Q_LORA:Q_LORA + KV_LORA])
    kr = _dot(hb, wkr_ref[...])
    z0 = gate_dot(0)

    cqn = _rms(cq, qn_ref[...]).astype(BF16)
    ckvn = _rms(ckv, kvn_ref[...]).astype(BF16)
    y = _dot(cqn, wuq_ref[...])
    kv = _dot(ckvn, wukv_ref[...])
    gate_out(0, z0)
    z1 = gate_dot(1)

    y_sw = swap_rot(y)
    cq_t = cq_ref[...]
    sq_t = sq_ref[...]
    for hd in range(N_HEADS):
        a = hd * HEAD_PAD
        q_out[:, a:a + HEAD_PAD] = (y[:, a:a + HEAD_PAD] * cq_t + y_sw[hd] * sq_t).astype(BF16)
    z2 = gate_dot(2)

    k_rope = kr * ck_ref[...] + swap_rot(kr)[0] * sk_ref[...]
    for hd in range(N_HEADS):
        a = hd * HEAD_PAD
        k_out[:, a:a + HEAD_PAD] = (kv[:, a:a + HEAD_PAD] + k_rope).astype(BF16)
    v_out[...] = (kv[:, hp:] + vone_ref[...]).astype(BF16)
    uf = _dot(hb, w_ref[:, _O_UF:_O_UF + F_WIDTH])
    ur = _dot(hb, w_ref[:, _O_UR:_O_UR + RNN_WIDTH])
    gate_out(1, z1)
    ug = _dot(hb, w_ref[:, _O_UG:_O_UG + RNN_WIDTH])
    gate_out(2, z2)
    uf_out[...] = uf.astype(BF16)
    ur_out[...] = ur
    gug_out[...] = _gelu(ug).astype(BF16)


def _in_call(x2d, mod_l, norm_g, w_in_p, qn, kvn, wuq, wukv, tabs, seq_len, mod_row, tm):
    rows, d = x2d.shape
    tiles_per_seq = seq_len // tm
    hp = N_HEADS * HEAD_PAD

    def row_spec(width):
        return pl.BlockSpec((tm, width), lambda i: (i, 0))

    def mod_spec(chunk):
        return pl.BlockSpec((None, None, 1, d), lambda i: (mod_row(i), chunk, 0, 0))

    tab_spec = pl.BlockSpec((tm, HEAD_PAD), lambda i: (i % tiles_per_seq, 0))
    out_shapes = (
        jax.ShapeDtypeStruct((rows, hp), BF16),
        jax.ShapeDtypeStruct((rows, hp), BF16),
        jax.ShapeDtypeStruct((rows, hp), BF16),
        jax.ShapeDtypeStruct((rows, F_WIDTH), BF16),
        jax.ShapeDtypeStruct((rows, RNN_WIDTH), F32),
        jax.ShapeDtypeStruct((rows, RNN_WIDTH), BF16),
        jax.ShapeDtypeStruct((rows, 3 * d), BF16),
    )
    return pl.pallas_call(
        _in_kernel,
        grid=(rows // tm,),
        in_specs=[
            row_spec(d),
            _resident((1, d)),
            mod_spec(0),
            mod_spec(1),
            _resident(w_in_p[0].shape), _resident(w_in_p[1].shape), _resident(w_in_p[2].shape),
            _resident((1, Q_LORA)),
            _resident((1, KV_LORA)),
            _resident(wuq.shape),
            _resident(wukv.shape),
            tab_spec, tab_spec, tab_spec, tab_spec,
            _resident((1, hp)),
        ],
        out_specs=tuple(row_spec(s.shape[1]) for s in out_shapes),
        out_shape=out_shapes,
        compiler_params=_cparams(("arbitrary",)),
        name="in_proj",
    )(x2d, norm_g, mod_l, mod_l, *w_in_p, qn, kvn, wuq, wukv, *tabs, _value_ones())


_ATTN_SUB_ROWS = 256


def _attn_kernel(*refs, n_seg):
    q_ref = refs[0]
    k_refs = refs[1:1 + n_seg]
    v_refs = refs[1 + n_seg:1 + 2 * n_seg]
    o_ref = refs[1 + 2 * n_seg]
    nt = (((1,), (1,)), ((), ()))
    sub = _ATTN_SUB_ROWS
    low = lax.broadcasted_iota(jnp.int32, (sub, HEAD_PAD), 1) < V_DIM

    def scores(item):
        r0, hd = item
        a = hd * HEAD_PAD
        qh = q_ref[r0:r0 + sub, a:a + HEAD_PAD]
        return [lax.dot_general(qh, k[:, a:a + HEAD_PAD], nt, preferred_element_type=F32) for k in k_refs]

    items = [(r0, hd) for r0 in range(0, q_ref.shape[0], sub) for hd in range(N_HEADS)]
    outs = {}
    s_next = scores(items[0])
    for idx, (r0, hd) in enumerate(items):
        a = hd * HEAD_PAD
        s = s_next
        if idx + 1 < len(items):
            s_next = scores(items[idx + 1])
        m = functools.reduce(jnp.maximum, [jnp.max(si, axis=-1, keepdims=True) for si in s])
        p = [jnp.exp2((si - m).astype(BF16)) for si in s]
        o = functools.reduce(jnp.add, [_dot(pi, v[:, a:a + HEAD_PAD]) for pi, v in zip(p, v_refs)])
        one_lane = V_DIM if hd % 2 == 0 else 0
        outs[hd] = o * (1.0 / o[:, one_lane:one_lane + 1])
        if hd % 2 == 1:
            o_ref[r0:r0 + sub, (hd - 1) * V_DIM:(hd + 1) * V_DIM] = jnp.where(low, outs.pop(hd - 1), outs.pop(hd)).astype(BF16)


def _attn_call(q, segs, q_len, tq):
    rows, hp = q.shape
    tiles = q_len // tq
    n_seg = len(segs)
    k_specs = [pl.BlockSpec((ln, hp), lambda i: (i // tiles, 0)) for _, _, ln in segs]
    return pl.pallas_call(
        functools.partial(_attn_kernel, n_seg=n_seg),
        grid=(rows // tq,),
        in_specs=[pl.BlockSpec((tq, hp), lambda i: (i, 0))] + k_specs + k_specs,
        out_specs=pl.BlockSpec((tq, N_HEADS * V_DIM), lambda i: (i, 0)),
        out_shape=jax.ShapeDtypeStruct((rows, N_HEADS * V_DIM), BF16),
        compiler_params=_cparams(("arbitrary",)),
        name="attention",
    )(q, *[k for k, _, _ in segs], *[v for _, v, _ in segs])


def _fourier_kernel(u_ref, cs_ref, dft_ref, o_ref, ab_ref, *, seq_len):
    @pl.when(pl.program_id(1) == 0)
    def _():
        ab = _dot(u_ref[...], cs_ref[...])
        ab_ref[:seq_len, :] = ab[:, :F_WIDTH].astype(BF16)
        ab_ref[seq_len:, :] = ab[:, F_WIDTH:].astype(BF16)

    o_ref[...] = _dot(dft_ref[...], ab_ref[...]).astype(BF16)


def _fourier_call(uf, cs, dft, seq_len, tmf):
    rows = uf.shape[0]
    tiles = seq_len // tmf
    return pl.pallas_call(
        functools.partial(_fourier_kernel, seq_len=seq_len),
        grid=(rows // seq_len, tiles),
        in_specs=[
            pl.BlockSpec((seq_len, F_WIDTH), lambda b, m: (b, 0)),
            _resident(cs.shape),
            pl.BlockSpec((tmf, 2 * seq_len), lambda b, m: (m, 0)),
        ],
        out_specs=pl.BlockSpec((tmf, F_WIDTH), lambda b, m: (b * tiles + m, 0)),
        out_shape=jax.ShapeDtypeStruct((rows, F_WIDTH), BF16),
        scratch_shapes=[pltpu.VMEM((2 * seq_len, F_WIDTH), BF16)],
        compiler_params=_cparams(("arbitrary", "arbitrary")),
        name="fourier",
    )(uf, cs, dft)


_RNN_HALF = RNN_WIDTH // 2
_RNN_CHUNK = 256


def _rnn_kernel(urc_ref, urx_ref, gugc_ref, gugx_ref, cw_ref, cb_ref, wg_ref, bg_ref, lam_ref,
                yc_ref, yx_ref, pad_ref, af_ref, bf_ref, ab_ref, bb_ref, *, ctx_len, seq_len):
    half = _RNN_HALF
    row8 = lax.broadcasted_iota(jnp.int32, (SUBLANES, half), 0)
    zeros8 = jnp.zeros((SUBLANES, half), F32)

    def gates(ur_ref, n, j):
        lo = j * half
        pad_ref[0:SUBLANES, :] = zeros8
        pad_ref[SUBLANES:SUBLANES + n, :] = ur_ref[:, lo:lo + half]
        pad_ref[SUBLANES + n:2 * SUBLANES + n, :] = zeros8
        cw = cw_ref[:, lo:lo + half]
        cb = cb_ref[:, lo:lo + half]
        lam = lam_ref[:, lo:lo + half]
        k = (-0.5 * LRU_C * math.log2(math.e)) * (jnp.maximum(-lam, 0.0) + jnp.log1p(jnp.exp(-jnp.abs(lam))))
        wg = wg_ref[j]
        bg = bg_ref[j]
        ext_rows = _RNN_CHUNK + 2 * SUBLANES

        def chunk(c, carry):
            r0 = pl.multiple_of(c * _RNN_CHUNK, _RNN_CHUNK)
            ext = pad_ref[pl.ds(r0, ext_rows), :]
            sl = slice(SUBLANES, SUBLANES + _RNN_CHUNK)
            uc = (cb + cw[0:1] * pltpu.roll(ext, 2, axis=0)[sl] + cw[1:2] * pltpu.roll(ext, 1, axis=0)[sl]
                  + cw[2:3] * ext[sl] + cw[3:4] * pltpu.roll(ext, ext_rows - 1, axis=0)[sl])
            g = _dot(uc.astype(BF16), wg) + bg
            uch = 0.5 * uc
            for d, (a_ref, b_ref) in enumerate(((af_ref, bf_ref), (ab_ref, bb_ref))):
                kd = k[d:d + 1]
                a = jnp.exp2(kd + kd * jnp.tanh(g[:, (2 * d) * half:(2 * d + 1) * half]))
                a_ref[pl.ds(r0, _RNN_CHUNK), :] = a
                om = 1.0 - a * a
                sq = om * lax.rsqrt(jnp.maximum(om, 1e-30))
                b_ref[pl.ds(r0, _RNN_CHUNK), :] = (sq * uch) * (1.0 + jnp.tanh(g[:, (2 * d + 1) * half:(2 * d + 2) * half]))
            return carry

        lax.fori_loop(0, n // _RNN_CHUNK, chunk, 0)

    def scan(n, hf, hb):
        nb = n // SUBLANES

        def block(i, carry):
            hf, hb = carry
            rf = pl.multiple_of(i * SUBLANES, SUBLANES)
            a = af_ref[pl.ds(rf, SUBLANES), :]
            b = bf_ref[pl.ds(rf, SUBLANES), :]
            for d in (1, 2, 4):
                ok = row8 >= d
                b = jnp.where(ok, a * pltpu.roll(b, d, axis=0) + b, b)
                a = jnp.where(ok, a * pltpu.roll(a, d, axis=0), a)
            h = a * hf + b
            bf_ref[pl.ds(rf, SUBLANES), :] = h
            hf = jnp.broadcast_to(h[SUBLANES - 1:SUBLANES, :], h.shape)

            rb = pl.multiple_of((nb - 1 - i) * SUBLANES, SUBLANES)
            a = ab_ref[pl.ds(rb, SUBLANES), :]
            b = bb_ref[pl.ds(rb, SUBLANES), :]
            for d in (1, 2, 4):
                ok = row8 < SUBLANES - d
                b = jnp.where(ok, a * pltpu.roll(b, SUBLANES - d, axis=0) + b, b)
                a = jnp.where(ok, a * pltpu.roll(a, SUBLANES - d, axis=0), a)
            h = a * hb + b
            bb_ref[pl.ds(rb, SUBLANES), :] = h
            hb = jnp.broadcast_to(h[0:1, :], h.shape)
            return hf, hb

        return lax.fori_loop(0, nb, block, (hf, hb), unroll=2)

    for j in range(RNN_WIDTH // half):
        lo = j * half
        gates(urc_ref, ctx_len, j)
        hf, hb = scan(ctx_len, zeros8, zeros8)
        yc_ref[:, lo:lo + half] = ((bf_ref[0:ctx_len, :] + bb_ref[0:ctx_len, :])
                                   * gugc_ref[:, lo:lo + half].astype(F32)).astype(BF16)
        gates(urx_ref, seq_len, j)
        scan(seq_len, hf, hb)
        yx_ref[:, lo:lo + half] = ((bf_ref[0:seq_len, :] + bb_ref[0:seq_len, :])
                                   * gugx_ref[:, lo:lo + half].astype(F32)).astype(BF16)


def _rnn_call(ur_c, ur_x, gug_c, gug_x, cw, cb, wg, bg, lam, ctx_len, seq_len):
    batch = ur_x.shape[0] // seq_len
    w = RNN_WIDTH

    def seq_spec(n):
        return pl.BlockSpec((n, w), lambda b: (b, 0))

    return pl.pallas_call(
        functools.partial(_rnn_kernel, ctx_len=ctx_len, seq_len=seq_len),
        grid=(batch,),
        in_specs=[seq_spec(ctx_len), seq_spec(seq_len), seq_spec(ctx_len), seq_spec(seq_len),
                  _resident(cw.shape), _resident(cb.shape), _resident(wg.shape), _resident(bg.shape),
                  _resident(lam.shape)],
        out_specs=(seq_spec(ctx_len), seq_spec(seq_len)),
        out_shape=(jax.ShapeDtypeStruct((batch * ctx_len, w), BF16),
                   jax.ShapeDtypeStruct((batch * seq_len, w), BF16)),
        scratch_shapes=[pltpu.VMEM((seq_len + 2 * SUBLANES, _RNN_HALF), F32)]
        + [pltpu.VMEM((seq_len, _RNN_HALF), F32)] * 4,
        compiler_params=_cparams(("arbitrary",)),
        name="rglru",
    )(ur_c, ur_x, gug_c, gug_x, cw, cb, wg, bg, lam)


_MERGE_SUB_ROWS = 256


def _merge_kernel(x_ref, att_ref, yf_ref, yr_ref, g_ref, gate_ref, woa_ref, wof_ref, wor_ref, wout_ref, o_ref):
    d = D_MODEL
    tm = x_ref.shape[0]
    sub = min(_MERGE_SUB_ROWS, tm)

    def branches(r0):
        rows = slice(r0, r0 + sub)
        return (_dot(att_ref[rows, :], woa_ref[...]), _dot(yf_ref[rows, :], wof_ref[...]),
                _dot(yr_ref[rows, :], wor_ref[...]))

    nxt = branches(0)
    for r0 in range(0, tm, sub):
        rows = slice(r0, r0 + sub)
        ya, yf, yr = nxt
        if r0 + sub < tm:
            nxt = branches(r0 + sub)
        m = (g_ref[rows, 0:d].astype(F32) * ya + g_ref[rows, d:2 * d].astype(F32) * yf
             + g_ref[rows, 2 * d:3 * d].astype(F32) * yr)
        o_ref[rows, :] = x_ref[rows, :] + gate_ref[...] * _dot(m.astype(BF16), wout_ref[...])


def _merge_call(x2d, att, yf, yr, g, mod_l, woa, wof, wor, wout, mod_row, tm):
    rows, d = x2d.shape

    def row_spec(width):
        return pl.BlockSpec((tm, width), lambda i: (i, 0))

    return pl.pallas_call(
        _merge_kernel,
        grid=(rows // tm,),
        in_specs=[row_spec(d), row_spec(att.shape[1]), row_spec(yf.shape[1]), row_spec(yr.shape[1]),
                  row_spec(3 * d),
                  pl.BlockSpec((None, None, 1, d), lambda i: (mod_row(i), 2, 0, 0)),
                  _resident(woa.shape), _resident(wof.shape), _resident(wor.shape), _resident(wout.shape)],
        out_specs=row_spec(d),
        out_shape=jax.ShapeDtypeStruct((rows, d), F32),
        compiler_params=_cparams(("arbitrary",)),
        name="merge",
    )(x2d, att, yf, yr, g, mod_l, woa, wof, wor, wout)


_FF_CHUNK = 256
_FF_ROW_BLOCKS = 2


def _ffn_kernel(x_ref, xp_ref, xn_ref, g_ref, shift_ref, scale_ref, gate_ref,
                wu_ref, cw_ref, cb_ref, wd_ref, fin_ref, o_ref, h_ref, u0_ref, u1_ref, a_ref, *, tm, seq_len, final_norm):
    i = pl.program_id(0)
    tiles_per_seq = max(1, seq_len // tm)
    ext = tm + 2 * SUBLANES
    n_chunks = D_FF // _FF_CHUNK
    fc = _FF_CHUNK

    def prep(x):
        return (_rms(x, g_ref[...]) * (1.0 + scale_ref[...]) + shift_ref[...]).astype(BF16)

    first = (i % tiles_per_seq) == 0
    last = (i % tiles_per_seq) == tiles_per_seq - 1
    def cols(c):
        return pl.ds(pl.multiple_of(c * fc, fc), fc)

    def up_proj_rows(t, u_ref, lo, hi):
        for j in range(2):
            u_ref[lo:hi, j * fc:(j + 1) * fc] = _dot(h_ref[lo:hi, :], wu_ref[:, cols(j * n_chunks + t)])

    sp = -(-(SUBLANES + tm // 2) // (2 * SUBLANES)) * (2 * SUBLANES)
    hp = prep(xp_ref[...])
    h_ref[0:SUBLANES, :] = jnp.where(first, jnp.zeros_like(hp), hp)
    h_ref[SUBLANES:sp, :] = prep(x_ref[0:sp - SUBLANES, :])
    up_proj_rows(0, u0_ref, 0, sp)
    hn = prep(xn_ref[...])
    h_ref[sp:SUBLANES + tm, :] = prep(x_ref[sp - SUBLANES:tm, :])
    h_ref[SUBLANES + tm:, :] = jnp.where(last, jnp.zeros_like(hn), hn)
    up_proj_rows(0, u0_ref, sp, ext)

    n_blk = _FF_ROW_BLOCKS
    hm = tm // n_blk
    hme = hm + 2 * SUBLANES
    slh = slice(SUBLANES, SUBLANES + hm)

    def conv(u, cw, cb, r):
        prev = pltpu.roll(u, 1, axis=0)[slh]
        nxt = pltpu.roll(u, hme - 1, axis=0)[slh]
        if seq_len < tm:
            pos = (r * hm + lax.broadcasted_iota(jnp.int32, prev.shape, 0)) % seq_len
            prev = jnp.where(pos == 0, 0.0, prev)
            nxt = jnp.where(pos == seq_len - 1, 0.0, nxt)
        return cb + cw[0:1] * prev + cw[1:2] * u[slh] + cw[2:3] * nxt

    def activate(c, u_ref, r):
        rows = slice(r * hm, r * hm + hme)
        up = conv(u_ref[rows, 0:fc], cw_ref[:, cols(c)], cb_ref[:, cols(c)], r)
        gt = conv(u_ref[rows, fc:2 * fc], cw_ref[:, cols(n_chunks + c)], cb_ref[:, cols(n_chunks + c)], r)
        a_ref[r * hm:(r + 1) * hm, cols(c)] = _gelu(gt.astype(BF16)) * up.astype(BF16)

    u_bufs = (u0_ref, u1_ref)

    def step(t, parity, do_up=True, do_act=True):
        for r in range(n_blk):
            if do_up and r % (n_blk // 2) == 0:
                j = r // (n_blk // 2)
                u_bufs[parity][:, j * fc:(j + 1) * fc] = _dot(h_ref[...], wu_ref[:, cols(j * n_chunks + t)])
            if do_act:
                activate(t - 1, u_bufs[1 - parity], r)

    assert n_chunks % 2 == 1

    def body(k, carry):
        t = 2 * k + 1
        step(t, 1)
        step(t + 1, 0)
        return carry

    lax.fori_loop(0, n_chunks // 2, body, 0)
    for r in range(n_blk):
        activate(n_chunks - 1, u0_ref, r)
        rows = slice(r * hm, (r + 1) * hm)
        y = x_ref[rows, :] + gate_ref[...] * _dot(a_ref[rows, :], wd_ref[...])
        if final_norm:
            y = _rms(y, fin_ref[...])
        o_ref[rows, :] = y


def _ffn_call(x2d, mod_l, norm_g, wu, cw, cb, wd, fin_g, seq_len, mod_row, tm, final_norm):
    rows, d = x2d.shape
    assert seq_len % tm == 0 or tm % seq_len == 0
    blk8 = tm // SUBLANES
    n_blk8 = rows // SUBLANES

    def mod_spec(chunk):
        return pl.BlockSpec((None, None, 1, d), lambda i: (mod_row(i), chunk, 0, 0))

    return pl.pallas_call(
        functools.partial(_ffn_kernel, tm=tm, seq_len=seq_len, final_norm=final_norm),
        grid=(rows // tm,),
        in_specs=[
            pl.BlockSpec((tm, d), lambda i: (i, 0)),
            pl.BlockSpec((SUBLANES, d), lambda i: (jnp.maximum(i * blk8 - 1, 0), 0)),
            pl.BlockSpec((SUBLANES, d), lambda i: (jnp.minimum((i + 1) * blk8, n_blk8 - 1), 0)),
            _resident((1, d)),
            mod_spec(3), mod_spec(4), mod_spec(5),
            _resident(wu.shape), _resident(cw.shape), _resident(cb.shape), _resident(wd.shape),
            _resident((1, d)),
        ],
        out_specs=pl.BlockSpec((tm, d), lambda i: (i, 0)),
        out_shape=jax.ShapeDtypeStruct((rows, d), F32),
        scratch_shapes=[pltpu.VMEM((tm + 2 * SUBLANES, d), BF16),
                        pltpu.VMEM((tm + 2 * SUBLANES, 2 * _FF_CHUNK), F32),
                        pltpu.VMEM((tm + 2 * SUBLANES, 2 * _FF_CHUNK), F32),
                        pltpu.VMEM((tm, D_FF), BF16)],
        compiler_params=_cparams(("arbitrary",), flags=_INTERLEAVE),
        name="conv_ffn",
    )(x2d, x2d, x2d, norm_g, mod_l, mod_l, mod_l, wu, cw, cb, wd, fin_g)


def _prep_ffn(w_up, cw, cb, w_down):
    return w_up.astype(BF16), cw, cb[None, :], w_down.astype(BF16)


def _rope_tables(seq_len, rope):
    scale = QK_DIM ** -0.5 * math.log2(math.e)
    if rope:
        n_rows = seq_len // GRID_W
        row = jnp.repeat(jnp.arange(n_rows, dtype=F32), GRID_W)
        col = jnp.tile(jnp.arange(GRID_W, dtype=F32), n_rows)
        n_freq = QK_ROPE // 4
        inv = ROPE_BASE ** (-jnp.arange(n_freq, dtype=F32) / n_freq)
        ang = jnp.concatenate([row[:, None] * inv, col[:, None] * inv], axis=-1)
        cos, sin = jnp.cos(ang), jnp.sin(ang)
    else:
        cos = jnp.ones((seq_len, QK_ROPE // 2), F32)
        sin = jnp.zeros((seq_len, QK_ROPE // 2), F32)
    ones = jnp.ones((seq_len, QK_NOPE), F32)
    zeros = jnp.zeros((seq_len, QK_NOPE), F32)
    tail = jnp.zeros((seq_len, HEAD_PAD - QK_DIM), F32)
    cos2 = jnp.concatenate([cos, cos], axis=-1)
    sin2 = jnp.concatenate([-sin, sin], axis=-1)
    cq = scale * jnp.concatenate([ones, cos2, tail], axis=-1)
    sq = scale * jnp.concatenate([zeros, sin2, tail], axis=-1)
    ck = jnp.concatenate([zeros, cos2, tail], axis=-1)
    sk = jnp.concatenate([zeros, sin2, tail], axis=-1)
    return cq, sq, ck, sk


def _value_ones():
    lane = np.arange(N_HEADS * HEAD_PAD) % HEAD_PAD
    head = np.arange(N_HEADS * HEAD_PAD) // HEAD_PAD
    return jnp.asarray(np.where(head % 2 == 0, lane == V_DIM, lane == 0)[None, :], F32)


def _prep_w_in(w):
    d = w.shape[0]
    n_front = Q_LORA + KV_LORA
    kr = w[:, n_front:n_front + QK_ROPE]
    z_lo = jnp.zeros((d, QK_NOPE), w.dtype)
    z_hi = jnp.zeros((d, HEAD_PAD - QK_DIM), w.dtype)
    w_kr = jnp.concatenate([z_lo, kr, z_hi], axis=-1)
    return w[:, :n_front].astype(BF16), w_kr.astype(BF16), w[:, n_front + QK_ROPE:].astype(BF16)


def _prep_w_uq(w):
    r, h, _ = w.shape
    pad = jnp.zeros((r, h, HEAD_PAD - QK_DIM), w.dtype)
    y1 = jnp.concatenate([w, pad], axis=-1)
    return y1.reshape(r, h * HEAD_PAD).astype(BF16)


def _prep_w_ukv(w):
    r, h, _ = w.shape
    k = jnp.concatenate([w[..., :QK_NOPE], jnp.zeros((r, h, HEAD_PAD - QK_NOPE), w.dtype)], axis=-1)
    v = w[..., QK_NOPE:]
    z = jnp.zeros_like(v)
    odd = (jnp.arange(h) % 2 == 1)[None, :, None]
    vp = jnp.concatenate([jnp.where(odd, z, v), jnp.where(odd, v, z)], axis=-1)
    return jnp.concatenate([k.reshape(r, h * HEAD_PAD), vp.reshape(r, h * HEAD_PAD)], axis=-1).astype(BF16)


def _prep_gates(w_a, b_a, w_x, b_x):
    half = _RNN_HALF
    heads_per_half = half // RNN_BLOCK

    def blockdiag(w):
        eye = jnp.eye(heads_per_half, dtype=w.dtype)
        return jnp.einsum('hij,hk->hikj', w, eye).reshape(half, half)

    ws, bs = [], []
    for j in range(RNN_WIDTH // half):
        hs = slice(j * heads_per_half, (j + 1) * heads_per_half)
        ls = slice(j * half, (j + 1) * half)
        ws.append(jnp.concatenate([blockdiag(w_a[0, hs]), blockdiag(w_x[0, hs]),
                                   blockdiag(w_a[1, hs]), blockdiag(w_x[1, hs])], axis=-1))
        bs.append(jnp.concatenate([b_a[0, ls], b_x[0, ls], b_a[1, ls], b_x[1, ls]], axis=-1)[None, :])
    return (0.5 * jnp.stack(ws)).astype(BF16), 0.5 * jnp.stack(bs)


def _dft_tables(seq_len):
    lo_n = 32 if seq_len % 32 == 0 else 1
    t = jnp.arange(seq_len, dtype=jnp.int32)

    def table(rows):
        ang = (2.0 * math.pi / seq_len) * ((rows[:, None] * t[None, :]) % seq_len).astype(F32)
        return jnp.cos(ang), jnp.sin(ang)

    c_hi, s_hi = table(jnp.arange(seq_len // lo_n, dtype=jnp.int32) * lo_n)
    c_lo, s_lo = table(jnp.arange(lo_n, dtype=jnp.int32))
    s = seq_len ** -0.5
    cos = (c_hi[:, None, :] * c_lo[None, :, :] - s_hi[:, None, :] * s_lo[None, :, :]).reshape(seq_len, seq_len)
    sin = (s_hi[:, None, :] * c_lo[None, :, :] + c_hi[:, None, :] * s_lo[None, :, :]).reshape(seq_len, seq_len)
    return jnp.concatenate([cos * s, sin * s], axis=-1).astype(BF16)


def _channel_dft():
    k = jnp.arange(F_GROUP_W, dtype=jnp.int32)
    ang = (2.0 * math.pi / F_GROUP_W) * ((k[:, None] * k[None, :]) % F_GROUP_W).astype(F32)
    s = F_GROUP_W ** -0.5
    eye = jnp.eye(F_GROUPS, dtype=F32)
    c = jnp.kron(eye, jnp.cos(ang) * s)
    sn = jnp.kron(eye, jnp.sin(ang) * s)
    return jnp.concatenate([c, -sn], axis=-1).astype(BF16)


def kernel(x, c, ctx, c_ctx, w_ada, b_ada, norm_mix, norm_ffn, w_in, q_norm, kv_norm, w_uq, w_ukv, w_o_attn,
           w_o_fourier, rnn_conv_w, rnn_conv_b, rg_w_a, rg_b_a, rg_w_x, rg_b_x, rg_lambda, w_o_rnn, w_out,
           w_up, ffn_conv_w, ffn_conv_b, w_down, final_norm):
    batch, seq_len, d = x.shape
    ctx_len = ctx.shape[1]
    depth = w_ada.shape[0]
    assert batch + 1 <= MOD_ROWS and d == D_MODEL

    cc = jnp.concatenate([c, c_ctx[None, :], jnp.zeros((MOD_ROWS - batch - 1, d), F32)], axis=0)
    mod = _ada_call(cc, w_ada, b_ada).reshape(depth, MOD_ROWS, N_MOD, 1, d)

    tabs_x = _rope_tables(seq_len, True)
    tabs_c = _rope_tables(ctx_len, False)
    dft_x = _dft_tables(seq_len)
    dft_c = _dft_tables(ctx_len)
    cs = _channel_dft()

    tm_in = min(512, seq_len)
    tm_x = min(512, seq_len)
    tm_c = min(256, ctx_len)
    tq = min(512, seq_len)
    tm_ffn = min(1024, seq_len)
    tm_ffn_c = min(1024, batch * ctx_len)

    def x_row(tm):
        return lambda i: i // (seq_len // tm)

    c_row = lambda i: batch

    xs = x.reshape(batch * seq_len, d)
    cs_tok = ctx.reshape(batch * ctx_len, d)

    for l in range(depth):
        last = l == depth - 1
        mod_l = mod[l]
        w_in_p = _prep_w_in(w_in[l])
        wuq = _prep_w_uq(w_uq[l])
        wukv = _prep_w_ukv(w_ukv[l])
        wg, bg = _prep_gates(rg_w_a[l], rg_b_a[l], rg_w_x[l], rg_b_x[l])
        nm = norm_mix[l][None, :]
        nf = norm_ffn[l][None, :]
        qn = q_norm[l][None, :]
        kvn = kv_norm[l][None, :]
        woa = w_o_attn[l].astype(BF16)
        wof = w_o_fourier[l].astype(BF16)
        wor = w_o_rnn[l].astype(BF16)
        wout = w_out[l].astype(BF16)
        wup, fcw, fcb, wdn = _prep_ffn(w_up[l], ffn_conv_w[l], ffn_conv_b[l], w_down[l])
        rcw = rnn_conv_w[l]
        rcb = rnn_conv_b[l][None, :]
        fin = final_norm[None, :]

        q_x, k_x, v_x, uf_x, ur_x, gug_x, g_x = _in_call(xs, mod_l, nm, w_in_p, qn, kvn, wuq, wukv, tabs_x,
                                                          seq_len, x_row(tm_in), tm_in)
        q_c, k_c, v_c, uf_c, ur_c, gug_c, g_c = _in_call(cs_tok, mod_l, nm, w_in_p, qn, kvn, wuq, wukv, tabs_c,
                                                          ctx_len, c_row, tm_c)

        att_x = _attn_call(q_x, [(k_c, v_c, ctx_len), (k_x, v_x, seq_len)], seq_len, tq)
        yf_x = _fourier_call(uf_x, cs, dft_x, seq_len, min(512, seq_len))
        yr_c, yr_x = _rnn_call(ur_c, ur_x, gug_c, gug_x, rcw, rcb, wg, bg, rg_lambda[l], ctx_len, seq_len)

        xs = _merge_call(xs, att_x, yf_x, yr_x, g_x, mod_l, woa, wof, wor, wout, x_row(tm_x), tm_x)
        xs = _ffn_call(xs, mod_l, nf, wup, fcw, fcb, wdn, fin, seq_len, x_row(tm_ffn), tm_ffn, last)

        if not last:
            att_c = _attn_call(q_c, [(k_c, v_c, ctx_len)], ctx_len, min(_ATTN_SUB_ROWS, ctx_len))
            yf_c = _fourier_call(uf_c, cs, dft_c, ctx_len, ctx_len)
            cs_tok = _merge_call(cs_tok, att_c, yf_c, yr_c, g_c, mod_l, woa, wof, wor, wout, c_row, tm_c)
            cs_tok = _ffn_call(cs_tok, mod_l, nf, wup, fcw, fcb, wdn, fin, ctx_len, c_row, tm_ffn_c, False)

    return xs.reshape(batch, seq_len, d)
```

```python
import functools
import math

import jax
import jax.numpy as jnp
import numpy as np
from jax import lax
from jax.experimental import pallas as pl
from jax.experimental.pallas import tpu as pltpu

F32 = jnp.float32
BF16 = jnp.bfloat16

D_MODEL = 1024
GRID_W = 64
N_HEADS = 8
Q_LORA = 384
KV_LORA = 256
QK_NOPE = 64
QK_ROPE = 32
V_DIM = 64
QK_DIM = QK_NOPE + QK_ROPE
ROPE_BASE = 10000.0
F_GROUPS = 4
F_GROUP_W = 128
F_WIDTH = F_GROUPS * F_GROUP_W
RNN_WIDTH = 512
RNN_HEADS = 8
RNN_BLOCK = RNN_WIDTH // RNN_HEADS
RNN_CONV = 4
LRU_C = 8.0
D_FF = 2816
FFN_CONV = 3
N_MOD = 6
EPS = 1e-6

LANES = 128
SUBLANES = 8
HEAD_PAD = 128
MOD_ROWS = 16

_O_UF = 0
_O_UR = _O_UF + F_WIDTH
_O_UG = _O_UR + RNN_WIDTH
_O_GL = _O_UG + RNN_WIDTH

_VMEM_LIMIT = 56 * 2 ** 20


_INTERLEAVE = None


def _cparams(sem, vmem=_VMEM_LIMIT, flags=None):
    return pltpu.CompilerParams(dimension_semantics=sem, vmem_limit_bytes=vmem, flags=flags)


def _resident(shape):
    nd = len(shape)
    return pl.BlockSpec(shape, lambda *_: (0,) * nd, pipeline_mode=pl.Buffered(1))


def _dot(a, b):
    return jnp.dot(a, b, preferred_element_type=F32)


def _sigmoid(x):
    return 0.5 * (1.0 + jnp.tanh(0.5 * x))


def _gelu(x):
    return 0.5 * x * (1.0 + jnp.tanh(math.sqrt(2.0 / math.pi) * (x + 0.044715 * (x * x * x))))


def _rms(x, g):
    return x * lax.rsqrt(jnp.mean(x * x, axis=-1, keepdims=True) + EPS) * g


def _ada_kernel(cc_ref, w_ref, b_ref, o_ref):
    cc = cc_ref[...]
    s = cc * _sigmoid(cc)
    o_ref[...] = _dot(s.astype(BF16), w_ref[...].astype(BF16)) + b_ref[...]


def _ada_call(cc, w_ada, b_ada):
    depth, d, n = w_ada.shape
    tn = 1536
    return pl.pallas_call(
        _ada_kernel,
        grid=(depth, n // tn),
        in_specs=[
            pl.BlockSpec((MOD_ROWS, d), lambda l, j: (0, 0)),
            pl.BlockSpec((None, d, tn), lambda l, j: (l, 0, j)),
            pl.BlockSpec((None, 1, tn), lambda l, j: (l, 0, j)),
        ],
        out_specs=pl.BlockSpec((None, MOD_ROWS, tn), lambda l, j: (l, 0, j)),
        out_shape=jax.ShapeDtypeStruct((depth, MOD_ROWS, n), F32),
        compiler_params=_cparams(("arbitrary", "arbitrary")),
        name="adaln",
    )(cc, w_ada, b_ada.reshape(depth, 1, n))


def _in_kernel(x_ref, g_ref, shift_ref, scale_ref, wf_ref, wkr_ref, w_ref, qn_ref, kvn_ref, wuq_ref, wukv_ref,
               cq_ref, sq_ref, ck_ref, sk_ref, vone_ref,
               q_out, k_out, v_out, uf_out, ur_out, gug_out, g_out):
    h = _rms(x_ref[...], g_ref[...]) * (1.0 + scale_ref[...]) + shift_ref[...]
    hb = h.astype(BF16)
    hp = N_HEADS * HEAD_PAD
    first_half = lax.broadcasted_iota(jnp.int32, (x_ref.shape[0], HEAD_PAD), 1) < QK_NOPE + QK_ROPE // 2

    def swap_rot(y):
        fwd = pltpu.roll(y, QK_ROPE // 2, axis=1)
        bwd = pltpu.roll(y, y.shape[1] - QK_ROPE // 2, axis=1)
        return [jnp.where(first_half, bwd[:, a:a + HEAD_PAD], fwd[:, a:a + HEAD_PAD])
                for a in range(0, y.shape[1], HEAD_PAD)]

    def gate_dot(j):
        return _dot(hb, w_ref[:, _O_GL + j * D_MODEL:_O_GL + (j + 1) * D_MODEL])

    def gate_out(j, z):
        g_out[:, j * D_MODEL:(j + 1) * D_MODEL] = _sigmoid(z).astype(BF16)

    cq = _dot(hb, wf_ref[:, 0:Q_LORA])
    ckv = _dot(hb, wf_ref[:, Q_LORA:Q_LORA + KV_LORA])
    kr = _dot(hb, wkr_ref[...])
    z0 = gate_dot(0)

    cqn = _rms(cq, qn_ref[...]).astype(BF16)
    ckvn = _rms(ckv, kvn_ref[...]).astype(BF16)
    y = _dot(cqn, wuq_ref[...])
    kv = _dot(ckvn, wukv_ref[...])
    gate_out(0, z0)
    z1 = gate_dot(1)

    y_sw = swap_rot(y)
    cq_t = cq_ref[...]
    sq_t = sq_ref[...]
    for hd in range(N_HEADS):
        a = hd * HEAD_PAD
        q_out[:, a:a + HEAD_PAD] = (y[:, a:a + HEAD_PAD] * cq_t + y_sw[hd] * sq_t).astype(BF16)
    z2 = gate_dot(2)

    k_rope = kr * ck_ref[...] + swap_rot(kr)[0] * sk_ref[...]
    for hd in range(N_HEADS):
        a = hd * HEAD_PAD
        k_out[:, a:a + HEAD_PAD] = (kv[:, a:a + HEAD_PAD] + k_rope).astype(BF16)
    v_out[...] = (kv[:, hp:] + vone_ref[...]).astype(BF16)
    uf = _dot(hb, w_ref[:, _O_UF:_O_UF + F_WIDTH])
    ur = _dot(hb, w_ref[:, _O_UR:_O_UR + RNN_WIDTH])
    gate_out(1, z1)
    ug = _dot(hb, w_ref[:, _O_UG:_O_UG + RNN_WIDTH])
    gate_out(2, z2)
    uf_out[...] = uf.astype(BF16)
    ur_out[...] = ur
    gug_out[...] = _gelu(ug).astype(BF16)


def _in_call(x2d, mod_l, norm_g, w_in_p, qn, kvn, wuq, wukv, tabs, seq_len, mod_row, tm):
    rows, d = x2d.shape
    tiles_per_seq = seq_len // tm
    hp = N_HEADS * HEAD_PAD

    def row_spec(width):
        return pl.BlockSpec((tm, width), lambda i: (i, 0))

    def mod_spec(chunk):
        return pl.BlockSpec((None, None, 1, d), lambda i: (mod_row(i), chunk, 0, 0))

    tab_spec = pl.BlockSpec((tm, HEAD_PAD), lambda i: (i % tiles_per_seq, 0))
    out_shapes = (
        jax.ShapeDtypeStruct((rows, hp), BF16),
        jax.ShapeDtypeStruct((rows, hp), BF16),
        jax.ShapeDtypeStruct((rows, hp), BF16),
        jax.ShapeDtypeStruct((rows, F_WIDTH), BF16),
        jax.ShapeDtypeStruct((rows, RNN_WIDTH), F32),
        jax.ShapeDtypeStruct((rows, RNN_WIDTH), BF16),
        jax.ShapeDtypeStruct((rows, 3 * d), BF16),
    )
    return pl.pallas_call(
        _in_kernel,
        grid=(rows // tm,),
        in_specs=[
            row_spec(d),
            _resident((1, d)),
            mod_spec(0),
            mod_spec(1),
            _resident(w_in_p[0].shape), _resident(w_in_p[1].shape), _resident(w_in_p[2].shape),
            _resident((1, Q_LORA)),
            _resident((1, KV_LORA)),
            _resident(wuq.shape),
            _resident(wukv.shape),
            tab_spec, tab_spec, tab_spec, tab_spec,
            _resident((1, hp)),
        ],
        out_specs=tuple(row_spec(s.shape[1]) for s in out_shapes),
        out_shape=out_shapes,
        compiler_params=_cparams(("arbitrary",)),
        name="in_proj",
    )(x2d, norm_g, mod_l, mod_l, *w_in_p, qn, kvn, wuq, wukv, *tabs, _value_ones())


_ATTN_SUB_ROWS = 256


def _attn_kernel(*refs, n_seg):
    q_ref = refs[0]
    k_refs = refs[1:1 + n_seg]
    v_refs = refs[1 + n_seg:1 + 2 * n_seg]
    o_ref = refs[1 + 2 * n_seg]
    nt = (((1,), (1,)), ((), ()))
    sub = _ATTN_SUB_ROWS
    low = lax.broadcasted_iota(jnp.int32, (sub, HEAD_PAD), 1) < V_DIM

    def scores(item):
        r0, hd = item
        a = hd * HEAD_PAD
        qh = q_ref[r0:r0 + sub, a:a + HEAD_PAD]
        return [lax.dot_general(qh, k[:, a:a + HEAD_PAD], nt, preferred_element_type=F32) for k in k_refs]

    items = [(r0, hd) for r0 in range(0, q_ref.shape[0], sub) for hd in range(N_HEADS)]
    outs = {}
    s_next = scores(items[0])
    for idx, (r0, hd) in enumerate(items):
        a = hd * HEAD_PAD
        s = s_next
        if idx + 1 < len(items):
            s_next = scores(items[idx + 1])
        m = functools.reduce(jnp.maximum, [jnp.max(si, axis=-1, keepdims=True) for si in s])
        p = [jnp.exp2((si - m).astype(BF16)) for si in s]
        o = functools.reduce(jnp.add, [_dot(pi, v[:, a:a + HEAD_PAD]) for pi, v in zip(p, v_refs)])
        one_lane = V_DIM if hd % 2 == 0 else 0
        outs[hd] = o * (1.0 / o[:, one_lane:one_lane + 1])
        if hd % 2 == 1:
            o_ref[r0:r0 + sub, (hd - 1) * V_DIM:(hd + 1) * V_DIM] = jnp.where(low, outs.pop(hd - 1), outs.pop(hd)).astype(BF16)


def _attn_call(q, segs, q_len, tq):
    rows, hp = q.shape
    tiles = q_len // tq
    n_seg = len(segs)
    k_specs = [pl.BlockSpec((ln, hp), lambda i: (i // tiles, 0)) for _, _, ln in segs]
    return pl.pallas_call(
        functools.partial(_attn_kernel, n_seg=n_seg),
        grid=(rows // tq,),
        in_specs=[pl.BlockSpec((tq, hp), lambda i: (i, 0))] + k_specs + k_specs,
        out_specs=pl.BlockSpec((tq, N_HEADS * V_DIM), lambda i: (i, 0)),
        out_shape=jax.ShapeDtypeStruct((rows, N_HEADS * V_DIM), BF16),
        compiler_params=_cparams(("arbitrary",)),
        name="attention",
    )(q, *[k for k, _, _ in segs], *[v for _, v, _ in segs])


def _fourier_kernel(u_ref, cs_ref, dft_ref, o_ref, ab_ref, *, seq_len):
    @pl.when(pl.program_id(1) == 0)
    def _():
        ab = _dot(u_ref[...], cs_ref[...])
        ab_ref[:seq_len, :] = ab[:, :F_WIDTH].astype(BF16)
        ab_ref[seq_len:, :] = ab[:, F_WIDTH:].astype(BF16)

    o_ref[...] = _dot(dft_ref[...], ab_ref[...]).astype(BF16)


def _fourier_call(uf, cs, dft, seq_len, tmf):
    rows = uf.shape[0]
    tiles = seq_len // tmf
    return pl.pallas_call(
        functools.partial(_fourier_kernel, seq_len=seq_len),
        grid=(rows // seq_len, tiles),
        in_specs=[
            pl.BlockSpec((seq_len, F_WIDTH), lambda b, m: (b, 0)),
            _resident(cs.shape),
            pl.BlockSpec((tmf, 2 * seq_len), lambda b, m: (m, 0)),
        ],
        out_specs=pl.BlockSpec((tmf, F_WIDTH), lambda b, m: (b * tiles + m, 0)),
        out_shape=jax.ShapeDtypeStruct((rows, F_WIDTH), BF16),
        scratch_shapes=[pltpu.VMEM((2 * seq_len, F_WIDTH), BF16)],
        compiler_params=_cparams(("arbitrary", "arbitrary")),
        name="fourier",
    )(uf, cs, dft)


_RNN_HALF = RNN_WIDTH // 2
_RNN_CHUNK = 256


_RNN_SEGS = SUBLANES
_RNN_SEG_PAD = SUBLANES


def _rnn_kernel(urc_ref, urx_ref, gugc_ref, gugx_ref, cw_ref, cb_ref, wg_ref, bg_ref, lam_ref,
                yc_ref, yx_ref, pad_ref, af_ref, bf_ref, ab_ref, bb_ref, *, ctx_len, seq_len):
    half = _RNN_HALF
    n_slab = half // LANES
    zeros8 = jnp.zeros((SUBLANES, half), F32)
    row8 = lax.broadcasted_iota(jnp.int32, (_RNN_SEGS, LANES), 0)
    dirs = ((af_ref, bf_ref), (ab_ref, bb_ref))

    def gates(ur_ref, n, j):
        lo = j * half
        seg = n // _RNN_SEGS
        pitch = seg + _RNN_SEG_PAD
        pad_ref[0:SUBLANES, :] = zeros8
        pad_ref[SUBLANES:SUBLANES + n, :] = ur_ref[:, lo:lo + half]
        pad_ref[SUBLANES + n:2 * SUBLANES + n, :] = zeros8
        cw = cw_ref[:, lo:lo + half]
        cb = cb_ref[:, lo:lo + half]
        lam = lam_ref[:, lo:lo + half]
        k = (-0.5 * LRU_C * math.log2(math.e)) * (jnp.maximum(-lam, 0.0) + jnp.log1p(jnp.exp(-jnp.abs(lam))))
        wg = wg_ref[j]
        bg = bg_ref[j]
        ext_rows = _RNN_CHUNK + 2 * SUBLANES

        def put(ref, c, val):
            for sl in range(n_slab):
                v = val[:, sl * LANES:(sl + 1) * LANES]
                if seg == _RNN_CHUNK:
                    ref[sl, pl.ds(pl.multiple_of(c * pitch, SUBLANES), _RNN_CHUNK), :] = v
                else:
                    for sg in range(_RNN_SEGS):
                        ref[sl, sg * pitch:sg * pitch + seg, :] = v[sg * seg:(sg + 1) * seg, :]

        def chunk(c, carry):
            r0 = pl.multiple_of(c * _RNN_CHUNK, _RNN_CHUNK)
            ext = pad_ref[pl.ds(r0, ext_rows), :]
            sl = slice(SUBLANES, SUBLANES + _RNN_CHUNK)
            uc = (cb + cw[0:1] * pltpu.roll(ext, 2, axis=0)[sl] + cw[1:2] * pltpu.roll(ext, 1, axis=0)[sl]
                  + cw[2:3] * ext[sl] + cw[3:4] * pltpu.roll(ext, ext_rows - 1, axis=0)[sl])
            g = _dot(uc.astype(BF16), wg) + bg
            uch = 0.5 * uc
            for d, (a_ref, b_ref) in enumerate(dirs):
                kd = k[d:d + 1]
                a = jnp.exp2(kd + kd * jnp.tanh(g[:, (2 * d) * half:(2 * d + 1) * half]))
                put(a_ref, c, a)
                om = 1.0 - a * a
                sq = om * lax.rsqrt(jnp.maximum(om, 1e-30))
                put(b_ref, c, (sq * uch) * (1.0 + jnp.tanh(g[:, (2 * d + 1) * half:(2 * d + 2) * half])))
            return carry

        assert seg == _RNN_CHUNK or n == _RNN_CHUNK
        lax.fori_loop(0, n // _RNN_CHUNK, chunk, 0)

    def scan(n, h0):
        seg = n // _RNN_SEGS
        pitch = seg + _RNN_SEG_PAD
        ones = jnp.ones((_RNN_SEGS, LANES), F32)
        zeros = jnp.zeros((_RNN_SEGS, LANES), F32)

        def step(i, carry):
            out = []
            for d, (a_ref, b_ref) in enumerate(dirs):
                pos = i if d == 0 else seg - 1 - i
                for sl in range(n_slab):
                    p, h = carry[d * n_slab + sl]
                    rows = pl.ds(pos, _RNN_SEGS, stride=pitch)
                    a = a_ref[sl, rows, :]
                    p = a * p
                    h = a * h + b_ref[sl, rows, :]
                    a_ref[sl, rows, :] = p
                    b_ref[sl, rows, :] = h
                    out.append((p, h))
            return tuple(out)

        ends = lax.fori_loop(0, seg, step, tuple((ones, zeros) for _ in range(2 * n_slab)), unroll=2)

        starts, finals = [], []
        for d in range(2):
            for sl in range(n_slab):
                p, h = ends[d * n_slab + sl]
                cur = h0[d * n_slab + sl]
                hin = zeros
                order = range(_RNN_SEGS) if d == 0 else range(_RNN_SEGS - 1, -1, -1)
                for sg in order:
                    hin = jnp.where(row8 == sg, jnp.broadcast_to(cur, hin.shape), hin)
                    cur = p[sg:sg + 1, :] * cur + h[sg:sg + 1, :]
                starts.append(hin)
                finals.append(cur)
        return starts, finals

    def emit(n, starts, gug_ref, y_ref, lo):
        seg = n // _RNN_SEGS
        pitch = seg + _RNN_SEG_PAD
        for sg in range(_RNN_SEGS):
            src = slice(sg * pitch, sg * pitch + seg)
            dst = slice(sg * seg, (sg + 1) * seg)
            for sl in range(n_slab):
                hsum = None
                for d, (a_ref, b_ref) in enumerate(dirs):
                    hin = starts[d * n_slab + sl][sg:sg + 1, :]
                    hd = a_ref[sl, src, :] * hin + b_ref[sl, src, :]
                    hsum = hd if hsum is None else hsum + hd
                lanes = slice(lo + sl * LANES, lo + (sl + 1) * LANES)
                y_ref[dst, lanes] = (hsum * gug_ref[dst, lanes].astype(F32)).astype(BF16)

    zero_state = [jnp.zeros((1, LANES), F32)] * (2 * n_slab)
    for j in range(RNN_WIDTH // half):
        lo = j * half
        gates(urc_ref, ctx_len, j)
        starts, finals = scan(ctx_len, zero_state)
        emit(ctx_len, starts, gugc_ref, yc_ref, lo)
        gates(urx_ref, seq_len, j)
        starts, _ = scan(seq_len, finals)
        emit(seq_len, starts, gugx_ref, yx_ref, lo)


def _rnn_call(ur_c, ur_x, gug_c, gug_x, cw, cb, wg, bg, lam, ctx_len, seq_len):
    batch = ur_x.shape[0] // seq_len
    w = RNN_WIDTH

    def seq_spec(n):
        return pl.BlockSpec((n, w), lambda b: (b, 0))

    return pl.pallas_call(
        functools.partial(_rnn_kernel, ctx_len=ctx_len, seq_len=seq_len),
        grid=(batch,),
        in_specs=[seq_spec(ctx_len), seq_spec(seq_len), seq_spec(ctx_len), seq_spec(seq_len),
                  _resident(cw.shape), _resident(cb.shape), _resident(wg.shape), _resident(bg.shape),
                  _resident(lam.shape)],
        out_specs=(seq_spec(ctx_len), seq_spec(seq_len)),
        out_shape=(jax.ShapeDtypeStruct((batch * ctx_len, w), BF16),
                   jax.ShapeDtypeStruct((batch * seq_len, w), BF16)),
        scratch_shapes=[pltpu.VMEM((seq_len + 2 * SUBLANES, _RNN_HALF), F32)]
        + [pltpu.VMEM((_RNN_HALF // LANES, seq_len + _RNN_SEGS * _RNN_SEG_PAD, LANES), F32)] * 4,
        compiler_params=_cparams(("arbitrary",)),
        name="rglru",
    )(ur_c, ur_x, gug_c, gug_x, cw, cb, wg, bg, lam)


_MERGE_SUB_ROWS = 256


def _merge_kernel(x_ref, att_ref, yf_ref, yr_ref, g_ref, gate_ref, woa_ref, wof_ref, wor_ref, wout_ref, o_ref):
    d = D_MODEL
    tm = x_ref.shape[0]
    sub = min(_MERGE_SUB_ROWS, tm)

    def branches(r0):
        rows = slice(r0, r0 + sub)
        return (_dot(att_ref[rows, :], woa_ref[...]), _dot(yf_ref[rows, :], wof_ref[...]),
                _dot(yr_ref[rows, :], wor_ref[...]))

    nxt = branches(0)
    for r0 in range(0, tm, sub):
        rows = slice(r0, r0 + sub)
        ya, yf, yr = nxt
        if r0 + sub < tm:
            nxt = branches(r0 + sub)
        m = (g_ref[rows, 0:d].astype(F32) * ya + g_ref[rows, d:2 * d].astype(F32) * yf
             + g_ref[rows, 2 * d:3 * d].astype(F32) * yr)
        o_ref[rows, :] = x_ref[rows, :] + gate_ref[...] * _dot(m.astype(BF16), wout_ref[...])


def _merge_call(x2d, att, yf, yr, g, mod_l, woa, wof, wor, wout, mod_row, tm):
    rows, d = x2d.shape

    def row_spec(width):
        return pl.BlockSpec((tm, width), lambda i: (i, 0))

    return pl.pallas_call(
        _merge_kernel,
        grid=(rows // tm,),
        in_specs=[row_spec(d), row_spec(att.shape[1]), row_spec(yf.shape[1]), row_spec(yr.shape[1]),
                  row_spec(3 * d),
                  pl.BlockSpec((None, None, 1, d), lambda i: (mod_row(i), 2, 0, 0)),
                  _resident(woa.shape), _resident(wof.shape), _resident(wor.shape), _resident(wout.shape)],
        out_specs=row_spec(d),
        out_shape=jax.ShapeDtypeStruct((rows, d), F32),
        compiler_params=_cparams(("arbitrary",)),
        name="merge",
    )(x2d, att, yf, yr, g, mod_l, woa, wof, wor, wout)


_FF_CHUNK = 256
_FF_ROW_BLOCKS = 2


def _ffn_kernel(x_ref, xp_ref, xn_ref, g_ref, shift_ref, scale_ref, gate_ref,
                wu_ref, cw_ref, cb_ref, wd_ref, fin_ref, o_ref, h_ref, u0_ref, u1_ref, a_ref, *, tm, seq_len, final_norm):
    i = pl.program_id(0)
    tiles_per_seq = max(1, seq_len // tm)
    ext = tm + 2 * SUBLANES
    n_chunks = D_FF // _FF_CHUNK
    fc = _FF_CHUNK

    def prep(x):
        return (_rms(x, g_ref[...]) * (1.0 + scale_ref[...]) + shift_ref[...]).astype(BF16)

    first = (i % tiles_per_seq) == 0
    last = (i % tiles_per_seq) == tiles_per_seq - 1
    def cols(c):
        return pl.ds(pl.multiple_of(c * fc, fc), fc)

    def up_proj_rows(t, u_ref, lo, hi):
        for j in range(2):
            u_ref[lo:hi, j * fc:(j + 1) * fc] = _dot(h_ref[lo:hi, :], wu_ref[:, cols(j * n_chunks + t)])

    sp = -(-(SUBLANES + tm // 2) // (2 * SUBLANES)) * (2 * SUBLANES)
    hp = prep(xp_ref[...])
    h_ref[0:SUBLANES, :] = jnp.where(first, jnp.zeros_like(hp), hp)
    h_ref[SUBLANES:sp, :] = prep(x_ref[0:sp - SUBLANES, :])
    up_proj_rows(0, u0_ref, 0, sp)
    hn = prep(xn_ref[...])
    h_ref[sp:SUBLANES + tm, :] = prep(x_ref[sp - SUBLANES:tm, :])
    h_ref[SUBLANES + tm:, :] = jnp.where(last, jnp.zeros_like(hn), hn)
    up_proj_rows(0, u0_ref, sp, ext)

    n_blk = _FF_ROW_BLOCKS
    hm = tm // n_blk
    hme = hm + 2 * SUBLANES
    slh = slice(SUBLANES, SUBLANES + hm)

    def conv(u, cw, cb, r):
        prev = pltpu.roll(u, 1, axis=0)[slh]
        nxt = pltpu.roll(u, hme - 1, axis=0)[slh]
        if seq_len < tm:
            pos = (r * hm + lax.broadcasted_iota(jnp.int32, prev.shape, 0)) % seq_len
            prev = jnp.where(pos == 0, 0.0, prev)
            nxt = jnp.where(pos == seq_len - 1, 0.0, nxt)
        return cb + cw[0:1] * prev + cw[1:2] * u[slh] + cw[2:3] * nxt

    def activate(c, u_ref, r):
        rows = slice(r * hm, r * hm + hme)
        up = conv(u_ref[rows, 0:fc], cw_ref[:, cols(c)], cb_ref[:, cols(c)], r)
        gt = conv(u_ref[rows, fc:2 * fc], cw_ref[:, cols(n_chunks + c)], cb_ref[:, cols(n_chunks + c)], r)
        a_ref[r * hm:(r + 1) * hm, cols(c)] = _gelu(gt.astype(BF16)) * up.astype(BF16)

    u_bufs = (u0_ref, u1_ref)

    def step(t, parity, do_up=True, do_act=True):
        for r in range(n_blk):
            if do_up and r % (n_blk // 2) == 0:
                j = r // (n_blk // 2)
                u_bufs[parity][:, j * fc:(j + 1) * fc] = _dot(h_ref[...], wu_ref[:, cols(j * n_chunks + t)])
            if do_act:
                activate(t - 1, u_bufs[1 - parity], r)

    assert n_chunks % 2 == 1

    def body(k, carry):
        t = 2 * k + 1
        step(t, 1)
        step(t + 1, 0)
        return carry

    lax.fori_loop(0, n_chunks // 2, body, 0)
    for r in range(n_blk):
        activate(n_chunks - 1, u0_ref, r)
        rows = slice(r * hm, (r + 1) * hm)
        y = x_ref[rows, :] + gate_ref[...] * _dot(a_ref[rows, :], wd_ref[...])
        if final_norm:
            y = _rms(y, fin_ref[...])
        o_ref[rows, :] = y


def _ffn_call(x2d, mod_l, norm_g, wu, cw, cb, wd, fin_g, seq_len, mod_row, tm, final_norm):
    rows, d = x2d.shape
    assert seq_len % tm == 0 or tm % seq_len == 0
    blk8 = tm // SUBLANES
    n_blk8 = rows // SUBLANES

    def mod_spec(chunk):
        return pl.BlockSpec((None, None, 1, d), lambda i: (mod_row(i), chunk, 0, 0))

    return pl.pallas_call(
        functools.partial(_ffn_kernel, tm=tm, seq_len=seq_len, final_norm=final_norm),
        grid=(rows // tm,),
        in_specs=[
            pl.BlockSpec((tm, d), lambda i: (i, 0)),
            pl.BlockSpec((SUBLANES, d), lambda i: (jnp.maximum(i * blk8 - 1, 0), 0)),
            pl.BlockSpec((SUBLANES, d), lambda i: (jnp.minimum((i + 1) * blk8, n_blk8 - 1), 0)),
            _resident((1, d)),
            mod_spec(3), mod_spec(4), mod_spec(5),
            _resident(wu.shape), _resident(cw.shape), _resident(cb.shape), _resident(wd.shape),
            _resident((1, d)),
        ],
        out_specs=pl.BlockSpec((tm, d), lambda i: (i, 0)),
        out_shape=jax.ShapeDtypeStruct((rows, d), F32),
        scratch_shapes=[pltpu.VMEM((tm + 2 * SUBLANES, d), BF16),
                        pltpu.VMEM((tm + 2 * SUBLANES, 2 * _FF_CHUNK), F32),
                        pltpu.VMEM((tm + 2 * SUBLANES, 2 * _FF_CHUNK), F32),
                        pltpu.VMEM((tm, D_FF), BF16)],
        compiler_params=_cparams(("arbitrary",), flags=_INTERLEAVE),
        name="conv_ffn",
    )(x2d, x2d, x2d, norm_g, mod_l, mod_l, mod_l, wu, cw, cb, wd, fin_g)


def _prep_ffn(w_up, cw, cb, w_down):
    return w_up.astype(BF16), cw, cb[None, :], w_down.astype(BF16)


def _rope_tables(seq_len, rope):
    scale = QK_DIM ** -0.5 * math.log2(math.e)
    if rope:
        n_rows = seq_len // GRID_W
        row = jnp.repeat(jnp.arange(n_rows, dtype=F32), GRID_W)
        col = jnp.tile(jnp.arange(GRID_W, dtype=F32), n_rows)
        n_freq = QK_ROPE // 4
        inv = ROPE_BASE ** (-jnp.arange(n_freq, dtype=F32) / n_freq)
        ang = jnp.concatenate([row[:, None] * inv, col[:, None] * inv], axis=-1)
        cos, sin = jnp.cos(ang), jnp.sin(ang)
    else:
        cos = jnp.ones((seq_len, QK_ROPE // 2), F32)
        sin = jnp.zeros((seq_len, QK_ROPE // 2), F32)
    ones = jnp.ones((seq_len, QK_NOPE), F32)
    zeros = jnp.zeros((seq_len, QK_NOPE), F32)
    tail = jnp.zeros((seq_len, HEAD_PAD - QK_DIM), F32)
    cos2 = jnp.concatenate([cos, cos], axis=-1)
    sin2 = jnp.concatenate([-sin, sin], axis=-1)
    cq = scale * jnp.concatenate([ones, cos2, tail], axis=-1)
    sq = scale * jnp.concatenate([zeros, sin2, tail], axis=-1)
    ck = jnp.concatenate([zeros, cos2, tail], axis=-1)
    sk = jnp.concatenate([zeros, sin2, tail], axis=-1)
    return cq, sq, ck, sk


def _value_ones():
    lane = np.arange(N_HEADS * HEAD_PAD) % HEAD_PAD
    head = np.arange(N_HEADS * HEAD_PAD) // HEAD_PAD
    return jnp.asarray(np.where(head % 2 == 0, lane == V_DIM, lane == 0)[None, :], F32)


def _prep_w_in(w):
    d = w.shape[0]
    n_front = Q_LORA + KV_LORA
    kr = w[:, n_front:n_front + QK_ROPE]
    z_lo = jnp.zeros((d, QK_NOPE), w.dtype)
    z_hi = jnp.zeros((d, HEAD_PAD - QK_DIM), w.dtype)
    w_kr = jnp.concatenate([z_lo, kr, z_hi], axis=-1)
    return w[:, :n_front].astype(BF16), w_kr.astype(BF16), w[:, n_front + QK_ROPE:].astype(BF16)


def _prep_w_uq(w):
    r, h, _ = w.shape
    pad = jnp.zeros((r, h, HEAD_PAD - QK_DIM), w.dtype)
    y1 = jnp.concatenate([w, pad], axis=-1)
    return y1.reshape(r, h * HEAD_PAD).astype(BF16)


def _prep_w_ukv(w):
    r, h, _ = w.shape
    k = jnp.concatenate([w[..., :QK_NOPE], jnp.zeros((r, h, HEAD_PAD - QK_NOPE), w.dtype)], axis=-1)
    v = w[..., QK_NOPE:]
    z = jnp.zeros_like(v)
    odd = (jnp.arange(h) % 2 == 1)[None, :, None]
    vp = jnp.concatenate([jnp.where(odd, z, v), jnp.where(odd, v, z)], axis=-1)
    return jnp.concatenate([k.reshape(r, h * HEAD_PAD), vp.reshape(r, h * HEAD_PAD)], axis=-1).astype(BF16)


def _prep_gates(w_a, b_a, w_x, b_x):
    half = _RNN_HALF
    heads_per_half = half // RNN_BLOCK

    def blockdiag(w):
        eye = jnp.eye(heads_per_half, dtype=w.dtype)
        return jnp.einsum('hij,hk->hikj', w, eye).reshape(half, half)

    ws, bs = [], []
    for j in range(RNN_WIDTH // half):
        hs = slice(j * heads_per_half, (j + 1) * heads_per_half)
        ls = slice(j * half, (j + 1) * half)
        ws.append(jnp.concatenate([blockdiag(w_a[0, hs]), blockdiag(w_x[0, hs]),
                                   blockdiag(w_a[1, hs]), blockdiag(w_x[1, hs])], axis=-1))
        bs.append(jnp.concatenate([b_a[0, ls], b_x[0, ls], b_a[1, ls], b_x[1, ls]], axis=-1)[None, :])
    return (0.5 * jnp.stack(ws)).astype(BF16), 0.5 * jnp.stack(bs)


def _dft_tables(seq_len):
    lo_n = 32 if seq_len % 32 == 0 else 1
    t = jnp.arange(seq_len, dtype=jnp.int32)

    def table(rows):
        ang = (2.0 * math.pi / seq_len) * ((rows[:, None] * t[None, :]) % seq_len).astype(F32)
        return jnp.cos(ang), jnp.sin(ang)

    c_hi, s_hi = table(jnp.arange(seq_len // lo_n, dtype=jnp.int32) * lo_n)
    c_lo, s_lo = table(jnp.arange(lo_n, dtype=jnp.int32))
    s = seq_len ** -0.5
    cos = (c_hi[:, None, :] * c_lo[None, :, :] - s_hi[:, None, :] * s_lo[None, :, :]).reshape(seq_len, seq_len)
    sin = (s_hi[:, None, :] * c_lo[None, :, :] + c_hi[:, None, :] * s_lo[None, :, :]).reshape(seq_len, seq_len)
    return jnp.concatenate([cos * s, sin * s], axis=-1).astype(BF16)


def _channel_dft():
    k = jnp.arange(F_GROUP_W, dtype=jnp.int32)
    ang = (2.0 * math.pi / F_GROUP_W) * ((k[:, None] * k[None, :]) % F_GROUP_W).astype(F32)
    s = F_GROUP_W ** -0.5
    eye = jnp.eye(F_GROUPS, dtype=F32)
    c = jnp.kron(eye, jnp.cos(ang) * s)
    sn = jnp.kron(eye, jnp.sin(ang) * s)
    return jnp.concatenate([c, -sn], axis=-1).astype(BF16)


def kernel(x, c, ctx, c_ctx, w_ada, b_ada, norm_mix, norm_ffn, w_in, q_norm, kv_norm, w_uq, w_ukv, w_o_attn,
           w_o_fourier, rnn_conv_w, rnn_conv_b, rg_w_a, rg_b_a, rg_w_x, rg_b_x, rg_lambda, w_o_rnn, w_out,
           w_up, ffn_conv_w, ffn_conv_b, w_down, final_norm):
    batch, seq_len, d = x.shape
    ctx_len = ctx.shape[1]
    depth = w_ada.shape[0]
    assert batch + 1 <= MOD_ROWS and d == D_MODEL

    cc = jnp.concatenate([c, c_ctx[None, :], jnp.zeros((MOD_ROWS - batch - 1, d), F32)], axis=0)
    mod = _ada_call(cc, w_ada, b_ada).reshape(depth, MOD_ROWS, N_MOD, 1, d)

    tabs_x = _rope_tables(seq_len, True)
    tabs_c = _rope_tables(ctx_len, False)
    dft_x = _dft_tables(seq_len)
    dft_c = _dft_tables(ctx_len)
    cs = _channel_dft()

    tm_in = min(512, seq_len)
    tm_x = min(512, seq_len)
    tm_c = min(256, ctx_len)
    tq = min(512, seq_len)
    tm_ffn = min(1024, seq_len)
    tm_ffn_c = min(1024, batch * ctx_len)

    def x_row(tm):
        return lambda i: i // (seq_len // tm)

    c_row = lambda i: batch

    xs = x.reshape(batch * seq_len, d)
    cs_tok = ctx.reshape(batch * ctx_len, d)

    for l in range(depth):
        last = l == depth - 1
        mod_l = mod[l]
        w_in_p = _prep_w_in(w_in[l])
        wuq = _prep_w_uq(w_uq[l])
        wukv = _prep_w_ukv(w_ukv[l])
        wg, bg = _prep_gates(rg_w_a[l], rg_b_a[l], rg_w_x[l], rg_b_x[l])
        nm = norm_mix[l][None, :]
        nf = norm_ffn[l][None, :]
        qn = q_norm[l][None, :]
        kvn = kv_norm[l][None, :]
        woa = w_o_attn[l].astype(BF16)
        wof = w_o_fourier[l].astype(BF16)
        wor = w_o_rnn[l].astype(BF16)
        wout = w_out[l].astype(BF16)
        wup, fcw, fcb, wdn = _prep_ffn(w_up[l], ffn_conv_w[l], ffn_conv_b[l], w_down[l])
        rcw = rnn_conv_w[l]
        rcb = rnn_conv_b[l][None, :]
        fin = final_norm[None, :]

        q_x, k_x, v_x, uf_x, ur_x, gug_x, g_x = _in_call(xs, mod_l, nm, w_in_p, qn, kvn, wuq, wukv, tabs_x,
                                                          seq_len, x_row(tm_in), tm_in)
        q_c, k_c, v_c, uf_c, ur_c, gug_c, g_c = _in_call(cs_tok, mod_l, nm, w_in_p, qn, kvn, wuq, wukv, tabs_c,
                                                          ctx_len, c_row, tm_c)

        att_x = _attn_call(q_x, [(k_c, v_c, ctx_len), (k_x, v_x, seq_len)], seq_len, tq)
        yf_x = _fourier_call(uf_x, cs, dft_x, seq_len, min(512, seq_len))
        yr_c, yr_x = _rnn_call(ur_c, ur_x, gug_c, gug_x, rcw, rcb, wg, bg, rg_lambda[l], ctx_len, seq_len)

        xs = _merge_call(xs, att_x, yf_x, yr_x, g_x, mod_l, woa, wof, wor, wout, x_row(tm_x), tm_x)
        xs = _ffn_call(xs, mod_l, nf, wup, fcw, fcb, wdn, fin, seq_len, x_row(tm_ffn), tm_ffn, last)

        if not last:
            att_c = _attn_call(q_c, [(k_c, v_c, ctx_len)], ctx_len, min(_ATTN_SUB_ROWS, ctx_len))
            yf_c = _fourier_call(uf_c, cs, dft_c, ctx_len, ctx_len)
            cs_tok = _merge_call(cs_tok, att_c, yf_c, yr_c, g_c, mod_l, woa, wof, wor, wout, c_row, tm_c)
            cs_tok = _ffn_call(cs_tok, mod_l, nf, wup, fcw, fcb, wdn, fin, ctx_len, c_row, tm_ffn_c, False)

    return xs.reshape(batch, seq_len, d)
```

```python
import functools
import math

import jax
import jax.numpy as jnp
import numpy as np
from jax import lax
from jax.experimental import pallas as pl
from jax.experimental.pallas import tpu as pltpu

F32 = jnp.float32
BF16 = jnp.bfloat16

D_MODEL = 1024
GRID_W = 64
N_HEADS = 8
Q_LORA = 384
KV_LORA = 256
QK_NOPE = 64
QK_ROPE = 32
V_DIM = 64
QK_DIM = QK_NOPE + QK_ROPE
ROPE_BASE = 10000.0
F_GROUPS = 4
F_GROUP_W = 128
F_WIDTH = F_GROUPS * F_GROUP_W
RNN_WIDTH = 512
RNN_HEADS = 8
RNN_BLOCK = RNN_WIDTH // RNN_HEADS
RNN_CONV = 4
LRU_C = 8.0
D_FF = 2816
FFN_CONV = 3
N_MOD = 6
EPS = 1e-6

LANES = 128
SUBLANES = 8
HEAD_PAD = 128
MOD_ROWS = 16

_O_UF = 0
_O_UR = _O_UF + F_WIDTH
_O_UG = _O_UR + RNN_WIDTH
_O_GL = _O_UG + RNN_WIDTH

_VMEM_LIMIT = 56 * 2 ** 20


_INTERLEAVE = None


def _cparams(sem, vmem=_VMEM_LIMIT, flags=None):
    return pltpu.CompilerParams(dimension_semantics=sem, vmem_limit_bytes=vmem, flags=flags)


def _resident(shape):
    nd = len(shape)
    return pl.BlockSpec(shape, lambda *_: (0,) * nd, pipeline_mode=pl.Buffered(1))


def _dot(a, b):
    return jnp.dot(a, b, preferred_element_type=F32)


def _sigmoid(x):
    return 0.5 * (1.0 + jnp.tanh(0.5 * x))


def _gelu(x):
    return 0.5 * x * (1.0 + jnp.tanh(math.sqrt(2.0 / math.pi) * (x + 0.044715 * (x * x * x))))


def _rms(x, g):
    return x * lax.rsqrt(jnp.mean(x * x, axis=-1, keepdims=True) + EPS) * g


def _ada_kernel(cc_ref, w_ref, b_ref, o_ref):
    cc = cc_ref[...]
    s = cc * _sigmoid(cc)
    o_ref[...] = _dot(s.astype(BF16), w_ref[...].astype(BF16)) + b_ref[...]


def _ada_call(cc, w_ada, b_ada):
    depth, d, n = w_ada.shape
    tn = 1536
    return pl.pallas_call(
        _ada_kernel,
        grid=(depth, n // tn),
        in_specs=[
            pl.BlockSpec((MOD_ROWS, d), lambda l, j: (0, 0)),
            pl.BlockSpec((None, d, tn), lambda l, j: (l, 0, j)),
            pl.BlockSpec((None, 1, tn), lambda l, j: (l, 0, j)),
        ],
        out_specs=pl.BlockSpec((None, MOD_ROWS, tn), lambda l, j: (l, 0, j)),
        out_shape=jax.ShapeDtypeStruct((depth, MOD_ROWS, n), F32),
        compiler_params=_cparams(("arbitrary", "arbitrary")),
        name="adaln",
    )(cc, w_ada, b_ada.reshape(depth, 1, n))


def _in_kernel(x_ref, g_ref, shift_ref, scale_ref, wf_ref, wkr_ref, w_ref, qn_ref, kvn_ref, wuq_ref, wukv_ref,
               cq_ref, sq_ref, ck_ref, sk_ref, vone_ref,
               q_out, k_out, v_out, uf_out, ur_out, gug_out, g_out):
    h = _rms(x_ref[...], g_ref[...]) * (1.0 + scale_ref[...]) + shift_ref[...]
    hb = h.astype(BF16)
    hp = N_HEADS * HEAD_PAD
    first_half = lax.broadcasted_iota(jnp.int32, (x_ref.shape[0], HEAD_PAD), 1) < QK_NOPE + QK_ROPE // 2

    def swap_rot(y):
        fwd = pltpu.roll(y, QK_ROPE // 2, axis=1)
        bwd = pltpu.roll(y, y.shape[1] - QK_ROPE // 2, axis=1)
        return [jnp.where(first_half, bwd[:, a:a + HEAD_PAD], fwd[:, a:a + HEAD_PAD])
                for a in range(0, y.shape[1], HEAD_PAD)]

    def gate_dot(j):
        return _dot(hb, w_ref[:, _O_GL + j * D_MODEL:_O_GL + (j + 1) * D_MODEL])

    def gate_out(j, z):
        g_out[:, j * D_MODEL:(j + 1) * D_MODEL] = _sigmoid(z).astype(BF16)

    cq = _dot(hb, wf_ref[:, 0:Q_LORA])
    ckv = _dot(hb, wf_ref[:, Q_LORA:Q_LORA + KV_LORA])
    kr = _dot(hb, wkr_ref[...])
    z0 = gate_dot(0)

    cqn = _rms(cq, qn_ref[...]).astype(BF16)
    ckvn = _rms(ckv, kvn_ref[...]).astype(BF16)
    y = _dot(cqn, wuq_ref[...])
    kv = _dot(ckvn, wukv_ref[...])
    gate_out(0, z0)
    z1 = gate_dot(1)

    y_sw = swap_rot(y)
    cq_t = cq_ref[...]
    sq_t = sq_ref[...]
    for hd in range(N_HEADS):
        a = hd * HEAD_PAD
        q_out[:, a:a + HEAD_PAD] = (y[:, a:a + HEAD_PAD] * cq_t + y_sw[hd] * sq_t).astype(BF16)
    z2 = gate_dot(2)

    k_rope = kr * ck_ref[...] + swap_rot(kr)[0] * sk_ref[...]
    for hd in range(N_HEADS):
        a = hd * HEAD_PAD
        k_out[:, a:a + HEAD_PAD] = (kv[:, a:a + HEAD_PAD] + k_rope).astype(BF16)
    v_out[...] = (kv[:, hp:] + vone_ref[...]).astype(BF16)
    uf = _dot(hb, w_ref[:, _O_UF:_O_UF + F_WIDTH])
    ur = _dot(hb, w_ref[:, _O_UR:_O_UR + RNN_WIDTH])
    gate_out(1, z1)
    ug = _dot(hb, w_ref[:, _O_UG:_O_UG + RNN_WIDTH])
    gate_out(2, z2)
    uf_out[...] = uf.astype(BF16)
    ur_out[...] = ur
    gug_out[...] = _gelu(ug).astype(BF16)


def _in_call(x2d, mod_l, norm_g, w_in_p, qn, kvn, wuq, wukv, tabs, seq_len, mod_row, tm):
    rows, d = x2d.shape
    tiles_per_seq = seq_len // tm
    hp = N_HEADS * HEAD_PAD

    def row_spec(width):
        return pl.BlockSpec((tm, width), lambda i: (i, 0))

    def mod_spec(chunk):
        return pl.BlockSpec((None, None, 1, d), lambda i: (mod_row(i), chunk, 0, 0))

    tab_spec = pl.BlockSpec((tm, HEAD_PAD), lambda i: (i % tiles_per_seq, 0))
    out_shapes = (
        jax.ShapeDtypeStruct((rows, hp), BF16),
        jax.ShapeDtypeStruct((rows, hp), BF16),
        jax.ShapeDtypeStruct((rows, hp), BF16),
        jax.ShapeDtypeStruct((rows, F_WIDTH), BF16),
        jax.ShapeDtypeStruct((rows, RNN_WIDTH), F32),
        jax.ShapeDtypeStruct((rows, RNN_WIDTH), BF16),
        jax.ShapeDtypeStruct((rows, 3 * d), BF16),
    )
    return pl.pallas_call(
        _in_kernel,
        grid=(rows // tm,),
        in_specs=[
            row_spec(d),
            _resident((1, d)),
            mod_spec(0),
            mod_spec(1),
            _resident(w_in_p[0].shape), _resident(w_in_p[1].shape), _resident(w_in_p[2].shape),
            _resident((1, Q_LORA)),
            _resident((1, KV_LORA)),
            _resident(wuq.shape),
            _resident(wukv.shape),
            tab_spec, tab_spec, tab_spec, tab_spec,
            _resident((1, hp)),
        ],
        out_specs=tuple(row_spec(s.shape[1]) for s in out_shapes),
        out_shape=out_shapes,
        compiler_params=_cparams(("arbitrary",)),
        name="in_proj",
    )(x2d, norm_g, mod_l, mod_l, *w_in_p, qn, kvn, wuq, wukv, *tabs, _value_ones())


_ATTN_SUB_ROWS = 256


def _attn_kernel(*refs, n_seg):
    q_ref = refs[0]
    k_refs = refs[1:1 + n_seg]
    v_refs = refs[1 + n_seg:1 + 2 * n_seg]
    o_ref = refs[1 + 2 * n_seg]
    nt = (((1,), (1,)), ((), ()))
    sub = _ATTN_SUB_ROWS
    low = lax.broadcasted_iota(jnp.int32, (sub, HEAD_PAD), 1) < V_DIM

    def scores(item):
        r0, hd = item
        a = hd * HEAD_PAD
        qh = q_ref[r0:r0 + sub, a:a + HEAD_PAD]
        return [lax.dot_general(qh, k[:, a:a + HEAD_PAD], nt, preferred_element_type=F32) for k in k_refs]

    items = [(r0, hd) for r0 in range(0, q_ref.shape[0], sub) for hd in range(N_HEADS)]
    outs = {}
    s_next = scores(items[0])
    for idx, (r0, hd) in enumerate(items):
        a = hd * HEAD_PAD
        s = s_next
        if idx + 1 < len(items):
            s_next = scores(items[idx + 1])
        m = functools.reduce(jnp.maximum, [jnp.max(si, axis=-1, keepdims=True) for si in s])
        p = [jnp.exp2((si - m).astype(BF16)) for si in s]
        o = functools.reduce(jnp.add, [_dot(pi, v[:, a:a + HEAD_PAD]) for pi, v in zip(p, v_refs)])
        one_lane = V_DIM if hd % 2 == 0 else 0
        outs[hd] = o * (1.0 / o[:, one_lane:one_lane + 1])
        if hd % 2 == 1:
            o_ref[r0:r0 + sub, (hd - 1) * V_DIM:(hd + 1) * V_DIM] = jnp.where(low, outs.pop(hd - 1), outs.pop(hd)).astype(BF16)


def _attn_call(q, segs, q_len, tq):
    rows, hp = q.shape
    tiles = q_len // tq
    n_seg = len(segs)
    k_specs = [pl.BlockSpec((ln, hp), lambda i: (i // tiles, 0)) for _, _, ln in segs]
    return pl.pallas_call(
        functools.partial(_attn_kernel, n_seg=n_seg),
        grid=(rows // tq,),
        in_specs=[pl.BlockSpec((tq, hp), lambda i: (i, 0))] + k_specs + k_specs,
        out_specs=pl.BlockSpec((tq, N_HEADS * V_DIM), lambda i: (i, 0)),
        out_shape=jax.ShapeDtypeStruct((rows, N_HEADS * V_DIM), BF16),
        compiler_params=_cparams(("arbitrary",)),
        name="attention",
    )(q, *[k for k, _, _ in segs], *[v for _, v, _ in segs])


def _fourier_kernel(u_ref, cs_ref, flip_ref, dft_ref, o_ref, ab_ref, mid_ref, *, seq_len, tmf):
    half = seq_len // 2
    m = pl.program_id(1)

    @pl.when(m == 0)
    def _():
        u_lo = u_ref[0:half, :].astype(F32)
        rev = _dot(flip_ref[...], u_ref[half:seq_len, :])
        rev = pltpu.roll(rev, 1, axis=0)
        row0 = lax.broadcasted_iota(jnp.int32, rev.shape, 0) == 0
        ue = jnp.where(row0, u_lo, u_lo + rev).astype(BF16)
        uo = jnp.where(row0, 0.0, u_lo - rev).astype(BF16)
        cs = cs_ref[...]
        ab_ref[0:half, :] = _dot(ue, cs[:, :F_WIDTH]).astype(BF16)
        ab_ref[half:seq_len, :] = _dot(uo, cs[:, F_WIDTH:]).astype(BF16)
        mid_ref[...] = _dot(u_ref[half:half + 2 * SUBLANES, :], cs[:, :F_WIDTH])[0:SUBLANES, :]

    k = m * tmf + lax.broadcasted_iota(jnp.int32, (tmf, F_WIDTH), 0)
    sign = (1 - 2 * (k & 1)).astype(F32) * seq_len ** -0.5
    o_ref[...] = (_dot(dft_ref[...], ab_ref[...]) + sign * mid_ref[0:1, :]).astype(BF16)


def _fourier_call(uf, cs, flip, dft, seq_len, tmf):
    rows = uf.shape[0]
    tiles = seq_len // tmf
    return pl.pallas_call(
        functools.partial(_fourier_kernel, seq_len=seq_len, tmf=tmf),
        grid=(rows // seq_len, tiles),
        in_specs=[
            pl.BlockSpec((seq_len, F_WIDTH), lambda b, m: (b, 0)),
            _resident(cs.shape),
            _resident(flip.shape),
            pl.BlockSpec((tmf, seq_len), lambda b, m: (m, 0)),
        ],
        out_specs=pl.BlockSpec((tmf, F_WIDTH), lambda b, m: (b * tiles + m, 0)),
        out_shape=jax.ShapeDtypeStruct((rows, F_WIDTH), BF16),
        scratch_shapes=[pltpu.VMEM((seq_len, F_WIDTH), BF16), pltpu.VMEM((SUBLANES, F_WIDTH), F32)],
        compiler_params=_cparams(("arbitrary", "arbitrary")),
        name="fourier",
    )(uf, cs, flip, dft)


_RNN_HALF = RNN_WIDTH // 2
_RNN_CHUNK = 256


_RNN_SEGS = SUBLANES
_RNN_SEG_PAD = SUBLANES


def _rnn_kernel(urc_ref, urx_ref, gugc_ref, gugx_ref, cw_ref, cb_ref, wg_ref, bg_ref, lam_ref,
                yc_ref, yx_ref, pad_ref, af_ref, bf_ref, ab_ref, bb_ref, *, ctx_len, seq_len):
    half = _RNN_HALF
    n_slab = half // LANES
    zeros8 = jnp.zeros((SUBLANES, half), F32)
    row8 = lax.broadcasted_iota(jnp.int32, (_RNN_SEGS, LANES), 0)
    dirs = ((af_ref, bf_ref), (ab_ref, bb_ref))

    def gates(ur_ref, n, j):
        lo = j * half
        seg = n // _RNN_SEGS
        pitch = seg + _RNN_SEG_PAD
        pad_ref[0:SUBLANES, :] = zeros8
        pad_ref[SUBLANES:SUBLANES + n, :] = ur_ref[:, lo:lo + half]
        pad_ref[SUBLANES + n:2 * SUBLANES + n, :] = zeros8
        cw = cw_ref[:, lo:lo + half]
        cb = cb_ref[:, lo:lo + half]
        lam = lam_ref[:, lo:lo + half]
        k = (-0.5 * LRU_C * math.log2(math.e)) * (jnp.maximum(-lam, 0.0) + jnp.log1p(jnp.exp(-jnp.abs(lam))))
        wg = wg_ref[j]
        bg = bg_ref[j]
        ext_rows = _RNN_CHUNK + 2 * SUBLANES

        def put(ref, c, val):
            for sl in range(n_slab):
                v = val[:, sl * LANES:(sl + 1) * LANES]
                if seg == _RNN_CHUNK:
                    ref[sl, pl.ds(pl.multiple_of(c * pitch, SUBLANES), _RNN_CHUNK), :] = v
                else:
                    for sg in range(_RNN_SEGS):
                        ref[sl, sg * pitch:sg * pitch + seg, :] = v[sg * seg:(sg + 1) * seg, :]

        def chunk(c, carry):
            r0 = pl.multiple_of(c * _RNN_CHUNK, _RNN_CHUNK)
            ext = pad_ref[pl.ds(r0, ext_rows), :]
            sl = slice(SUBLANES, SUBLANES + _RNN_CHUNK)
            uc = (cb + cw[0:1] * pltpu.roll(ext, 2, axis=0)[sl] + cw[1:2] * pltpu.roll(ext, 1, axis=0)[sl]
                  + cw[2:3] * ext[sl] + cw[3:4] * pltpu.roll(ext, ext_rows - 1, axis=0)[sl])
            g = _dot(uc.astype(BF16), wg) + bg
            uch = 0.5 * uc
            for d, (a_ref, b_ref) in enumerate(dirs):
                kd = k[d:d + 1]
                a = jnp.exp2(kd + kd * jnp.tanh(g[:, (2 * d) * half:(2 * d + 1) * half]))
                put(a_ref, c, a)
                om = 1.0 - a * a
                sq = om * lax.rsqrt(jnp.maximum(om, 1e-30))
                put(b_ref, c, (sq * uch) * (1.0 + jnp.tanh(g[:, (2 * d + 1) * half:(2 * d + 2) * half])))
            return carry

        assert seg == _RNN_CHUNK or n == _RNN_CHUNK
        lax.fori_loop(0, n // _RNN_CHUNK, chunk, 0)

    def scan(n, h0):
        seg = n // _RNN_SEGS
        pitch = seg + _RNN_SEG_PAD
        ones = jnp.ones((_RNN_SEGS, LANES), F32)
        zeros = jnp.zeros((_RNN_SEGS, LANES), F32)

        def step(i, carry):
            out = []
            for d, (a_ref, b_ref) in enumerate(dirs):
                pos = i if d == 0 else seg - 1 - i
                for sl in range(n_slab):
                    p, h = carry[d * n_slab + sl]
                    rows = pl.ds(pos, _RNN_SEGS, stride=pitch)
                    a = a_ref[sl, rows, :]
                    p = a * p
                    h = a * h + b_ref[sl, rows, :]
                    a_ref[sl, rows, :] = p
                    b_ref[sl, rows, :] = h
                    out.append((p, h))
            return tuple(out)

        ends = lax.fori_loop(0, seg, step, tuple((ones, zeros) for _ in range(2 * n_slab)), unroll=2)

        starts, finals = [], []
        for d in range(2):
            for sl in range(n_slab):
                p, h = ends[d * n_slab + sl]
                cur = h0[d * n_slab + sl]
                hin = zeros
                order = range(_RNN_SEGS) if d == 0 else range(_RNN_SEGS - 1, -1, -1)
                for sg in order:
                    hin = jnp.where(row8 == sg, jnp.broadcast_to(cur, hin.shape), hin)
                    cur = p[sg:sg + 1, :] * cur + h[sg:sg + 1, :]
                starts.append(hin)
                finals.append(cur)
        return starts, finals

    def emit(n, starts, gug_ref, y_ref, lo):
        seg = n // _RNN_SEGS
        pitch = seg + _RNN_SEG_PAD
        for sg in range(_RNN_SEGS):
            src = slice(sg * pitch, sg * pitch + seg)
            dst = slice(sg * seg, (sg + 1) * seg)
            for sl in range(n_slab):
                hsum = None
                for d, (a_ref, b_ref) in enumerate(dirs):
                    hin = starts[d * n_slab + sl][sg:sg + 1, :]
                    hd = a_ref[sl, src, :] * hin + b_ref[sl, src, :]
                    hsum = hd if hsum is None else hsum + hd
                lanes = slice(lo + sl * LANES, lo + (sl + 1) * LANES)
                y_ref[dst, lanes] = (hsum * gug_ref[dst, lanes].astype(F32)).astype(BF16)

    zero_state = [jnp.zeros((1, LANES), F32)] * (2 * n_slab)
    for j in range(RNN_WIDTH // half):
        lo = j * half
        gates(urc_ref, ctx_len, j)
        starts, finals = scan(ctx_len, zero_state)
        emit(ctx_len, starts, gugc_ref, yc_ref, lo)
        gates(urx_ref, seq_len, j)
        starts, _ = scan(seq_len, finals)
        emit(seq_len, starts, gugx_ref, yx_ref, lo)


def _rnn_call(ur_c, ur_x, gug_c, gug_x, cw, cb, wg, bg, lam, ctx_len, seq_len):
    batch = ur_x.shape[0] // seq_len
    w = RNN_WIDTH

    def seq_spec(n):
        return pl.BlockSpec((n, w), lambda b: (b, 0))

    return pl.pallas_call(
        functools.partial(_rnn_kernel, ctx_len=ctx_len, seq_len=seq_len),
        grid=(batch,),
        in_specs=[seq_spec(ctx_len), seq_spec(seq_len), seq_spec(ctx_len), seq_spec(seq_len),
                  _resident(cw.shape), _resident(cb.shape), _resident(wg.shape), _resident(bg.shape),
                  _resident(lam.shape)],
        out_specs=(seq_spec(ctx_len), seq_spec(seq_len)),
        out_shape=(jax.ShapeDtypeStruct((batch * ctx_len, w), BF16),
                   jax.ShapeDtypeStruct((batch * seq_len, w), BF16)),
        scratch_shapes=[pltpu.VMEM((seq_len + 2 * SUBLANES, _RNN_HALF), F32)]
        + [pltpu.VMEM((_RNN_HALF // LANES, seq_len + _RNN_SEGS * _RNN_SEG_PAD, LANES), F32)] * 4,
        compiler_params=_cparams(("arbitrary",)),
        name="rglru",
    )(ur_c, ur_x, gug_c, gug_x, cw, cb, wg, bg, lam)


_MERGE_SUB_ROWS = 256


def _merge_kernel(x_ref, att_ref, yf_ref, yr_ref, g_ref, gate_ref, woa_ref, wof_ref, wor_ref, wout_ref, o_ref):
    d = D_MODEL
    tm = x_ref.shape[0]
    sub = min(_MERGE_SUB_ROWS, tm)

    def branches(r0):
        rows = slice(r0, r0 + sub)
        return (_dot(att_ref[rows, :], woa_ref[...]), _dot(yf_ref[rows, :], wof_ref[...]),
                _dot(yr_ref[rows, :], wor_ref[...]))

    nxt = branches(0)
    for r0 in range(0, tm, sub):
        rows = slice(r0, r0 + sub)
        ya, yf, yr = nxt
        if r0 + sub < tm:
            nxt = branches(r0 + sub)
        m = (g_ref[rows, 0:d].astype(F32) * ya + g_ref[rows, d:2 * d].astype(F32) * yf
             + g_ref[rows, 2 * d:3 * d].astype(F32) * yr)
        o_ref[rows, :] = x_ref[rows, :] + gate_ref[...] * _dot(m.astype(BF16), wout_ref[...])


def _merge_call(x2d, att, yf, yr, g, mod_l, woa, wof, wor, wout, mod_row, tm):
    rows, d = x2d.shape

    def row_spec(width):
        return pl.BlockSpec((tm, width), lambda i: (i, 0))

    return pl.pallas_call(
        _merge_kernel,
        grid=(rows // tm,),
        in_specs=[row_spec(d), row_spec(att.shape[1]), row_spec(yf.shape[1]), row_spec(yr.shape[1]),
                  row_spec(3 * d),
                  pl.BlockSpec((None, None, 1, d), lambda i: (mod_row(i), 2, 0, 0)),
                  _resident(woa.shape), _resident(wof.shape), _resident(wor.shape), _resident(wout.shape)],
        out_specs=row_spec(d),
        out_shape=jax.ShapeDtypeStruct((rows, d), F32),
        compiler_params=_cparams(("arbitrary",)),
        name="merge",
    )(x2d, att, yf, yr, g, mod_l, woa, wof, wor, wout)


_FF_CHUNK = 256
_FF_ROW_BLOCKS = 2


def _ffn_kernel(x_ref, xp_ref, xn_ref, g_ref, shift_ref, scale_ref, gate_ref,
                wu_ref, cw_ref, cb_ref, wd_ref, fin_ref, o_ref, h_ref, u0_ref, u1_ref, a_ref, *, tm, seq_len, final_norm):
    i = pl.program_id(0)
    tiles_per_seq = max(1, seq_len // tm)
    ext = tm + 2 * SUBLANES
    n_chunks = D_FF // _FF_CHUNK
    fc = _FF_CHUNK

    def prep(x):
        return (_rms(x, g_ref[...]) * (1.0 + scale_ref[...]) + shift_ref[...]).astype(BF16)

    first = (i % tiles_per_seq) == 0
    last = (i % tiles_per_seq) == tiles_per_seq - 1
    def cols(c):
        return pl.ds(pl.multiple_of(c * fc, fc), fc)

    def up_proj_rows(t, u_ref, lo, hi):
        for j in range(2):
            u_ref[lo:hi, j * fc:(j + 1) * fc] = _dot(h_ref[lo:hi, :], wu_ref[:, cols(j * n_chunks + t)])

    sp = -(-(SUBLANES + tm // 2) // (2 * SUBLANES)) * (2 * SUBLANES)
    hp = prep(xp_ref[...])
    h_ref[0:SUBLANES, :] = jnp.where(first, jnp.zeros_like(hp), hp)
    h_ref[SUBLANES:sp, :] = prep(x_ref[0:sp - SUBLANES, :])
    up_proj_rows(0, u0_ref, 0, sp)
    hn = prep(xn_ref[...])
    h_ref[sp:SUBLANES + tm, :] = prep(x_ref[sp - SUBLANES:tm, :])
    h_ref[SUBLANES + tm:, :] = jnp.where(last, jnp.zeros_like(hn), hn)
    up_proj_rows(0, u0_ref, sp, ext)

    n_blk = _FF_ROW_BLOCKS
    hm = tm // n_blk
    hme = hm + 2 * SUBLANES
    slh = slice(SUBLANES, SUBLANES + hm)

    def conv(u, cw, cb, r):
        prev = pltpu.roll(u, 1, axis=0)[slh]
        nxt = pltpu.roll(u, hme - 1, axis=0)[slh]
        if seq_len < tm:
            pos = (r * hm + lax.broadcasted_iota(jnp.int32, prev.shape, 0)) % seq_len
            prev = jnp.where(pos == 0, 0.0, prev)
            nxt = jnp.where(pos == seq_len - 1, 0.0, nxt)
        return cb + cw[0:1] * prev + cw[1:2] * u[slh] + cw[2:3] * nxt

    def activate(c, u_ref, r):
        rows = slice(r * hm, r * hm + hme)
        up = conv(u_ref[rows, 0:fc], cw_ref[:, cols(c)], cb_ref[:, cols(c)], r)
        gt = conv(u_ref[rows, fc:2 * fc], cw_ref[:, cols(n_chunks + c)], cb_ref[:, cols(n_chunks + c)], r)
        a_ref[r * hm:(r + 1) * hm, cols(c)] = _gelu(gt.astype(BF16)) * up.astype(BF16)

    u_bufs = (u0_ref, u1_ref)

    def step(t, parity, do_up=True, do_act=True):
        for r in range(n_blk):
            if do_up and r % (n_blk // 2) == 0:
                j = r // (n_blk // 2)
                u_bufs[parity][:, j * fc:(j + 1) * fc] = _dot(h_ref[...], wu_ref[:, cols(j * n_chunks + t)])
            if do_act:
                activate(t - 1, u_bufs[1 - parity], r)

    assert n_chunks % 2 == 1

    def body(k, carry):
        t = 2 * k + 1
        step(t, 1)
        step(t + 1, 0)
        return carry

    lax.fori_loop(0, n_chunks // 2, body, 0)
    for r in range(n_blk):
        activate(n_chunks - 1, u0_ref, r)
        rows = slice(r * hm, (r + 1) * hm)
        y = x_ref[rows, :] + gate_ref[...] * _dot(a_ref[rows, :], wd_ref[...])
        if final_norm:
            y = _rms(y, fin_ref[...])
        o_ref[rows, :] = y


def _ffn_call(x2d, mod_l, norm_g, wu, cw, cb, wd, fin_g, seq_len, mod_row, tm, final_norm):
    rows, d = x2d.shape
    assert seq_len % tm == 0 or tm % seq_len == 0
    blk8 = tm // SUBLANES
    n_blk8 = rows // SUBLANES

    def mod_spec(chunk):
        return pl.BlockSpec((None, None, 1, d), lambda i: (mod_row(i), chunk, 0, 0))

    return pl.pallas_call(
        functools.partial(_ffn_kernel, tm=tm, seq_len=seq_len, final_norm=final_norm),
        grid=(rows // tm,),
        in_specs=[
            pl.BlockSpec((tm, d), lambda i: (i, 0)),
            pl.BlockSpec((SUBLANES, d), lambda i: (jnp.maximum(i * blk8 - 1, 0), 0)),
            pl.BlockSpec((SUBLANES, d), lambda i: (jnp.minimum((i + 1) * blk8, n_blk8 - 1), 0)),
            _resident((1, d)),
            mod_spec(3), mod_spec(4), mod_spec(5),
            _resident(wu.shape), _resident(cw.shape), _resident(cb.shape), _resident(wd.shape),
            _resident((1, d)),
        ],
        out_specs=pl.BlockSpec((tm, d), lambda i: (i, 0)),
        out_shape=jax.ShapeDtypeStruct((rows, d), F32),
        scratch_shapes=[pltpu.VMEM((tm + 2 * SUBLANES, d), BF16),
                        pltpu.VMEM((tm + 2 * SUBLANES, 2 * _FF_CHUNK), F32),
                        pltpu.VMEM((tm + 2 * SUBLANES, 2 * _FF_CHUNK), F32),
                        pltpu.VMEM((tm, D_FF), BF16)],
        compiler_params=_cparams(("arbitrary",), flags=_INTERLEAVE),
        name="conv_ffn",
    )(x2d, x2d, x2d, norm_g, mod_l, mod_l, mod_l, wu, cw, cb, wd, fin_g)


def _prep_ffn(w_up, cw, cb, w_down):
    return w_up.astype(BF16), cw, cb[None, :], w_down.astype(BF16)


def _rope_tables(seq_len, rope):
    scale = QK_DIM ** -0.5 * math.log2(math.e)
    if rope:
        n_rows = seq_len // GRID_W
        row = jnp.repeat(jnp.arange(n_rows, dtype=F32), GRID_W)
        col = jnp.tile(jnp.arange(GRID_W, dtype=F32), n_rows)
        n_freq = QK_ROPE // 4
        inv = ROPE_BASE ** (-jnp.arange(n_freq, dtype=F32) / n_freq)
        ang = jnp.concatenate([row[:, None] * inv, col[:, None] * inv], axis=-1)
        cos, sin = jnp.cos(ang), jnp.sin(ang)
    else:
        cos = jnp.ones((seq_len, QK_ROPE // 2), F32)
        sin = jnp.zeros((seq_len, QK_ROPE // 2), F32)
    ones = jnp.ones((seq_len, QK_NOPE), F32)
    zeros = jnp.zeros((seq_len, QK_NOPE), F32)
    tail = jnp.zeros((seq_len, HEAD_PAD - QK_DIM), F32)
    cos2 = jnp.concatenate([cos, cos], axis=-1)
    sin2 = jnp.concatenate([-sin, sin], axis=-1)
    cq = scale * jnp.concatenate([ones, cos2, tail], axis=-1)
    sq = scale * jnp.concatenate([zeros, sin2, tail], axis=-1)
    ck = jnp.concatenate([zeros, cos2, tail], axis=-1)
    sk = jnp.concatenate([zeros, sin2, tail], axis=-1)
    return cq, sq, ck, sk


def _value_ones():
    lane = np.arange(N_HEADS * HEAD_PAD) % HEAD_PAD
    head = np.arange(N_HEADS * HEAD_PAD) // HEAD_PAD
    return jnp.asarray(np.where(head % 2 == 0, lane == V_DIM, lane == 0)[None, :], F32)


def _prep_w_in(w):
    d = w.shape[0]
    n_front = Q_LORA + KV_LORA
    kr = w[:, n_front:n_front + QK_ROPE]
    z_lo = jnp.zeros((d, QK_NOPE), w.dtype)
    z_hi = jnp.zeros((d, HEAD_PAD - QK_DIM), w.dtype)
    w_kr = jnp.concatenate([z_lo, kr, z_hi], axis=-1)
    return w[:, :n_front].astype(BF16), w_kr.astype(BF16), w[:, n_front + QK_ROPE:].astype(BF16)


def _prep_w_uq(w):
    r, h, _ = w.shape
    pad = jnp.zeros((r, h, HEAD_PAD - QK_DIM), w.dtype)
    y1 = jnp.concatenate([w, pad], axis=-1)
    return y1.reshape(r, h * HEAD_PAD).astype(BF16)


def _prep_w_ukv(w):
    r, h, _ = w.shape
    k = jnp.concatenate([w[..., :QK_NOPE], jnp.zeros((r, h, HEAD_PAD - QK_NOPE), w.dtype)], axis=-1)
    v = w[..., QK_NOPE:]
    z = jnp.zeros_like(v)
    odd = (jnp.arange(h) % 2 == 1)[None, :, None]
    vp = jnp.concatenate([jnp.where(odd, z, v), jnp.where(odd, v, z)], axis=-1)
    return jnp.concatenate([k.reshape(r, h * HEAD_PAD), vp.reshape(r, h * HEAD_PAD)], axis=-1).astype(BF16)


def _prep_gates(w_a, b_a, w_x, b_x):
    half = _RNN_HALF
    heads_per_half = half // RNN_BLOCK

    def blockdiag(w):
        eye = jnp.eye(heads_per_half, dtype=w.dtype)
        return jnp.einsum('hij,hk->hikj', w, eye).reshape(half, half)

    ws, bs = [], []
    for j in range(RNN_WIDTH // half):
        hs = slice(j * heads_per_half, (j + 1) * heads_per_half)
        ls = slice(j * half, (j + 1) * half)
        ws.append(jnp.concatenate([blockdiag(w_a[0, hs]), blockdiag(w_x[0, hs]),
                                   blockdiag(w_a[1, hs]), blockdiag(w_x[1, hs])], axis=-1))
        bs.append(jnp.concatenate([b_a[0, ls], b_x[0, ls], b_a[1, ls], b_x[1, ls]], axis=-1)[None, :])
    return (0.5 * jnp.stack(ws)).astype(BF16), 0.5 * jnp.stack(bs)


def _dft_tables(seq_len):
    lo_n = 32 if seq_len % 32 == 0 else 1
    half = seq_len // 2
    t = jnp.arange(half, dtype=jnp.int32)

    def table(rows):
        ang = (2.0 * math.pi / seq_len) * ((rows[:, None] * t[None, :]) % seq_len).astype(F32)
        return jnp.cos(ang), jnp.sin(ang)

    c_hi, s_hi = table(jnp.arange(seq_len // lo_n, dtype=jnp.int32) * lo_n)
    c_lo, s_lo = table(jnp.arange(lo_n, dtype=jnp.int32))
    s = seq_len ** -0.5
    cos = (c_hi[:, None, :] * c_lo[None, :, :] - s_hi[:, None, :] * s_lo[None, :, :]).reshape(seq_len, half)
    sin = (s_hi[:, None, :] * c_lo[None, :, :] + c_hi[:, None, :] * s_lo[None, :, :]).reshape(seq_len, half)
    flip = jnp.flip(jnp.eye(half, dtype=BF16), axis=0)
    return jnp.concatenate([cos * s, sin * s], axis=-1).astype(BF16), flip


def _channel_dft():
    k = jnp.arange(F_GROUP_W, dtype=jnp.int32)
    ang = (2.0 * math.pi / F_GROUP_W) * ((k[:, None] * k[None, :]) % F_GROUP_W).astype(F32)
    s = F_GROUP_W ** -0.5
    eye = jnp.eye(F_GROUPS, dtype=F32)
    c = jnp.kron(eye, jnp.cos(ang) * s)
    sn = jnp.kron(eye, jnp.sin(ang) * s)
    return jnp.concatenate([c, -sn], axis=-1).astype(BF16)


def kernel(x, c, ctx, c_ctx, w_ada, b_ada, norm_mix, norm_ffn, w_in, q_norm, kv_norm, w_uq, w_ukv, w_o_attn,
           w_o_fourier, rnn_conv_w, rnn_conv_b, rg_w_a, rg_b_a, rg_w_x, rg_b_x, rg_lambda, w_o_rnn, w_out,
           w_up, ffn_conv_w, ffn_conv_b, w_down, final_norm):
    batch, seq_len, d = x.shape
    ctx_len = ctx.shape[1]
    depth = w_ada.shape[0]
    assert batch + 1 <= MOD_ROWS and d == D_MODEL

    cc = jnp.concatenate([c, c_ctx[None, :], jnp.zeros((MOD_ROWS - batch - 1, d), F32)], axis=0)
    mod = _ada_call(cc, w_ada, b_ada).reshape(depth, MOD_ROWS, N_MOD, 1, d)

    tabs_x = _rope_tables(seq_len, True)
    tabs_c = _rope_tables(ctx_len, False)
    dft_x, flip_x = _dft_tables(seq_len)
    dft_c, flip_c = _dft_tables(ctx_len)
    cs = _channel_dft()

    tm_in = min(512, seq_len)
    tm_x = min(512, seq_len)
    tm_c = min(256, ctx_len)
    tq = min(512, seq_len)
    tm_ffn = min(1024, seq_len)
    tm_ffn_c = min(1024, batch * ctx_len)

    def x_row(tm):
        return lambda i: i // (seq_len // tm)

    c_row = lambda i: batch

    xs = x.reshape(batch * seq_len, d)
    cs_tok = ctx.reshape(batch * ctx_len, d)

    for l in range(depth):
        last = l == depth - 1
        mod_l = mod[l]
        w_in_p = _prep_w_in(w_in[l])
        wuq = _prep_w_uq(w_uq[l])
        wukv = _prep_w_ukv(w_ukv[l])
        wg, bg = _prep_gates(rg_w_a[l], rg_b_a[l], rg_w_x[l], rg_b_x[l])
        nm = norm_mix[l][None, :]
        nf = norm_ffn[l][None, :]
        qn = q_norm[l][None, :]
        kvn = kv_norm[l][None, :]
        woa = w_o_attn[l].astype(BF16)
        wof = w_o_fourier[l].astype(BF16)
        wor = w_o_rnn[l].astype(BF16)
        wout = w_out[l].astype(BF16)
        wup, fcw, fcb, wdn = _prep_ffn(w_up[l], ffn_conv_w[l], ffn_conv_b[l], w_down[l])
        rcw = rnn_conv_w[l]
        rcb = rnn_conv_b[l][None, :]
        fin = final_norm[None, :]

        q_x, k_x, v_x, uf_x, ur_x, gug_x, g_x = _in_call(xs, mod_l, nm, w_in_p, qn, kvn, wuq, wukv, tabs_x,
                                                          seq_len, x_row(tm_in), tm_in)
        q_c, k_c, v_c, uf_c, ur_c, gug_c, g_c = _in_call(cs_tok, mod_l, nm, w_in_p, qn, kvn, wuq, wukv, tabs_c,
                                                          ctx_len, c_row, tm_c)

        att_x = _attn_call(q_x, [(k_c, v_c, ctx_len), (k_x, v_x, seq_len)], seq_len, tq)
        yf_x = _fourier_call(uf_x, cs, flip_x, dft_x, seq_len, min(512, seq_len))
        yr_c, yr_x = _rnn_call(ur_c, ur_x, gug_c, gug_x, rcw, rcb, wg, bg, rg_lambda[l], ctx_len, seq_len)

        xs = _merge_call(xs, att_x, yf_x, yr_x, g_x, mod_l, woa, wof, wor, wout, x_row(tm_x), tm_x)
        xs = _ffn_call(xs, mod_l, nf, wup, fcw, fcb, wdn, fin, seq_len, x_row(tm_ffn), tm_ffn, last)

        if not last:
            att_c = _attn_call(q_c, [(k_c, v_c, ctx_len)], ctx_len, min(_ATTN_SUB_ROWS, ctx_len))
            yf_c = _fourier_call(uf_c, cs, flip_c, dft_c, ctx_len, ctx_len)
            cs_tok = _merge_call(cs_tok, att_c, yf_c, yr_c, g_c, mod_l, woa, wof, wor, wout, c_row, tm_c)
            cs_tok = _ffn_call(cs_tok, mod_l, nf, wup, fcw, fcb, wdn, fin, ctx_len, c_row, tm_ffn_c, False)

    return xs.reshape(batch, seq_len, d)
```

```python
import functools
import math
from typing import NamedTuple

import jax
import jax.numpy as jnp
import numpy as np
from jax import lax
from jax.experimental import pallas as pl
from jax.experimental.pallas import tpu as pltpu

F32 = jnp.float32
BF16 = jnp.bfloat16

D_MODEL = 1024
GRID_W = 64
N_HEADS = 8
Q_LORA = 384
KV_LORA = 256
QK_NOPE = 64
QK_ROPE = 32
V_DIM = 64
QK_DIM = QK_NOPE + QK_ROPE
ROPE_BASE = 10000.0
F_GROUPS = 4
F_GROUP_W = 128
F_WIDTH = F_GROUPS * F_GROUP_W
RNN_WIDTH = 512
RNN_HEADS = 8
RNN_BLOCK = RNN_WIDTH // RNN_HEADS
RNN_CONV = 4
LRU_C = 8.0
D_FF = 2816
FFN_CONV = 3
N_MOD = 6
EPS = 1e-6

LANES = 128
SUBLANES = 8
HEAD_PAD = 128
MOD_ROWS = 16

_O_UF = 0
_O_UR = _O_UF + F_WIDTH
_O_UG = _O_UR + RNN_WIDTH
_O_GL = _O_UG + RNN_WIDTH

_VMEM_LIMIT = 56 * 2 ** 20


_INTERLEAVE = None


def _cparams(sem, vmem=_VMEM_LIMIT, flags=None):
    return pltpu.CompilerParams(dimension_semantics=sem, vmem_limit_bytes=vmem, flags=flags)


def _resident(shape):
    nd = len(shape)
    return pl.BlockSpec(shape, lambda *_: (0,) * nd, pipeline_mode=pl.Buffered(1))


class _Layered(NamedTuple):
    array: jax.Array
    layer: int


def _spec_of(w):
    if isinstance(w, _Layered):
        shape = w.array.shape[1:]
        index = (w.layer,) + (0,) * len(shape)
        return pl.BlockSpec((None,) + shape, lambda *_: index, pipeline_mode=pl.Buffered(1))
    return _resident(w.shape)


def _arr_of(w):
    return w.array if isinstance(w, _Layered) else w


def _dot(a, b):
    return jnp.dot(a, b, preferred_element_type=F32)


def _sigmoid(x):
    return 0.5 * (1.0 + jnp.tanh(0.5 * x))


def _gelu(x):
    return 0.5 * x * (1.0 + jnp.tanh(math.sqrt(2.0 / math.pi) * (x + 0.044715 * (x * x * x))))


def _rms(x, g):
    return x * lax.rsqrt(jnp.mean(x * x, axis=-1, keepdims=True) + EPS) * g


def _ada_kernel(cc_ref, w_ref, b_ref, o_ref):
    cc = cc_ref[...]
    s = cc * _sigmoid(cc)
    o_ref[...] = _dot(s.astype(BF16), w_ref[...].astype(BF16)) + b_ref[...]


def _ada_call(cc, w_ada, b_ada):
    depth, d, n = w_ada.shape
    tn = 1536
    return pl.pallas_call(
        _ada_kernel,
        grid=(depth, n // tn),
        in_specs=[
            pl.BlockSpec((MOD_ROWS, d), lambda l, j: (0, 0)),
            pl.BlockSpec((None, d, tn), lambda l, j: (l, 0, j)),
            pl.BlockSpec((None, 1, tn), lambda l, j: (l, 0, j)),
        ],
        out_specs=pl.BlockSpec((None, MOD_ROWS, tn), lambda l, j: (l, 0, j)),
        out_shape=jax.ShapeDtypeStruct((depth, MOD_ROWS, n), F32),
        compiler_params=_cparams(("arbitrary", "arbitrary")),
        name="adaln",
    )(cc, w_ada, b_ada.reshape(depth, 1, n))


def _in_kernel(x_ref, g_ref, shift_ref, scale_ref, wf_ref, wkr_ref, w_ref, qn_ref, kvn_ref, wuq_ref, wukv_ref,
               cq_ref, sq_ref, ck_ref, sk_ref, vone_ref,
               q_out, k_out, v_out, uf_out, ur_out, gug_out, g_out):
    h = _rms(x_ref[...], g_ref[...]) * (1.0 + scale_ref[...]) + shift_ref[...]
    hb = h.astype(BF16)
    hp = N_HEADS * HEAD_PAD
    first_half = lax.broadcasted_iota(jnp.int32, (x_ref.shape[0], HEAD_PAD), 1) < QK_NOPE + QK_ROPE // 2

    def swap_rot(y):
        fwd = pltpu.roll(y, QK_ROPE // 2, axis=1)
        bwd = pltpu.roll(y, y.shape[1] - QK_ROPE // 2, axis=1)
        return [jnp.where(first_half, bwd[:, a:a + HEAD_PAD], fwd[:, a:a + HEAD_PAD])
                for a in range(0, y.shape[1], HEAD_PAD)]

    def gate_dot(j):
        return _dot(hb, w_ref[:, _O_GL + j * D_MODEL:_O_GL + (j + 1) * D_MODEL])

    def gate_out(j, z):
        g_out[:, j * D_MODEL:(j + 1) * D_MODEL] = _sigmoid(z).astype(BF16)

    cq = _dot(hb, wf_ref[:, 0:Q_LORA])
    ckv = _dot(hb, wf_ref[:, Q_LORA:Q_LORA + KV_LORA])
    kr = _dot(hb, wkr_ref[...])
    z0 = gate_dot(0)

    cqn = _rms(cq, qn_ref[...]).astype(BF16)
    ckvn = _rms(ckv, kvn_ref[...]).astype(BF16)
    y = _dot(cqn, wuq_ref[...])
    kv = _dot(ckvn, wukv_ref[...])
    gate_out(0, z0)
    z1 = gate_dot(1)

    y_sw = swap_rot(y)
    cq_t = cq_ref[...]
    sq_t = sq_ref[...]
    for hd in range(N_HEADS):
        a = hd * HEAD_PAD
        q_out[:, a:a + HEAD_PAD] = (y[:, a:a + HEAD_PAD] * cq_t + y_sw[hd] * sq_t).astype(BF16)
    z2 = gate_dot(2)

    k_rope = kr * ck_ref[...] + swap_rot(kr)[0] * sk_ref[...]
    for hd in range(N_HEADS):
        a = hd * HEAD_PAD
        k_out[:, a:a + HEAD_PAD] = (kv[:, a:a + HEAD_PAD] + k_rope).astype(BF16)
    v_out[...] = (kv[:, hp:] + vone_ref[...]).astype(BF16)
    uf = _dot(hb, w_ref[:, _O_UF:_O_UF + F_WIDTH])
    ur = _dot(hb, w_ref[:, _O_UR:_O_UR + RNN_WIDTH])
    gate_out(1, z1)
    ug = _dot(hb, w_ref[:, _O_UG:_O_UG + RNN_WIDTH])
    gate_out(2, z2)
    uf_out[...] = uf.astype(BF16)
    ur_out[...] = ur
    gug_out[...] = _gelu(ug).astype(BF16)


def _in_call(x2d, mod_l, norm_g, w_in_p, qn, kvn, wuq, wukv, tabs, seq_len, mod_row, tm):
    rows, d = x2d.shape
    tiles_per_seq = seq_len // tm
    hp = N_HEADS * HEAD_PAD

    def row_spec(width):
        return pl.BlockSpec((tm, width), lambda i: (i, 0))

    def mod_spec(chunk):
        return pl.BlockSpec((None, None, 1, d), lambda i: (mod_row(i), chunk, 0, 0))

    tab_spec = pl.BlockSpec((tm, HEAD_PAD), lambda i: (i % tiles_per_seq, 0))
    out_shapes = (
        jax.ShapeDtypeStruct((rows, hp), BF16),
        jax.ShapeDtypeStruct((rows, hp), BF16),
        jax.ShapeDtypeStruct((rows, hp), BF16),
        jax.ShapeDtypeStruct((rows, F_WIDTH), BF16),
        jax.ShapeDtypeStruct((rows, RNN_WIDTH), F32),
        jax.ShapeDtypeStruct((rows, RNN_WIDTH), BF16),
        jax.ShapeDtypeStruct((rows, 3 * d), BF16),
    )
    return pl.pallas_call(
        _in_kernel,
        grid=(rows // tm,),
        in_specs=[
            row_spec(d),
            _spec_of(norm_g),
            mod_spec(0),
            mod_spec(1),
            _spec_of(w_in_p[0]), _spec_of(w_in_p[1]), _spec_of(w_in_p[2]),
            _spec_of(qn),
            _spec_of(kvn),
            _spec_of(wuq),
            _spec_of(wukv),
            tab_spec, tab_spec, tab_spec, tab_spec,
            _resident((1, hp)),
        ],
        out_specs=tuple(row_spec(s.shape[1]) for s in out_shapes),
        out_shape=out_shapes,
        compiler_params=_cparams(("arbitrary",)),
        name="in_proj",
    )(x2d, _arr_of(norm_g), mod_l, mod_l, *map(_arr_of, w_in_p), _arr_of(qn), _arr_of(kvn), _arr_of(wuq), _arr_of(wukv),
      *tabs, _value_ones())


_ATTN_SUB_ROWS = 256


def _attn_kernel(*refs, n_seg):
    q_ref = refs[0]
    k_refs = refs[1:1 + n_seg]
    v_refs = refs[1 + n_seg:1 + 2 * n_seg]
    o_ref = refs[1 + 2 * n_seg]
    nt = (((1,), (1,)), ((), ()))
    sub = _ATTN_SUB_ROWS
    low = lax.broadcasted_iota(jnp.int32, (sub, HEAD_PAD), 1) < V_DIM

    def scores(item):
        r0, hd = item
        a = hd * HEAD_PAD
        qh = q_ref[r0:r0 + sub, a:a + HEAD_PAD]
        return [lax.dot_general(qh, k[:, a:a + HEAD_PAD], nt, preferred_element_type=F32) for k in k_refs]

    items = [(r0, hd) for r0 in range(0, q_ref.shape[0], sub) for hd in range(N_HEADS)]
    outs = {}
    s_next = scores(items[0])
    for idx, (r0, hd) in enumerate(items):
        a = hd * HEAD_PAD
        s = s_next
        if idx + 1 < len(items):
            s_next = scores(items[idx + 1])
        m = functools.reduce(jnp.maximum, [jnp.max(si, axis=-1, keepdims=True) for si in s])
        p = [jnp.exp2((si - m).astype(BF16)) for si in s]
        o = functools.reduce(jnp.add, [_dot(pi, v[:, a:a + HEAD_PAD]) for pi, v in zip(p, v_refs)])
        one_lane = V_DIM if hd % 2 == 0 else 0
        outs[hd] = o * (1.0 / o[:, one_lane:one_lane + 1])
        if hd % 2 == 1:
            o_ref[r0:r0 + sub, (hd - 1) * V_DIM:(hd + 1) * V_DIM] = jnp.where(low, outs.pop(hd - 1), outs.pop(hd)).astype(BF16)


def _attn_call(q, segs, q_len, tq):
    rows, hp = q.shape
    tiles = q_len // tq
    n_seg = len(segs)
    k_specs = [pl.BlockSpec((ln, hp), lambda i: (i // tiles, 0)) for _, _, ln in segs]
    return pl.pallas_call(
        functools.partial(_attn_kernel, n_seg=n_seg),
        grid=(rows // tq,),
        in_specs=[pl.BlockSpec((tq, hp), lambda i: (i, 0))] + k_specs + k_specs,
        out_specs=pl.BlockSpec((tq, N_HEADS * V_DIM), lambda i: (i, 0)),
        out_shape=jax.ShapeDtypeStruct((rows, N_HEADS * V_DIM), BF16),
        compiler_params=_cparams(("arbitrary",)),
        name="attention",
    )(q, *[k for k, _, _ in segs], *[v for _, v, _ in segs])


def _fourier_kernel(u_ref, cs_ref, flip_ref, dft_ref, o_ref, ab_ref, mid_ref, *, seq_len, tmf):
    half = seq_len // 2
    m = pl.program_id(1)

    @pl.when(m == 0)
    def _():
        u_lo = u_ref[0:half, :].astype(F32)
        rev = _dot(flip_ref[...], u_ref[half:seq_len, :])
        rev = pltpu.roll(rev, 1, axis=0)
        row0 = lax.broadcasted_iota(jnp.int32, rev.shape, 0) == 0
        ue = jnp.where(row0, u_lo, u_lo + rev).astype(BF16)
        uo = jnp.where(row0, 0.0, u_lo - rev).astype(BF16)
        cs = cs_ref[...]
        ab_ref[0:half, :] = _dot(ue, cs[:, :F_WIDTH]).astype(BF16)
        ab_ref[half:seq_len, :] = _dot(uo, cs[:, F_WIDTH:]).astype(BF16)
        mid_ref[...] = _dot(u_ref[half:half + 2 * SUBLANES, :], cs[:, :F_WIDTH])[0:SUBLANES, :]

    k = m * tmf + lax.broadcasted_iota(jnp.int32, (tmf, F_WIDTH), 0)
    sign = (1 - 2 * (k & 1)).astype(F32) * seq_len ** -0.5
    o_ref[...] = (_dot(dft_ref[...], ab_ref[...]) + sign * mid_ref[0:1, :]).astype(BF16)


def _fourier_call(uf, cs, flip, dft, seq_len, tmf):
    rows = uf.shape[0]
    tiles = seq_len // tmf
    return pl.pallas_call(
        functools.partial(_fourier_kernel, seq_len=seq_len, tmf=tmf),
        grid=(rows // seq_len, tiles),
        in_specs=[
            pl.BlockSpec((seq_len, F_WIDTH), lambda b, m: (b, 0)),
            _resident(cs.shape),
            _resident(flip.shape),
            pl.BlockSpec((tmf, seq_len), lambda b, m: (m, 0)),
        ],
        out_specs=pl.BlockSpec((tmf, F_WIDTH), lambda b, m: (b * tiles + m, 0)),
        out_shape=jax.ShapeDtypeStruct((rows, F_WIDTH), BF16),
        scratch_shapes=[pltpu.VMEM((seq_len, F_WIDTH), BF16), pltpu.VMEM((SUBLANES, F_WIDTH), F32)],
        compiler_params=_cparams(("arbitrary", "arbitrary")),
        name="fourier",
    )(uf, cs, flip, dft)


_RNN_HALF = RNN_WIDTH // 2
_RNN_CHUNK = 256


_RNN_SEGS = SUBLANES
_RNN_SEG_PAD = SUBLANES


def _rnn_kernel(urc_ref, urx_ref, gugc_ref, gugx_ref, cw_ref, cb_ref, wg_ref, bg_ref, lam_ref,
                yc_ref, yx_ref, pad_ref, af_ref, bf_ref, ab_ref, bb_ref, *, ctx_len, seq_len):
    half = _RNN_HALF
    n_slab = half // LANES
    zeros8 = jnp.zeros((SUBLANES, half), F32)
    row8 = lax.broadcasted_iota(jnp.int32, (_RNN_SEGS, LANES), 0)
    dirs = ((af_ref, bf_ref), (ab_ref, bb_ref))

    def gates(ur_ref, n, j):
        lo = j * half
        seg = n // _RNN_SEGS
        pitch = seg + _RNN_SEG_PAD
        pad_ref[0:SUBLANES, :] = zeros8
        pad_ref[SUBLANES:SUBLANES + n, :] = ur_ref[:, lo:lo + half]
        pad_ref[SUBLANES + n:2 * SUBLANES + n, :] = zeros8
        cw = cw_ref[:, lo:lo + half]
        cb = cb_ref[:, lo:lo + half]
        lam = lam_ref[:, lo:lo + half]
        k = (-0.5 * LRU_C * math.log2(math.e)) * (jnp.maximum(-lam, 0.0) + jnp.log1p(jnp.exp(-jnp.abs(lam))))
        wg = wg_ref[j]
        bg = bg_ref[j]
        ext_rows = _RNN_CHUNK + 2 * SUBLANES

        def put(ref, c, val):
            for sl in range(n_slab):
                v = val[:, sl * LANES:(sl + 1) * LANES]
                if seg == _RNN_CHUNK:
                    ref[sl, pl.ds(pl.multiple_of(c * pitch, SUBLANES), _RNN_CHUNK), :] = v
                else:
                    for sg in range(_RNN_SEGS):
                        ref[sl, sg * pitch:sg * pitch + seg, :] = v[sg * seg:(sg + 1) * seg, :]

        def chunk(c, carry):
            r0 = pl.multiple_of(c * _RNN_CHUNK, _RNN_CHUNK)
            ext = pad_ref[pl.ds(r0, ext_rows), :]
            sl = slice(SUBLANES, SUBLANES + _RNN_CHUNK)
            uc = (cb + cw[0:1] * pltpu.roll(ext, 2, axis=0)[sl] + cw[1:2] * pltpu.roll(ext, 1, axis=0)[sl]
                  + cw[2:3] * ext[sl] + cw[3:4] * pltpu.roll(ext, ext_rows - 1, axis=0)[sl])
            g = _dot(uc.astype(BF16), wg) + bg
            uch = 0.5 * uc
            for d, (a_ref, b_ref) in enumerate(dirs):
                kd = k[d:d + 1]
                a = jnp.exp2(kd + kd * jnp.tanh(g[:, (2 * d) * half:(2 * d + 1) * half]))
                put(a_ref, c, a)
                om = 1.0 - a * a
                sq = om * lax.rsqrt(jnp.maximum(om, 1e-30))
                put(b_ref, c, (sq * uch) * (1.0 + jnp.tanh(g[:, (2 * d + 1) * half:(2 * d + 2) * half])))
            return carry

        assert seg == _RNN_CHUNK or n == _RNN_CHUNK
        lax.fori_loop(0, n // _RNN_CHUNK, chunk, 0)

    def scan(n, h0):
        seg = n // _RNN_SEGS
        pitch = seg + _RNN_SEG_PAD
        ones = jnp.ones((_RNN_SEGS, LANES), F32)
        zeros = jnp.zeros((_RNN_SEGS, LANES), F32)

        def step(i, carry):
            out = []
            for d, (a_ref, b_ref) in enumerate(dirs):
                pos = i if d == 0 else seg - 1 - i
                for sl in range(n_slab):
                    p, h = carry[d * n_slab + sl]
                    rows = pl.ds(pos, _RNN_SEGS, stride=pitch)
                    a = a_ref[sl, rows, :]
                    p = a * p
                    h = a * h + b_ref[sl, rows, :]
                    a_ref[sl, rows, :] = p
                    b_ref[sl, rows, :] = h
                    out.append((p, h))
            return tuple(out)

        ends = lax.fori_loop(0, seg, step, tuple((ones, zeros) for _ in range(2 * n_slab)), unroll=2)

        starts, finals = [], []
        for d in range(2):
            for sl in range(n_slab):
                p, h = ends[d * n_slab + sl]
                cur = h0[d * n_slab + sl]
                hin = zeros
                order = range(_RNN_SEGS) if d == 0 else range(_RNN_SEGS - 1, -1, -1)
                for sg in order:
                    hin = jnp.where(row8 == sg, jnp.broadcast_to(cur, hin.shape), hin)
                    cur = p[sg:sg + 1, :] * cur + h[sg:sg + 1, :]
                starts.append(hin)
                finals.append(cur)
        return starts, finals

    def emit(n, starts, gug_ref, y_ref, lo):
        seg = n // _RNN_SEGS
        pitch = seg + _RNN_SEG_PAD
        for sg in range(_RNN_SEGS):
            src = slice(sg * pitch, sg * pitch + seg)
            dst = slice(sg * seg, (sg + 1) * seg)
            for sl in range(n_slab):
                hsum = None
                for d, (a_ref, b_ref) in enumerate(dirs):
                    hin = starts[d * n_slab + sl][sg:sg + 1, :]
                    hd = a_ref[sl, src, :] * hin + b_ref[sl, src, :]
                    hsum = hd if hsum is None else hsum + hd
                lanes = slice(lo + sl * LANES, lo + (sl + 1) * LANES)
                y_ref[dst, lanes] = (hsum * gug_ref[dst, lanes].astype(F32)).astype(BF16)

    zero_state = [jnp.zeros((1, LANES), F32)] * (2 * n_slab)
    for j in range(RNN_WIDTH // half):
        lo = j * half
        gates(urc_ref, ctx_len, j)
        starts, finals = scan(ctx_len, zero_state)
        emit(ctx_len, starts, gugc_ref, yc_ref, lo)
        gates(urx_ref, seq_len, j)
        starts, _ = scan(seq_len, finals)
        emit(seq_len, starts, gugx_ref, yx_ref, lo)


def _rnn_call(ur_c, ur_x, gug_c, gug_x, cw, cb, wg, bg, lam, ctx_len, seq_len):
    batch = ur_x.shape[0] // seq_len
    w = RNN_WIDTH

    def seq_spec(n):
        return pl.BlockSpec((n, w), lambda b: (b, 0))

    return pl.pallas_call(
        functools.partial(_rnn_kernel, ctx_len=ctx_len, seq_len=seq_len),
        grid=(batch,),
        in_specs=[seq_spec(ctx_len), seq_spec(seq_len), seq_spec(ctx_len), seq_spec(seq_len),
                  _spec_of(cw), _spec_of(cb), _spec_of(wg), _spec_of(bg), _spec_of(lam)],
        out_specs=(seq_spec(ctx_len), seq_spec(seq_len)),
        out_shape=(jax.ShapeDtypeStruct((batch * ctx_len, w), BF16),
                   jax.ShapeDtypeStruct((batch * seq_len, w), BF16)),
        scratch_shapes=[pltpu.VMEM((seq_len + 2 * SUBLANES, _RNN_HALF), F32)]
        + [pltpu.VMEM((_RNN_HALF // LANES, seq_len + _RNN_SEGS * _RNN_SEG_PAD, LANES), F32)] * 4,
        compiler_params=_cparams(("arbitrary",)),
        name="rglru",
    )(ur_c, ur_x, gug_c, gug_x, *map(_arr_of, (cw, cb, wg, bg, lam)))


_MERGE_SUB_ROWS = 256


def _merge_kernel(x_ref, att_ref, yf_ref, yr_ref, g_ref, gate_ref, woa_ref, wof_ref, wor_ref, wout_ref, o_ref):
    d = D_MODEL
    tm = x_ref.shape[0]
    sub = min(_MERGE_SUB_ROWS, tm)

    def branches(r0):
        rows = slice(r0, r0 + sub)
        return (_dot(att_ref[rows, :], woa_ref[...]), _dot(yf_ref[rows, :], wof_ref[...]),
                _dot(yr_ref[rows, :], wor_ref[...]))

    nxt = branches(0)
    for r0 in range(0, tm, sub):
        rows = slice(r0, r0 + sub)
        ya, yf, yr = nxt
        if r0 + sub < tm:
            nxt = branches(r0 + sub)
        m = (g_ref[rows, 0:d].astype(F32) * ya + g_ref[rows, d:2 * d].astype(F32) * yf
             + g_ref[rows, 2 * d:3 * d].astype(F32) * yr)
        o_ref[rows, :] = x_ref[rows, :] + gate_ref[...] * _dot(m.astype(BF16), wout_ref[...])


def _merge_call(x2d, att, yf, yr, g, mod_l, woa, wof, wor, wout, mod_row, tm):
    rows, d = x2d.shape

    def row_spec(width):
        return pl.BlockSpec((tm, width), lambda i: (i, 0))

    return pl.pallas_call(
        _merge_kernel,
        grid=(rows // tm,),
        in_specs=[row_spec(d), row_spec(att.shape[1]), row_spec(yf.shape[1]), row_spec(yr.shape[1]),
                  row_spec(3 * d),
                  pl.BlockSpec((None, None, 1, d), lambda i: (mod_row(i), 2, 0, 0)),
                  _spec_of(woa), _spec_of(wof), _spec_of(wor), _spec_of(wout)],
        out_specs=row_spec(d),
        out_shape=jax.ShapeDtypeStruct((rows, d), F32),
        compiler_params=_cparams(("arbitrary",)),
        name="merge",
    )(x2d, att, yf, yr, g, mod_l, *map(_arr_of, (woa, wof, wor, wout)))


_FF_CHUNK = 256
_FF_ROW_BLOCKS = 2


def _ffn_kernel(x_ref, xp_ref, xn_ref, g_ref, shift_ref, scale_ref, gate_ref,
                wu_ref, cw_ref, cb_ref, wd_ref, fin_ref, o_ref, h_ref, u0_ref, u1_ref, a_ref, *, tm, seq_len, final_norm):
    i = pl.program_id(0)
    tiles_per_seq = max(1, seq_len // tm)
    ext = tm + 2 * SUBLANES
    n_chunks = D_FF // _FF_CHUNK
    fc = _FF_CHUNK

    def prep(x):
        return (_rms(x, g_ref[...]) * (1.0 + scale_ref[...]) + shift_ref[...]).astype(BF16)

    first = (i % tiles_per_seq) == 0
    last = (i % tiles_per_seq) == tiles_per_seq - 1
    def cols(c):
        return pl.ds(pl.multiple_of(c * fc, fc), fc)

    def up_proj_rows(t, u_ref, lo, hi):
        for j in range(2):
            u_ref[lo:hi, j * fc:(j + 1) * fc] = _dot(h_ref[lo:hi, :], wu_ref[:, cols(j * n_chunks + t)])

    sp = -(-(SUBLANES + tm // 2) // (2 * SUBLANES)) * (2 * SUBLANES)
    hp = prep(xp_ref[...])
    h_ref[0:SUBLANES, :] = jnp.where(first, jnp.zeros_like(hp), hp)
    h_ref[SUBLANES:sp, :] = prep(x_ref[0:sp - SUBLANES, :])
    up_proj_rows(0, u0_ref, 0, sp)
    hn = prep(xn_ref[...])
    h_ref[sp:SUBLANES + tm, :] = prep(x_ref[sp - SUBLANES:tm, :])
    h_ref[SUBLANES + tm:, :] = jnp.where(last, jnp.zeros_like(hn), hn)
    up_proj_rows(0, u0_ref, sp, ext)

    n_blk = _FF_ROW_BLOCKS
    hm = tm // n_blk
    hme = hm + 2 * SUBLANES
    slh = slice(SUBLANES, SUBLANES + hm)

    def conv(u, cw, cb, r):
        prev = pltpu.roll(u, 1, axis=0)[slh]
        nxt = pltpu.roll(u, hme - 1, axis=0)[slh]
        if seq_len < tm:
            pos = (r * hm + lax.broadcasted_iota(jnp.int32, prev.shape, 0)) % seq_len
            prev = jnp.where(pos == 0, 0.0, prev)
            nxt = jnp.where(pos == seq_len - 1, 0.0, nxt)
        return cb + cw[0:1] * prev + cw[1:2] * u[slh] + cw[2:3] * nxt

    def activate(c, u_ref, r):
        rows = slice(r * hm, r * hm + hme)
        up = conv(u_ref[rows, 0:fc], cw_ref[:, cols(c)], cb_ref[:, cols(c)], r)
        gt = conv(u_ref[rows, fc:2 * fc], cw_ref[:, cols(n_chunks + c)], cb_ref[:, cols(n_chunks + c)], r)
        a_ref[r * hm:(r + 1) * hm, cols(c)] = _gelu(gt.astype(BF16)) * up.astype(BF16)

    u_bufs = (u0_ref, u1_ref)

    def step(t, parity, do_up=True, do_act=True):
        for r in range(n_blk):
            if do_up and r % (n_blk // 2) == 0:
                j = r // (n_blk // 2)
                u_bufs[parity][:, j * fc:(j + 1) * fc] = _dot(h_ref[...], wu_ref[:, cols(j * n_chunks + t)])
            if do_act:
                activate(t - 1, u_bufs[1 - parity], r)

    assert n_chunks % 2 == 1

    def body(k, carry):
        t = 2 * k + 1
        step(t, 1)
        step(t + 1, 0)
        return carry

    lax.fori_loop(0, n_chunks // 2, body, 0)
    for r in range(n_blk):
        activate(n_chunks - 1, u0_ref, r)
        rows = slice(r * hm, (r + 1) * hm)
        y = x_ref[rows, :] + gate_ref[...] * _dot(a_ref[rows, :], wd_ref[...])
        if final_norm:
            y = _rms(y, fin_ref[...])
        o_ref[rows, :] = y


def _ffn_call(x2d, mod_l, norm_g, wu, cw, cb, wd, fin_g, seq_len, mod_row, tm, final_norm):
    rows, d = x2d.shape
    assert seq_len % tm == 0 or tm % seq_len == 0
    blk8 = tm // SUBLANES
    n_blk8 = rows // SUBLANES

    def mod_spec(chunk):
        return pl.BlockSpec((None, None, 1, d), lambda i: (mod_row(i), chunk, 0, 0))

    return pl.pallas_call(
        functools.partial(_ffn_kernel, tm=tm, seq_len=seq_len, final_norm=final_norm),
        grid=(rows // tm,),
        in_specs=[
            pl.BlockSpec((tm, d), lambda i: (i, 0)),
            pl.BlockSpec((SUBLANES, d), lambda i: (jnp.maximum(i * blk8 - 1, 0), 0)),
            pl.BlockSpec((SUBLANES, d), lambda i: (jnp.minimum((i + 1) * blk8, n_blk8 - 1), 0)),
            _spec_of(norm_g),
            mod_spec(3), mod_spec(4), mod_spec(5),
            _spec_of(wu), _spec_of(cw), _spec_of(cb), _spec_of(wd),
            _resident((1, d)),
        ],
        out_specs=pl.BlockSpec((tm, d), lambda i: (i, 0)),
        out_shape=jax.ShapeDtypeStruct((rows, d), F32),
        scratch_shapes=[pltpu.VMEM((tm + 2 * SUBLANES, d), BF16),
                        pltpu.VMEM((tm + 2 * SUBLANES, 2 * _FF_CHUNK), F32),
                        pltpu.VMEM((tm + 2 * SUBLANES, 2 * _FF_CHUNK), F32),
                        pltpu.VMEM((tm, D_FF), BF16)],
        compiler_params=_cparams(("arbitrary",), flags=_INTERLEAVE),
        name="conv_ffn",
    )(x2d, x2d, x2d, _arr_of(norm_g), mod_l, mod_l, mod_l, *map(_arr_of, (wu, cw, cb, wd)), fin_g)


def _rope_tables(seq_len, rope):
    scale = QK_DIM ** -0.5 * math.log2(math.e)
    if rope:
        n_rows = seq_len // GRID_W
        row = jnp.repeat(jnp.arange(n_rows, dtype=F32), GRID_W)
        col = jnp.tile(jnp.arange(GRID_W, dtype=F32), n_rows)
        n_freq = QK_ROPE // 4
        inv = ROPE_BASE ** (-jnp.arange(n_freq, dtype=F32) / n_freq)
        ang = jnp.concatenate([row[:, None] * inv, col[:, None] * inv], axis=-1)
        cos, sin = jnp.cos(ang), jnp.sin(ang)
    else:
        cos = jnp.ones((seq_len, QK_ROPE // 2), F32)
        sin = jnp.zeros((seq_len, QK_ROPE // 2), F32)
    ones = jnp.ones((seq_len, QK_NOPE), F32)
    zeros = jnp.zeros((seq_len, QK_NOPE), F32)
    tail = jnp.zeros((seq_len, HEAD_PAD - QK_DIM), F32)
    cos2 = jnp.concatenate([cos, cos], axis=-1)
    sin2 = jnp.concatenate([-sin, sin], axis=-1)
    cq = scale * jnp.concatenate([ones, cos2, tail], axis=-1)
    sq = scale * jnp.concatenate([zeros, sin2, tail], axis=-1)
    ck = jnp.concatenate([zeros, cos2, tail], axis=-1)
    sk = jnp.concatenate([zeros, sin2, tail], axis=-1)
    return cq, sq, ck, sk


def _value_ones():
    lane = np.arange(N_HEADS * HEAD_PAD) % HEAD_PAD
    head = np.arange(N_HEADS * HEAD_PAD) // HEAD_PAD
    return jnp.asarray(np.where(head % 2 == 0, lane == V_DIM, lane == 0)[None, :], F32)


def _prep_w_in(w):
    n_front = Q_LORA + KV_LORA
    kr = w[..., n_front:n_front + QK_ROPE]
    z_lo = jnp.zeros(w.shape[:-1] + (QK_NOPE,), w.dtype)
    z_hi = jnp.zeros(w.shape[:-1] + (HEAD_PAD - QK_DIM,), w.dtype)
    w_kr = jnp.concatenate([z_lo, kr, z_hi], axis=-1)
    return w[..., :n_front].astype(BF16), w_kr.astype(BF16), w[..., n_front + QK_ROPE:].astype(BF16)


def _prep_w_uq(w):
    h = w.shape[-2]
    pad = jnp.zeros(w.shape[:-1] + (HEAD_PAD - QK_DIM,), w.dtype)
    return jnp.concatenate([w, pad], axis=-1).reshape(w.shape[:-2] + (h * HEAD_PAD,)).astype(BF16)


def _prep_w_ukv(w):
    h = w.shape[-2]
    flat = w.shape[:-2] + (h * HEAD_PAD,)
    k = jnp.concatenate([w[..., :QK_NOPE], jnp.zeros(w.shape[:-1] + (HEAD_PAD - QK_NOPE,), w.dtype)], axis=-1)
    v = w[..., QK_NOPE:]
    z = jnp.zeros_like(v)
    odd = (jnp.arange(h) % 2 == 1)[:, None]
    vp = jnp.concatenate([jnp.where(odd, z, v), jnp.where(odd, v, z)], axis=-1)
    return jnp.concatenate([k.reshape(flat), vp.reshape(flat)], axis=-1).astype(BF16)


def _prep_gates(w_a, b_a, w_x, b_x):
    half = _RNN_HALF
    heads_per_half = half // RNN_BLOCK

    def blockdiag(w):
        eye = jnp.eye(heads_per_half, dtype=w.dtype)
        return jnp.einsum('hij,hk->hikj', w, eye).reshape(half, half)

    ws, bs = [], []
    for j in range(RNN_WIDTH // half):
        hs = slice(j * heads_per_half, (j + 1) * heads_per_half)
        ls = slice(j * half, (j + 1) * half)
        ws.append(jnp.concatenate([blockdiag(w_a[0, hs]), blockdiag(w_x[0, hs]),
                                   blockdiag(w_a[1, hs]), blockdiag(w_x[1, hs])], axis=-1))
        bs.append(jnp.concatenate([b_a[0, ls], b_x[0, ls], b_a[1, ls], b_x[1, ls]], axis=-1)[None, :])
    return (0.5 * jnp.stack(ws)).astype(BF16), 0.5 * jnp.stack(bs)


def _dft_tables(seq_len):
    lo_n = 32 if seq_len % 32 == 0 else 1
    half = seq_len // 2
    t = jnp.arange(half, dtype=jnp.int32)

    def table(rows):
        ang = (2.0 * math.pi / seq_len) * ((rows[:, None] * t[None, :]) % seq_len).astype(F32)
        return jnp.cos(ang), jnp.sin(ang)

    c_hi, s_hi = table(jnp.arange(seq_len // lo_n, dtype=jnp.int32) * lo_n)
    c_lo, s_lo = table(jnp.arange(lo_n, dtype=jnp.int32))
    s = seq_len ** -0.5
    cos = (c_hi[:, None, :] * c_lo[None, :, :] - s_hi[:, None, :] * s_lo[None, :, :]).reshape(seq_len, half)
    sin = (s_hi[:, None, :] * c_lo[None, :, :] + c_hi[:, None, :] * s_lo[None, :, :]).reshape(seq_len, half)
    flip = jnp.flip(jnp.eye(half, dtype=BF16), axis=0)
    return jnp.concatenate([cos * s, sin * s], axis=-1).astype(BF16), flip


def _channel_dft():
    k = jnp.arange(F_GROUP_W, dtype=jnp.int32)
    ang = (2.0 * math.pi / F_GROUP_W) * ((k[:, None] * k[None, :]) % F_GROUP_W).astype(F32)
    s = F_GROUP_W ** -0.5
    eye = jnp.eye(F_GROUPS, dtype=F32)
    c = jnp.kron(eye, jnp.cos(ang) * s)
    sn = jnp.kron(eye, jnp.sin(ang) * s)
    return jnp.concatenate([c, -sn], axis=-1).astype(BF16)


def kernel(x, c, ctx, c_ctx, w_ada, b_ada, norm_mix, norm_ffn, w_in, q_norm, kv_norm, w_uq, w_ukv, w_o_attn,
           w_o_fourier, rnn_conv_w, rnn_conv_b, rg_w_a, rg_b_a, rg_w_x, rg_b_x, rg_lambda, w_o_rnn, w_out,
           w_up, ffn_conv_w, ffn_conv_b, w_down, final_norm):
    batch, seq_len, d = x.shape
    ctx_len = ctx.shape[1]
    depth = w_ada.shape[0]
    assert batch + 1 <= MOD_ROWS and d == D_MODEL

    cc = jnp.concatenate([c, c_ctx[None, :], jnp.zeros((MOD_ROWS - batch - 1, d), F32)], axis=0)
    mod = _ada_call(cc, w_ada, b_ada).reshape(depth, MOD_ROWS, N_MOD, 1, d)

    tabs_x = _rope_tables(seq_len, True)
    tabs_c = _rope_tables(ctx_len, False)
    dft_x, flip_x = _dft_tables(seq_len)
    dft_c, flip_c = _dft_tables(ctx_len)
    cs = _channel_dft()

    tm_in = min(512, seq_len)
    tm_x = min(512, seq_len)
    tm_c = min(256, ctx_len)
    tq = min(512, seq_len)
    tm_ffn = min(1024, seq_len)
    tm_ffn_c = min(1024, batch * ctx_len)

    def x_row(tm):
        return lambda i: i // (seq_len // tm)

    c_row = lambda i: batch

    xs = x.reshape(batch * seq_len, d)
    cs_tok = ctx.reshape(batch * ctx_len, d)

    w_in_all = _prep_w_in(w_in)
    wuq_all = _prep_w_uq(w_uq)
    wukv_all = _prep_w_ukv(w_ukv)
    wg_all, bg_all = jax.vmap(_prep_gates)(rg_w_a, rg_b_a, rg_w_x, rg_b_x)
    stacked = dict(
        nm=norm_mix[:, None, :], nf=norm_ffn[:, None, :], qn=q_norm[:, None, :], kvn=kv_norm[:, None, :],
        woa=w_o_attn.astype(BF16), wof=w_o_fourier.astype(BF16), wor=w_o_rnn.astype(BF16), wout=w_out.astype(BF16),
        wup=w_up.astype(BF16), fcw=ffn_conv_w, fcb=ffn_conv_b[:, None, :], wdn=w_down.astype(BF16),
        rcw=rnn_conv_w, rcb=rnn_conv_b[:, None, :], lam=rg_lambda, wuq=wuq_all, wukv=wukv_all, wg=wg_all, bg=bg_all)
    fin = final_norm[None, :]

    for l in range(depth):
        last = l == depth - 1
        mod_l = mod[l]
        p = {k: _Layered(v, l) for k, v in stacked.items()}
        w_in_p = tuple(_Layered(v, l) for v in w_in_all)
        nm, nf, qn, kvn, wuq, wukv, wg, bg = (p[k] for k in ('nm', 'nf', 'qn', 'kvn', 'wuq', 'wukv', 'wg', 'bg'))
        woa, wof, wor, wout = (p[k] for k in ('woa', 'wof', 'wor', 'wout'))
        wup, fcw, fcb, wdn, rcw, rcb = (p[k] for k in ('wup', 'fcw', 'fcb', 'wdn', 'rcw', 'rcb'))

        q_x, k_x, v_x, uf_x, ur_x, gug_x, g_x = _in_call(xs, mod_l, nm, w_in_p, qn, kvn, wuq, wukv, tabs_x,
                                                          seq_len, x_row(tm_in), tm_in)
        q_c, k_c, v_c, uf_c, ur_c, gug_c, g_c = _in_call(cs_tok, mod_l, nm, w_in_p, qn, kvn, wuq, wukv, tabs_c,
                                                          ctx_len, c_row, tm_c)

        att_x = _attn_call(q_x, [(k_c, v_c, ctx_len), (k_x, v_x, seq_len)], seq_len, tq)
        yf_x = _fourier_call(uf_x, cs, flip_x, dft_x, seq_len, min(512, seq_len))
        yr_c, yr_x = _rnn_call(ur_c, ur_x, gug_c, gug_x, rcw, rcb, wg, bg, p['lam'], ctx_len, seq_len)

        xs = _merge_call(xs, att_x, yf_x, yr_x, g_x, mod_l, woa, wof, wor, wout, x_row(tm_x), tm_x)
        xs = _ffn_call(xs, mod_l, nf, wup, fcw, fcb, wdn, fin, seq_len, x_row(tm_ffn), tm_ffn, last)

        if not last:
            att_c = _attn_call(q_c, [(k_c, v_c, ctx_len)], ctx_len, min(_ATTN_SUB_ROWS, ctx_len))
            yf_c = _fourier_call(uf_c, cs, flip_c, dft_c, ctx_len, ctx_len)
            cs_tok = _merge_call(cs_tok, att_c, yf_c, yr_c, g_c, mod_l, woa, wof, wor, wout, c_row, tm_c)
            cs_tok = _ffn_call(cs_tok, mod_l, nf, wup, fcw, fcb, wdn, fin, ctx_len, c_row, tm_ffn_c, False)

    return xs.reshape(batch, seq_len, d)
```

```python
import functools
import math
from typing import NamedTuple

import jax
import jax.numpy as jnp
import numpy as np
from jax import lax
from jax.experimental import pallas as pl
from jax.experimental.pallas import tpu as pltpu

F32 = jnp.float32
BF16 = jnp.bfloat16

D_MODEL = 1024
GRID_W = 64
N_HEADS = 8
Q_LORA = 384
KV_LORA = 256
QK_NOPE = 64
QK_ROPE = 32
V_DIM = 64
QK_DIM = QK_NOPE + QK_ROPE
ROPE_BASE = 10000.0
F_GROUPS = 4
F_GROUP_W = 128
F_WIDTH = F_GROUPS * F_GROUP_W
RNN_WIDTH = 512
RNN_HEADS = 8
RNN_BLOCK = RNN_WIDTH // RNN_HEADS
RNN_CONV = 4
LRU_C = 8.0
D_FF = 2816
FFN_CONV = 3
N_MOD = 6
EPS = 1e-6

LANES = 128
SUBLANES = 8
HEAD_PAD = 128
MOD_ROWS = 16

_O_UF = 0
_O_UR = _O_UF + F_WIDTH
_O_UG = _O_UR + RNN_WIDTH
_O_GL = _O_UG + RNN_WIDTH

_VMEM_LIMIT = 56 * 2 ** 20


_INTERLEAVE = None


def _cparams(sem, vmem=_VMEM_LIMIT, flags=None):
    return pltpu.CompilerParams(dimension_semantics=sem, vmem_limit_bytes=vmem, flags=flags)


def _resident(shape):
    nd = len(shape)
    return pl.BlockSpec(shape, lambda *_: (0,) * nd, pipeline_mode=pl.Buffered(1))


class _Layered(NamedTuple):
    array: jax.Array
    layer: int


def _spec_of(w):
    if isinstance(w, _Layered):
        shape = w.array.shape[1:]
        index = (w.layer,) + (0,) * len(shape)
        return pl.BlockSpec((None,) + shape, lambda *_: index, pipeline_mode=pl.Buffered(1))
    return _resident(w.shape)


def _arr_of(w):
    return w.array if isinstance(w, _Layered) else w


def _dot(a, b):
    return jnp.dot(a, b, preferred_element_type=F32)


def _sigmoid(x):
    return 0.5 * (1.0 + jnp.tanh(0.5 * x))


def _gelu(x):
    return 0.5 * x * (1.0 + jnp.tanh(math.sqrt(2.0 / math.pi) * (x + 0.044715 * (x * x * x))))


def _rms(x, g):
    return x * lax.rsqrt(jnp.mean(x * x, axis=-1, keepdims=True) + EPS) * g


def _ada_kernel(cc_ref, w_ref, b_ref, o_ref):
    cc = cc_ref[...]
    s = cc * _sigmoid(cc)
    o_ref[...] = _dot(s.astype(BF16), w_ref[...].astype(BF16)) + b_ref[...]


def _ada_call(cc, w_ada, b_ada):
    depth, d, n = w_ada.shape
    tn = 1536
    return pl.pallas_call(
        _ada_kernel,
        grid=(depth, n // tn),
        in_specs=[
            pl.BlockSpec((MOD_ROWS, d), lambda l, j: (0, 0)),
            pl.BlockSpec((None, d, tn), lambda l, j: (l, 0, j)),
            pl.BlockSpec((None, 1, tn), lambda l, j: (l, 0, j)),
        ],
        out_specs=pl.BlockSpec((None, MOD_ROWS, tn), lambda l, j: (l, 0, j)),
        out_shape=jax.ShapeDtypeStruct((depth, MOD_ROWS, n), F32),
        compiler_params=_cparams(("arbitrary", "arbitrary")),
        name="adaln",
    )(cc, w_ada, b_ada.reshape(depth, 1, n))


def _in_kernel(x_ref, g_ref, shift_ref, scale_ref, wf_ref, wkr_ref, w_ref, qn_ref, kvn_ref, wuq_ref, wukv_ref,
               cq_ref, sq_ref, ck_ref, sk_ref, vone_ref,
               q_out, k_out, v_out, uf_out, ur_out, gug_out, g_out):
    h = _rms(x_ref[...], g_ref[...]) * (1.0 + scale_ref[...]) + shift_ref[...]
    hb = h.astype(BF16)
    hp = N_HEADS * HEAD_PAD
    first_half = lax.broadcasted_iota(jnp.int32, (x_ref.shape[0], HEAD_PAD), 1) < QK_NOPE + QK_ROPE // 2

    def swap_rot(y):
        fwd = pltpu.roll(y, QK_ROPE // 2, axis=1)
        bwd = pltpu.roll(y, y.shape[1] - QK_ROPE // 2, axis=1)
        return [jnp.where(first_half, bwd[:, a:a + HEAD_PAD], fwd[:, a:a + HEAD_PAD])
                for a in range(0, y.shape[1], HEAD_PAD)]

    def gate_dot(j):
        return _dot(hb, w_ref[:, _O_GL + j * D_MODEL:_O_GL + (j + 1) * D_MODEL])

    def gate_out(j, z):
        g_out[:, j * D_MODEL:(j + 1) * D_MODEL] = _sigmoid(z).astype(BF16)

    cq = _dot(hb, wf_ref[:, 0:Q_LORA])
    ckv = _dot(hb, wf_ref[:, Q_LORA:Q_LORA + KV_LORA])
    kr = _dot(hb, wkr_ref[...])
    z0 = gate_dot(0)

    cqn = _rms(cq, qn_ref[...]).astype(BF16)
    ckvn = _rms(ckv, kvn_ref[...]).astype(BF16)
    y = _dot(cqn, wuq_ref[...])
    kv = _dot(ckvn, wukv_ref[...])
    gate_out(0, z0)
    z1 = gate_dot(1)

    y_sw = swap_rot(y)
    cq_t = cq_ref[...]
    sq_t = sq_ref[...]
    for hd in range(N_HEADS):
        a = hd * HEAD_PAD
        q_out[:, a:a + HEAD_PAD] = (y[:, a:a + HEAD_PAD] * cq_t + y_sw[hd] * sq_t).astype(BF16)
    z2 = gate_dot(2)

    k_rope = kr * ck_ref[...] + swap_rot(kr)[0] * sk_ref[...]
    for hd in range(N_HEADS):
        a = hd * HEAD_PAD
        k_out[:, a:a + HEAD_PAD] = (kv[:, a:a + HEAD_PAD] + k_rope).astype(BF16)
    v_out[...] = (kv[:, hp:] + vone_ref[...]).astype(BF16)
    uf = _dot(hb, w_ref[:, _O_UF:_O_UF + F_WIDTH])
    ur = _dot(hb, w_ref[:, _O_UR:_O_UR + RNN_WIDTH])
    gate_out(1, z1)
    ug = _dot(hb, w_ref[:, _O_UG:_O_UG + RNN_WIDTH])
    gate_out(2, z2)
    uf_out[...] = uf.astype(BF16)
    ur_out[...] = ur
    gug_out[...] = _gelu(ug).astype(BF16)


def _in_call(x2d, mod_l, norm_g, w_in_p, qn, kvn, wuq, wukv, tabs, seq_len, mod_row, tm):
    rows, d = x2d.shape
    tiles_per_seq = seq_len // tm
    hp = N_HEADS * HEAD_PAD

    def row_spec(width):
        return pl.BlockSpec((tm, width), lambda i: (i, 0))

    def mod_spec(chunk):
        return pl.BlockSpec((None, None, 1, d), lambda i: (mod_row(i), chunk, 0, 0))

    tab_spec = pl.BlockSpec((tm, HEAD_PAD), lambda i: (i % tiles_per_seq, 0))
    out_shapes = (
        jax.ShapeDtypeStruct((rows, hp), BF16),
        jax.ShapeDtypeStruct((rows, hp), BF16),
        jax.ShapeDtypeStruct((rows, hp), BF16),
        jax.ShapeDtypeStruct((rows, F_WIDTH), BF16),
        jax.ShapeDtypeStruct((rows, RNN_WIDTH), F32),
        jax.ShapeDtypeStruct((rows, RNN_WIDTH), BF16),
        jax.ShapeDtypeStruct((rows, 3 * d), BF16),
    )
    return pl.pallas_call(
        _in_kernel,
        grid=(rows // tm,),
        in_specs=[
            row_spec(d),
            _spec_of(norm_g),
            mod_spec(0),
            mod_spec(1),
            _spec_of(w_in_p[0]), _spec_of(w_in_p[1]), _spec_of(w_in_p[2]),
            _spec_of(qn),
            _spec_of(kvn),
            _spec_of(wuq),
            _spec_of(wukv),
            tab_spec, tab_spec, tab_spec, tab_spec,
            _resident((1, hp)),
        ],
        out_specs=tuple(row_spec(s.shape[1]) for s in out_shapes),
        out_shape=out_shapes,
        compiler_params=_cparams(("arbitrary",)),
        name="in_proj",
    )(x2d, _arr_of(norm_g), mod_l, mod_l, *map(_arr_of, w_in_p), _arr_of(qn), _arr_of(kvn), _arr_of(wuq), _arr_of(wukv),
      *tabs, _value_ones())


_ATTN_SUB_ROWS = 256


def _attn_kernel(*refs, n_seg):
    q_ref = refs[0]
    k_refs = refs[1:1 + n_seg]
    v_refs = refs[1 + n_seg:1 + 2 * n_seg]
    o_ref = refs[1 + 2 * n_seg]
    nt = (((1,), (1,)), ((), ()))
    sub = _ATTN_SUB_ROWS
    low = lax.broadcasted_iota(jnp.int32, (sub, HEAD_PAD), 1) < V_DIM

    def scores(item):
        r0, hd = item
        a = hd * HEAD_PAD
        qh = q_ref[r0:r0 + sub, a:a + HEAD_PAD]
        return [lax.dot_general(qh, k[:, a:a + HEAD_PAD], nt, preferred_element_type=F32) for k in k_refs]

    items = [(r0, hd) for r0 in range(0, q_ref.shape[0], sub) for hd in range(N_HEADS)]
    outs = {}
    s_next = scores(items[0])
    for idx, (r0, hd) in enumerate(items):
        a = hd * HEAD_PAD
        s = s_next
        if idx + 1 < len(items):
            s_next = scores(items[idx + 1])
        m = functools.reduce(jnp.maximum, [jnp.max(si, axis=-1, keepdims=True) for si in s])
        p = [jnp.exp2((si - m).astype(BF16)) for si in s]
        o = functools.reduce(jnp.add, [_dot(pi, v[:, a:a + HEAD_PAD]) for pi, v in zip(p, v_refs)])
        one_lane = V_DIM if hd % 2 == 0 else 0
        outs[hd] = o * (1.0 / o[:, one_lane:one_lane + 1])
        if hd % 2 == 1:
            o_ref[r0:r0 + sub, (hd - 1) * V_DIM:(hd + 1) * V_DIM] = jnp.where(low, outs.pop(hd - 1), outs.pop(hd)).astype(BF16)


def _attn_call(q, segs, q_len, tq):
    rows, hp = q.shape
    tiles = q_len // tq
    n_seg = len(segs)
    k_specs = [pl.BlockSpec((ln, hp), lambda i: (i // tiles, 0)) for _, _, ln in segs]
    return pl.pallas_call(
        functools.partial(_attn_kernel, n_seg=n_seg),
        grid=(rows // tq,),
        in_specs=[pl.BlockSpec((tq, hp), lambda i: (i, 0))] + k_specs + k_specs,
        out_specs=pl.BlockSpec((tq, N_HEADS * V_DIM), lambda i: (i, 0)),
        out_shape=jax.ShapeDtypeStruct((rows, N_HEADS * V_DIM), BF16),
        compiler_params=_cparams(("arbitrary",)),
        name="attention",
    )(q, *[k for k, _, _ in segs], *[v for _, v, _ in segs])


def _fourier_kernel(u_ref, cs_ref, flip_ref, dft_ref, o_ref, ab_ref, mid_ref, *, seq_len, tmf):
    half = seq_len // 2
    m = pl.program_id(1)

    @pl.when(m == 0)
    def _():
        u_lo = u_ref[0:half, :].astype(F32)
        rev = _dot(flip_ref[...], u_ref[half:seq_len, :])
        rev = pltpu.roll(rev, 1, axis=0)
        row0 = lax.broadcasted_iota(jnp.int32, rev.shape, 0) == 0
        ue = jnp.where(row0, u_lo, u_lo + rev).astype(BF16)
        uo = jnp.where(row0, 0.0, u_lo - rev).astype(BF16)
        cs = cs_ref[...]
        ab_ref[0:half, :] = _dot(ue, cs[:, :F_WIDTH]).astype(BF16)
        ab_ref[half:seq_len, :] = _dot(uo, cs[:, F_WIDTH:]).astype(BF16)
        mid_ref[...] = _dot(u_ref[half:half + 2 * SUBLANES, :], cs[:, :F_WIDTH])[0:SUBLANES, :]

    k = m * tmf + lax.broadcasted_iota(jnp.int32, (tmf, F_WIDTH), 0)
    sign = (1 - 2 * (k & 1)).astype(F32) * seq_len ** -0.5
    o_ref[...] = (_dot(dft_ref[...], ab_ref[...]) + sign * mid_ref[0:1, :]).astype(BF16)


def _fourier_call(uf, cs, flip, dft, seq_len, tmf):
    rows = uf.shape[0]
    tiles = seq_len // tmf
    return pl.pallas_call(
        functools.partial(_fourier_kernel, seq_len=seq_len, tmf=tmf),
        grid=(rows // seq_len, tiles),
        in_specs=[
            pl.BlockSpec((seq_len, F_WIDTH), lambda b, m: (b, 0)),
            _resident(cs.shape),
            _resident(flip.shape),
            pl.BlockSpec((tmf, seq_len), lambda b, m: (m, 0)),
        ],
        out_specs=pl.BlockSpec((tmf, F_WIDTH), lambda b, m: (b * tiles + m, 0)),
        out_shape=jax.ShapeDtypeStruct((rows, F_WIDTH), BF16),
        scratch_shapes=[pltpu.VMEM((seq_len, F_WIDTH), BF16), pltpu.VMEM((SUBLANES, F_WIDTH), F32)],
        compiler_params=_cparams(("arbitrary", "arbitrary")),
        name="fourier",
    )(uf, cs, flip, dft)


_RNN_HALF = RNN_WIDTH // 2
_RNN_CHUNK = 256


_RNN_SEGS = SUBLANES
_RNN_SEG_PAD = SUBLANES


def _rnn_kernel(urc_ref, urx_ref, gugc_ref, gugx_ref, cw_ref, cb_ref, wg_ref, bg_ref, lam_ref,
                yc_ref, yx_ref, pad_ref, af_ref, bf_ref, ab_ref, bb_ref, *, ctx_len, seq_len):
    half = _RNN_HALF
    n_slab = half // LANES
    zeros8 = jnp.zeros((SUBLANES, half), F32)
    row8 = lax.broadcasted_iota(jnp.int32, (_RNN_SEGS, LANES), 0)
    dirs = ((af_ref, bf_ref), (ab_ref, bb_ref))

    def gates(ur_ref, n, j):
        lo = j * half
        seg = n // _RNN_SEGS
        pitch = seg + _RNN_SEG_PAD
        pad_ref[0:SUBLANES, :] = zeros8
        pad_ref[SUBLANES:SUBLANES + n, :] = ur_ref[:, lo:lo + half]
        pad_ref[SUBLANES + n:2 * SUBLANES + n, :] = zeros8
        cw = cw_ref[:, lo:lo + half]
        cb = cb_ref[:, lo:lo + half]
        lam = lam_ref[:, lo:lo + half]
        k = (-0.5 * LRU_C * math.log2(math.e)) * (jnp.maximum(-lam, 0.0) + jnp.log1p(jnp.exp(-jnp.abs(lam))))
        wg = wg_ref[j]
        bg = bg_ref[j]
        ext_rows = _RNN_CHUNK + 2 * SUBLANES

        def put(ref, c, val):
            for sl in range(n_slab):
                v = val[:, sl * LANES:(sl + 1) * LANES]
                if seg == _RNN_CHUNK:
                    ref[sl, pl.ds(pl.multiple_of(c * pitch, SUBLANES), _RNN_CHUNK), :] = v
                else:
                    for sg in range(_RNN_SEGS):
                        ref[sl, sg * pitch:sg * pitch + seg, :] = v[sg * seg:(sg + 1) * seg, :]

        def chunk(c, carry):
            r0 = pl.multiple_of(c * _RNN_CHUNK, _RNN_CHUNK)
            ext = pad_ref[pl.ds(r0, ext_rows), :]
            sl = slice(SUBLANES, SUBLANES + _RNN_CHUNK)
            uc = (cb + cw[0:1] * pltpu.roll(ext, 2, axis=0)[sl] + cw[1:2] * pltpu.roll(ext, 1, axis=0)[sl]
                  + cw[2:3] * ext[sl] + cw[3:4] * pltpu.roll(ext, ext_rows - 1, axis=0)[sl])
            g = _dot(uc.astype(BF16), wg) + bg
            uch = 0.5 * uc
            for d, (a_ref, b_ref) in enumerate(dirs):
                kd = k[d:d + 1]
                a = jnp.exp2(kd + kd * jnp.tanh(g[:, (2 * d) * half:(2 * d + 1) * half]))
                put(a_ref, c, a)
                om = 1.0 - a * a
                sq = om * lax.rsqrt(jnp.maximum(om, 1e-30))
                put(b_ref, c, (sq * uch) * (1.0 + jnp.tanh(g[:, (2 * d + 1) * half:(2 * d + 2) * half])))
            return carry

        assert seg == _RNN_CHUNK or n == _RNN_CHUNK
        lax.fori_loop(0, n // _RNN_CHUNK, chunk, 0)

    def scan(n, h0):
        seg = n // _RNN_SEGS
        pitch = seg + _RNN_SEG_PAD
        ones = jnp.ones((_RNN_SEGS, LANES), F32)
        zeros = jnp.zeros((_RNN_SEGS, LANES), F32)

        def step(i, carry):
            out = []
            for d, (a_ref, b_ref) in enumerate(dirs):
                pos = i if d == 0 else seg - 1 - i
                for sl in range(n_slab):
                    p, h = carry[d * n_slab + sl]
                    rows = pl.ds(pos, _RNN_SEGS, stride=pitch)
                    a = a_ref[sl, rows, :]
                    p = a * p
                    h = a * h + b_ref[sl, rows, :]
                    a_ref[sl, rows, :] = p
                    b_ref[sl, rows, :] = h
                    out.append((p, h))
            return tuple(out)

        ends = lax.fori_loop(0, seg, step, tuple((ones, zeros) for _ in range(2 * n_slab)), unroll=2)

        starts, finals = [], []
        for d in range(2):
            for sl in range(n_slab):
                p, h = ends[d * n_slab + sl]
                cur = h0[d * n_slab + sl]
                hin = zeros
                order = range(_RNN_SEGS) if d == 0 else range(_RNN_SEGS - 1, -1, -1)
                for sg in order:
                    hin = jnp.where(row8 == sg, jnp.broadcast_to(cur, hin.shape), hin)
                    cur = p[sg:sg + 1, :] * cur + h[sg:sg + 1, :]
                starts.append(hin)
                finals.append(cur)
        return starts, finals

    def emit(n, starts, gug_ref, y_ref, lo):
        seg = n // _RNN_SEGS
        pitch = seg + _RNN_SEG_PAD
        for sg in range(_RNN_SEGS):
            src = slice(sg * pitch, sg * pitch + seg)
            dst = slice(sg * seg, (sg + 1) * seg)
            for sl in range(n_slab):
                hsum = None
                for d, (a_ref, b_ref) in enumerate(dirs):
                    hin = starts[d * n_slab + sl][sg:sg + 1, :]
                    hd = a_ref[sl, src, :] * hin + b_ref[sl, src, :]
                    hsum = hd if hsum is None else hsum + hd
                lanes = slice(lo + sl * LANES, lo + (sl + 1) * LANES)
                y_ref[dst, lanes] = (hsum * gug_ref[dst, lanes].astype(F32)).astype(BF16)

    zero_state = [jnp.zeros((1, LANES), F32)] * (2 * n_slab)
    for j in range(RNN_WIDTH // half):
        lo = j * half
        gates(urc_ref, ctx_len, j)
        starts, finals = scan(ctx_len, zero_state)
        emit(ctx_len, starts, gugc_ref, yc_ref, lo)
        gates(urx_ref, seq_len, j)
        starts, _ = scan(seq_len, finals)
        emit(seq_len, starts, gugx_ref, yx_ref, lo)


def _rnn_call(ur_c, ur_x, gug_c, gug_x, cw, cb, wg, bg, lam, ctx_len, seq_len):
    batch = ur_x.shape[0] // seq_len
    w = RNN_WIDTH

    def seq_spec(n):
        return pl.BlockSpec((n, w), lambda b: (b, 0))

    return pl.pallas_call(
        functools.partial(_rnn_kernel, ctx_len=ctx_len, seq_len=seq_len),
        grid=(batch,),
        in_specs=[seq_spec(ctx_len), seq_spec(seq_len), seq_spec(ctx_len), seq_spec(seq_len),
                  _spec_of(cw), _spec_of(cb), _spec_of(wg), _spec_of(bg), _spec_of(lam)],
        out_specs=(seq_spec(ctx_len), seq_spec(seq_len)),
        out_shape=(jax.ShapeDtypeStruct((batch * ctx_len, w), BF16),
                   jax.ShapeDtypeStruct((batch * seq_len, w), BF16)),
        scratch_shapes=[pltpu.VMEM((seq_len + 2 * SUBLANES, _RNN_HALF), F32)]
        + [pltpu.VMEM((_RNN_HALF // LANES, seq_len + _RNN_SEGS * _RNN_SEG_PAD, LANES), F32)] * 4,
        compiler_params=_cparams(("arbitrary",)),
        name="rglru",
    )(ur_c, ur_x, gug_c, gug_x, *map(_arr_of, (cw, cb, wg, bg, lam)))


_MERGE_SUB_ROWS = 256


def _merge_kernel(x_ref, att_ref, yf_ref, yr_ref, g_ref, gate_ref, woa_ref, wof_ref, wor_ref, wout_ref, o_ref):
    d = D_MODEL
    tm = x_ref.shape[0]
    sub = min(_MERGE_SUB_ROWS, tm)

    def branches(r0):
        rows = slice(r0, r0 + sub)
        return (_dot(att_ref[rows, :], woa_ref[...]), _dot(yf_ref[rows, :], wof_ref[...]),
                _dot(yr_ref[rows, :], wor_ref[...]))

    nxt = branches(0)
    for r0 in range(0, tm, sub):
        rows = slice(r0, r0 + sub)
        ya, yf, yr = nxt
        if r0 + sub < tm:
            nxt = branches(r0 + sub)
        m = (g_ref[rows, 0:d].astype(F32) * ya + g_ref[rows, d:2 * d].astype(F32) * yf
             + g_ref[rows, 2 * d:3 * d].astype(F32) * yr)
        o_ref[rows, :] = x_ref[rows, :] + gate_ref[...] * _dot(m.astype(BF16), wout_ref[...])


def _merge_call(x2d, att, yf, yr, g, mod_l, woa, wof, wor, wout, mod_row, tm):
    rows, d = x2d.shape

    def row_spec(width):
        return pl.BlockSpec((tm, width), lambda i: (i, 0))

    return pl.pallas_call(
        _merge_kernel,
        grid=(rows // tm,),
        in_specs=[row_spec(d), row_spec(att.shape[1]), row_spec(yf.shape[1]), row_spec(yr.shape[1]),
                  row_spec(3 * d),
                  pl.BlockSpec((None, None, 1, d), lambda i: (mod_row(i), 2, 0, 0)),
                  _spec_of(woa), _spec_of(wof), _spec_of(wor), _spec_of(wout)],
        out_specs=row_spec(d),
        out_shape=jax.ShapeDtypeStruct((rows, d), F32),
        compiler_params=_cparams(("arbitrary",)),
        name="merge",
    )(x2d, att, yf, yr, g, mod_l, *map(_arr_of, (woa, wof, wor, wout)))


_FF_CHUNK = 256
_FF_ROW_BLOCKS = 2


def _ffn_kernel(x_ref, xp_ref, xn_ref, g_ref, shift_ref, scale_ref, gate_ref,
                wu_ref, cw_ref, cb_ref, wd_ref, fin_ref, o_ref, h_ref, u0_ref, u1_ref, a_ref, *, tm, seq_len, final_norm):
    i = pl.program_id(0)
    tiles_per_seq = max(1, seq_len // tm)
    ext = tm + 2 * SUBLANES
    n_chunks = D_FF // _FF_CHUNK
    fc = _FF_CHUNK

    def prep(x):
        return (_rms(x, g_ref[...]) * (1.0 + scale_ref[...]) + shift_ref[...]).astype(BF16)

    first = (i % tiles_per_seq) == 0
    last = (i % tiles_per_seq) == tiles_per_seq - 1
    def cols(c):
        return pl.ds(pl.multiple_of(c * fc, fc), fc)

    def up_proj_rows(t, u_ref, lo, hi):
        for j in range(2):
            u_ref[lo:hi, j * fc:(j + 1) * fc] = _dot(h_ref[lo:hi, :], wu_ref[:, cols(j * n_chunks + t)])

    sp = -(-(SUBLANES + tm // 2) // (2 * SUBLANES)) * (2 * SUBLANES)
    hp = prep(xp_ref[...])
    h_ref[0:SUBLANES, :] = jnp.where(first, jnp.zeros_like(hp), hp)
    h_ref[SUBLANES:sp, :] = prep(x_ref[0:sp - SUBLANES, :])
    up_proj_rows(0, u0_ref, 0, sp)
    hn = prep(xn_ref[...])
    h_ref[sp:SUBLANES + tm, :] = prep(x_ref[sp - SUBLANES:tm, :])
    h_ref[SUBLANES + tm:, :] = jnp.where(last, jnp.zeros_like(hn), hn)
    up_proj_rows(0, u0_ref, sp, ext)

    n_blk = _FF_ROW_BLOCKS
    hm = tm // n_blk
    hme = hm + 2 * SUBLANES
    slh = slice(SUBLANES, SUBLANES + hm)

    def conv(u, cw, cb, r):
        prev = pltpu.roll(u, 1, axis=0)[slh]
        nxt = pltpu.roll(u, hme - 1, axis=0)[slh]
        if seq_len < tm:
            pos = (r * hm + lax.broadcasted_iota(jnp.int32, prev.shape, 0)) % seq_len
            prev = jnp.where(pos == 0, 0.0, prev)
            nxt = jnp.where(pos == seq_len - 1, 0.0, nxt)
        return cb + cw[0:1] * prev + cw[1:2] * u[slh] + cw[2:3] * nxt

    def activate(c, u_ref, r):
        rows = slice(r * hm, r * hm + hme)
        up = conv(u_ref[rows, 0:fc], cw_ref[:, cols(c)], cb_ref[:, cols(c)], r)
        gt = conv(u_ref[rows, fc:2 * fc], cw_ref[:, cols(n_chunks + c)], cb_ref[:, cols(n_chunks + c)], r)
        a_ref[r * hm:(r + 1) * hm, cols(c)] = _gelu(gt.astype(BF16)) * up.astype(BF16)

    u_bufs = (u0_ref, u1_ref)

    def step(t, parity, do_up=True, do_act=True):
        for r in range(n_blk):
            if do_up and r % (n_blk // 2) == 0:
                j = r // (n_blk // 2)
                u_bufs[parity][:, j * fc:(j + 1) * fc] = _dot(h_ref[...], wu_ref[:, cols(j * n_chunks + t)])
            if do_act:
                activate(t - 1, u_bufs[1 - parity], r)

    assert n_chunks % 2 == 1

    def body(k, carry):
        t = 2 * k + 1
        step(t, 1)
        step(t + 1, 0)
        return carry

    lax.fori_loop(0, n_chunks // 2, body, 0)
    for r in range(n_blk):
        activate(n_chunks - 1, u0_ref, r)
        rows = slice(r * hm, (r + 1) * hm)
        y = x_ref[rows, :] + gate_ref[...] * _dot(a_ref[rows, :], wd_ref[...])
        if final_norm:
            y = _rms(y, fin_ref[...])
        o_ref[rows, :] = y


def _ffn_call(x2d, mod_l, norm_g, wu, cw, cb, wd, fin_g, seq_len, mod_row, tm, final_norm):
    rows, d = x2d.shape
    assert seq_len % tm == 0 or tm % seq_len == 0
    blk8 = tm // SUBLANES
    n_blk8 = rows // SUBLANES

    def mod_spec(chunk):
        return pl.BlockSpec((None, None, 1, d), lambda i: (mod_row(i), chunk, 0, 0))

    return pl.pallas_call(
        functools.partial(_ffn_kernel, tm=tm, seq_len=seq_len, final_norm=final_norm),
        grid=(rows // tm,),
        in_specs=[
            pl.BlockSpec((tm, d), lambda i: (i, 0)),
            pl.BlockSpec((SUBLANES, d), lambda i: (jnp.maximum(i * blk8 - 1, 0), 0)),
            pl.BlockSpec((SUBLANES, d), lambda i: (jnp.minimum((i + 1) * blk8, n_blk8 - 1), 0)),
            _spec_of(norm_g),
            mod_spec(3), mod_spec(4), mod_spec(5),
            _spec_of(wu), _spec_of(cw), _spec_of(cb), _spec_of(wd),
            _resident((1, d)),
        ],
        out_specs=pl.BlockSpec((tm, d), lambda i: (i, 0)),
        out_shape=jax.ShapeDtypeStruct((rows, d), F32),
        scratch_shapes=[pltpu.VMEM((tm + 2 * SUBLANES, d), BF16),
                        pltpu.VMEM((tm + 2 * SUBLANES, 2 * _FF_CHUNK), F32),
                        pltpu.VMEM((tm + 2 * SUBLANES, 2 * _FF_CHUNK), F32),
                        pltpu.VMEM((tm, D_FF), BF16)],
        compiler_params=_cparams(("arbitrary",), flags=_INTERLEAVE),
        name="conv_ffn",
    )(x2d, x2d, x2d, _arr_of(norm_g), mod_l, mod_l, mod_l, *map(_arr_of, (wu, cw, cb, wd)), fin_g)


def _rope_tables(seq_len, rope):
    scale = QK_DIM ** -0.5 * math.log2(math.e)
    if rope:
        n_rows = seq_len // GRID_W
        row = jnp.repeat(jnp.arange(n_rows, dtype=F32), GRID_W)
        col = jnp.tile(jnp.arange(GRID_W, dtype=F32), n_rows)
        n_freq = QK_ROPE // 4
        inv = ROPE_BASE ** (-jnp.arange(n_freq, dtype=F32) / n_freq)
        ang = jnp.concatenate([row[:, None] * inv, col[:, None] * inv], axis=-1)
        cos, sin = jnp.cos(ang), jnp.sin(ang)
    else:
        cos = jnp.ones((seq_len, QK_ROPE // 2), F32)
        sin = jnp.zeros((seq_len, QK_ROPE // 2), F32)
    ones = jnp.ones((seq_len, QK_NOPE), F32)
    zeros = jnp.zeros((seq_len, QK_NOPE), F32)
    tail = jnp.zeros((seq_len, HEAD_PAD - QK_DIM), F32)
    cos2 = jnp.concatenate([cos, cos], axis=-1)
    sin2 = jnp.concatenate([-sin, sin], axis=-1)
    cq = scale * jnp.concatenate([ones, cos2, tail], axis=-1)
    sq = scale * jnp.concatenate([zeros, sin2, tail], axis=-1)
    ck = jnp.concatenate([zeros, cos2, tail], axis=-1)
    sk = jnp.concatenate([zeros, sin2, tail], axis=-1)
    return cq, sq, ck, sk


def _value_ones():
    lane = np.arange(N_HEADS * HEAD_PAD) % HEAD_PAD
    head = np.arange(N_HEADS * HEAD_PAD) // HEAD_PAD
    return jnp.asarray(np.where(head % 2 == 0, lane == V_DIM, lane == 0)[None, :], F32)


def _w_in_kernel(w_ref, front_ref, kr_ref, rest_ref):
    w = w_ref[...]
    n_front = Q_LORA + KV_LORA
    front_ref[...] = w[:, :n_front].astype(BF16)
    kr = w[:, n_front:n_front + QK_ROPE]
    rows = w.shape[0]
    kr_ref[...] = jnp.concatenate([jnp.zeros((rows, QK_NOPE), F32), kr, jnp.zeros((rows, HEAD_PAD - QK_DIM), F32)],
                                  axis=-1).astype(BF16)
    rest_ref[...] = w[:, n_front + QK_ROPE:].astype(BF16)


def _prep_w_in(w):
    depth, d, d_in = w.shape
    n_front = Q_LORA + KV_LORA
    n_rest = d_in - n_front - QK_ROPE
    tr = 256
    return pl.pallas_call(
        _w_in_kernel,
        grid=(depth, d // tr),
        in_specs=[pl.BlockSpec((None, tr, d_in), lambda l, i: (l, i, 0))],
        out_specs=(pl.BlockSpec((None, tr, n_front), lambda l, i: (l, i, 0)),
                   pl.BlockSpec((None, tr, HEAD_PAD), lambda l, i: (l, i, 0)),
                   pl.BlockSpec((None, tr, n_rest), lambda l, i: (l, i, 0))),
        out_shape=(jax.ShapeDtypeStruct((depth, d, n_front), BF16),
                   jax.ShapeDtypeStruct((depth, d, HEAD_PAD), BF16),
                   jax.ShapeDtypeStruct((depth, d, n_rest), BF16)),
        compiler_params=_cparams(("arbitrary", "arbitrary")),
        name="w_in_prep",
    )(w)


def _prep_w_uq(w):
    h = w.shape[-2]
    pad = jnp.zeros(w.shape[:-1] + (HEAD_PAD - QK_DIM,), w.dtype)
    return jnp.concatenate([w, pad], axis=-1).reshape(w.shape[:-2] + (h * HEAD_PAD,)).astype(BF16)


def _prep_w_ukv(w):
    h = w.shape[-2]
    flat = w.shape[:-2] + (h * HEAD_PAD,)
    k = jnp.concatenate([w[..., :QK_NOPE], jnp.zeros(w.shape[:-1] + (HEAD_PAD - QK_NOPE,), w.dtype)], axis=-1)
    v = w[..., QK_NOPE:]
    z = jnp.zeros_like(v)
    odd = (jnp.arange(h) % 2 == 1)[:, None]
    vp = jnp.concatenate([jnp.where(odd, z, v), jnp.where(odd, v, z)], axis=-1)
    return jnp.concatenate([k.reshape(flat), vp.reshape(flat)], axis=-1).astype(BF16)


def _prep_gates(w_a, b_a, w_x, b_x):
    half = _RNN_HALF
    heads_per_half = half // RNN_BLOCK

    def blockdiag(w):
        eye = jnp.eye(heads_per_half, dtype=w.dtype)
        return jnp.einsum('hij,hk->hikj', w, eye).reshape(half, half)

    ws, bs = [], []
    for j in range(RNN_WIDTH // half):
        hs = slice(j * heads_per_half, (j + 1) * heads_per_half)
        ls = slice(j * half, (j + 1) * half)
        ws.append(jnp.concatenate([blockdiag(w_a[0, hs]), blockdiag(w_x[0, hs]),
                                   blockdiag(w_a[1, hs]), blockdiag(w_x[1, hs])], axis=-1))
        bs.append(jnp.concatenate([b_a[0, ls], b_x[0, ls], b_a[1, ls], b_x[1, ls]], axis=-1)[None, :])
    return (0.5 * jnp.stack(ws)).astype(BF16), 0.5 * jnp.stack(bs)


def _dft_tables(seq_len):
    lo_n = 32 if seq_len % 32 == 0 else 1
    half = seq_len // 2
    t = jnp.arange(half, dtype=jnp.int32)

    def table(rows):
        ang = (2.0 * math.pi / seq_len) * ((rows[:, None] * t[None, :]) % seq_len).astype(F32)
        return jnp.cos(ang), jnp.sin(ang)

    c_hi, s_hi = table(jnp.arange(seq_len // lo_n, dtype=jnp.int32) * lo_n)
    c_lo, s_lo = table(jnp.arange(lo_n, dtype=jnp.int32))
    s = seq_len ** -0.5
    cos = (c_hi[:, None, :] * c_lo[None, :, :] - s_hi[:, None, :] * s_lo[None, :, :]).reshape(seq_len, half)
    sin = (s_hi[:, None, :] * c_lo[None, :, :] + c_hi[:, None, :] * s_lo[None, :, :]).reshape(seq_len, half)
    flip = (t[:, None] + t[None, :] == half - 1).astype(BF16)
    return jnp.concatenate([cos * s, sin * s], axis=-1).astype(BF16), flip


def _channel_dft():
    k = jnp.arange(F_GROUP_W, dtype=jnp.int32)
    ang = (2.0 * math.pi / F_GROUP_W) * ((k[:, None] * k[None, :]) % F_GROUP_W).astype(F32)
    s = F_GROUP_W ** -0.5
    eye = jnp.eye(F_GROUPS, dtype=F32)
    c = jnp.kron(eye, jnp.cos(ang) * s)
    sn = jnp.kron(eye, jnp.sin(ang) * s)
    return jnp.concatenate([c, -sn], axis=-1).astype(BF16)


def kernel(x, c, ctx, c_ctx, w_ada, b_ada, norm_mix, norm_ffn, w_in, q_norm, kv_norm, w_uq, w_ukv, w_o_attn,
           w_o_fourier, rnn_conv_w, rnn_conv_b, rg_w_a, rg_b_a, rg_w_x, rg_b_x, rg_lambda, w_o_rnn, w_out,
           w_up, ffn_conv_w, ffn_conv_b, w_down, final_norm):
    batch, seq_len, d = x.shape
    ctx_len = ctx.shape[1]
    depth = w_ada.shape[0]
    assert batch + 1 <= MOD_ROWS and d == D_MODEL

    cc = jnp.concatenate([c, c_ctx[None, :], jnp.zeros((MOD_ROWS - batch - 1, d), F32)], axis=0)
    mod = _ada_call(cc, w_ada, b_ada).reshape(depth, MOD_ROWS, N_MOD, 1, d)

    tabs_x = _rope_tables(seq_len, True)
    tabs_c = _rope_tables(ctx_len, False)
    dft_x, flip_x = _dft_tables(seq_len)
    dft_c, flip_c = _dft_tables(ctx_len)
    cs = _channel_dft()

    tm_in = min(512, seq_len)
    tm_x = min(512, seq_len)
    tm_c = min(256, ctx_len)
    tq = min(512, seq_len)
    tm_ffn = min(1024, seq_len)
    tm_ffn_c = min(1024, batch * ctx_len)

    def x_row(tm):
        return lambda i: i // (seq_len // tm)

    c_row = lambda i: batch

    xs = x.reshape(batch * seq_len, d)
    cs_tok = ctx.reshape(batch * ctx_len, d)

    w_in_all = _prep_w_in(w_in)
    wuq_all = _prep_w_uq(w_uq)
    wukv_all = _prep_w_ukv(w_ukv)
    wg_all, bg_all = jax.vmap(_prep_gates)(rg_w_a, rg_b_a, rg_w_x, rg_b_x)
    stacked = dict(
        nm=norm_mix[:, None, :], nf=norm_ffn[:, None, :], qn=q_norm[:, None, :], kvn=kv_norm[:, None, :],
        woa=w_o_attn.astype(BF16), wof=w_o_fourier.astype(BF16), wor=w_o_rnn.astype(BF16), wout=w_out.astype(BF16),
        wup=w_up.astype(BF16), fcw=ffn_conv_w, fcb=ffn_conv_b[:, None, :], wdn=w_down.astype(BF16),
        rcw=rnn_conv_w, rcb=rnn_conv_b[:, None, :], lam=rg_lambda, wuq=wuq_all, wukv=wukv_all, wg=wg_all, bg=bg_all)
    fin = final_norm[None, :]

    for l in range(depth):
        last = l == depth - 1
        mod_l = mod[l]
        p = {k: _Layered(v, l) for k, v in stacked.items()}
        w_in_p = tuple(_Layered(v, l) for v in w_in_all)
        nm, nf, qn, kvn, wuq, wukv, wg, bg = (p[k] for k in ('nm', 'nf', 'qn', 'kvn', 'wuq', 'wukv', 'wg', 'bg'))
        woa, wof, wor, wout = (p[k] for k in ('woa', 'wof', 'wor', 'wout'))
        wup, fcw, fcb, wdn, rcw, rcb = (p[k] for k in ('wup', 'fcw', 'fcb', 'wdn', 'rcw', 'rcb'))

        q_x, k_x, v_x, uf_x, ur_x, gug_x, g_x = _in_call(xs, mod_l, nm, w_in_p, qn, kvn, wuq, wukv, tabs_x,
                                                          seq_len, x_row(tm_in), tm_in)
        q_c, k_c, v_c, uf_c, ur_c, gug_c, g_c = _in_call(cs_tok, mod_l, nm, w_in_p, qn, kvn, wuq, wukv, tabs_c,
                                                          ctx_len, c_row, tm_c)

        att_x = _attn_call(q_x, [(k_c, v_c, ctx_len), (k_x, v_x, seq_len)], seq_len, tq)
        yf_x = _fourier_call(uf_x, cs, flip_x, dft_x, seq_len, min(512, seq_len))
        yr_c, yr_x = _rnn_call(ur_c, ur_x, gug_c, gug_x, rcw, rcb, wg, bg, p['lam'], ctx_len, seq_len)

        xs = _merge_call(xs, att_x, yf_x, yr_x, g_x, mod_l, woa, wof, wor, wout, x_row(tm_x), tm_x)
        xs = _ffn_call(xs, mod_l, nf, wup, fcw, fcb, wdn, fin, seq_len, x_row(tm_ffn), tm_ffn, last)

        if not last:
            att_c = _attn_call(q_c, [(k_c, v_c, ctx_len)], ctx_len, min(_ATTN_SUB_ROWS, ctx_len))
            yf_c = _fourier_call(uf_c, cs, flip_c, dft_c, ctx_len, ctx_len)
            cs_tok = _merge_call(cs_tok, att_c, yf_c, yr_c, g_c, mod_l, woa, wof, wor, wout, c_row, tm_c)
            cs_tok = _ffn_call(cs_tok, mod_l, nf, wup, fcw, fcb, wdn, fin, ctx_len, c_row, tm_ffn_c, False)

    return xs.reshape(batch, seq_len, d)
```

```python
import functools
import math
from typing import NamedTuple

import jax
import jax.numpy as jnp
import numpy as np
from jax import lax
from jax.experimental import pallas as pl
from jax.experimental.pallas import tpu as pltpu

F32 = jnp.float32
BF16 = jnp.bfloat16

D_MODEL = 1024
GRID_W = 64
N_HEADS = 8
Q_LORA = 384
KV_LORA = 256
QK_NOPE = 64
QK_ROPE = 32
V_DIM = 64
QK_DIM = QK_NOPE + QK_ROPE
ROPE_BASE = 10000.0
F_GROUPS = 4
F_GROUP_W = 128
F_WIDTH = F_GROUPS * F_GROUP_W
RNN_WIDTH = 512
RNN_HEADS = 8
RNN_BLOCK = RNN_WIDTH // RNN_HEADS
RNN_CONV = 4
LRU_C = 8.0
D_FF = 2816
FFN_CONV = 3
N_MOD = 6
EPS = 1e-6

LANES = 128
SUBLANES = 8
HEAD_PAD = 128
MOD_ROWS = 16

_O_UF = 0
_O_UR = _O_UF + F_WIDTH
_O_UG = _O_UR + RNN_WIDTH
_O_GL = _O_UG + RNN_WIDTH

_VMEM_LIMIT = 56 * 2 ** 20


_INTERLEAVE = None


def _cparams(sem, vmem=_VMEM_LIMIT, flags=None):
    return pltpu.CompilerParams(dimension_semantics=sem, vmem_limit_bytes=vmem, flags=flags)


def _resident(shape):
    nd = len(shape)
    return pl.BlockSpec(shape, lambda *_: (0,) * nd, pipeline_mode=pl.Buffered(1))


class _Layered(NamedTuple):
    array: jax.Array
    layer: int


def _spec_of(w):
    if isinstance(w, _Layered):
        shape = w.array.shape[1:]
        index = (w.layer,) + (0,) * len(shape)
        return pl.BlockSpec((None,) + shape, lambda *_: index, pipeline_mode=pl.Buffered(1))
    return _resident(w.shape)


def _arr_of(w):
    return w.array if isinstance(w, _Layered) else w


def _dot(a, b):
    return jnp.dot(a, b, preferred_element_type=F32)


def _sigmoid(x):
    return 0.5 * (1.0 + jnp.tanh(0.5 * x))


def _gelu(x):
    return 0.5 * x * (1.0 + jnp.tanh(math.sqrt(2.0 / math.pi) * (x + 0.044715 * (x * x * x))))


def _rms(x, g):
    return x * lax.rsqrt(jnp.mean(x * x, axis=-1, keepdims=True) + EPS) * g


def _ada_kernel(cc_ref, w_ref, b_ref, o_ref):
    cc = cc_ref[...]
    s = cc * _sigmoid(cc)
    o_ref[...] = _dot(s.astype(BF16), w_ref[...].astype(BF16)) + b_ref[...]


def _ada_call(cc, w_ada, b_ada):
    depth, d, n = w_ada.shape
    tn = 1536
    return pl.pallas_call(
        _ada_kernel,
        grid=(depth, n // tn),
        in_specs=[
            pl.BlockSpec((MOD_ROWS, d), lambda l, j: (0, 0)),
            pl.BlockSpec((None, d, tn), lambda l, j: (l, 0, j)),
            pl.BlockSpec((None, 1, tn), lambda l, j: (l, 0, j)),
        ],
        out_specs=pl.BlockSpec((None, MOD_ROWS, tn), lambda l, j: (l, 0, j)),
        out_shape=jax.ShapeDtypeStruct((depth, MOD_ROWS, n), F32),
        compiler_params=_cparams(("arbitrary", "arbitrary")),
        name="adaln",
    )(cc, w_ada, b_ada.reshape(depth, 1, n))


def _in_kernel(x_ref, g_ref, shift_ref, scale_ref, wf_ref, wkr_ref, w_ref, qn_ref, kvn_ref, wuq_ref, wukv_ref,
               cq_ref, sq_ref, ck_ref, sk_ref, vone_ref,
               q_out, k_out, v_out, uf_out, ur_out, gug_out, g_out):
    h = _rms(x_ref[...], g_ref[...]) * (1.0 + scale_ref[...]) + shift_ref[...]
    hb = h.astype(BF16)
    hp = N_HEADS * HEAD_PAD
    first_half = lax.broadcasted_iota(jnp.int32, (x_ref.shape[0], HEAD_PAD), 1) < QK_NOPE + QK_ROPE // 2

    def swap_rot(y):
        fwd = pltpu.roll(y, QK_ROPE // 2, axis=1)
        bwd = pltpu.roll(y, y.shape[1] - QK_ROPE // 2, axis=1)
        return [jnp.where(first_half, bwd[:, a:a + HEAD_PAD], fwd[:, a:a + HEAD_PAD])
                for a in range(0, y.shape[1], HEAD_PAD)]

    def gate_dot(j):
        return _dot(hb, w_ref[:, _O_GL + j * D_MODEL:_O_GL + (j + 1) * D_MODEL])

    def gate_out(j, z):
        g_out[:, j * D_MODEL:(j + 1) * D_MODEL] = _sigmoid(z).astype(BF16)

    cq = _dot(hb, wf_ref[:, 0:Q_LORA])
    ckv = _dot(hb, wf_ref[:, Q_LORA:Q_LORA + KV_LORA])
    kr = _dot(hb, wkr_ref[...])
    z0 = gate_dot(0)

    cqn = _rms(cq, qn_ref[...]).astype(BF16)
    ckvn = _rms(ckv, kvn_ref[...]).astype(BF16)
    y = _dot(cqn, wuq_ref[...])
    kv = _dot(ckvn, wukv_ref[...])
    gate_out(0, z0)
    z1 = gate_dot(1)

    y_sw = swap_rot(y)
    cq_t = cq_ref[...]
    sq_t = sq_ref[...]
    for hd in range(N_HEADS):
        a = hd * HEAD_PAD
        q_out[:, a:a + HEAD_PAD] = (y[:, a:a + HEAD_PAD] * cq_t + y_sw[hd] * sq_t).astype(BF16)
    z2 = gate_dot(2)

    k_rope = kr * ck_ref[...] + swap_rot(kr)[0] * sk_ref[...]
    for hd in range(N_HEADS):
        a = hd * HEAD_PAD
        k_out[:, a:a + HEAD_PAD] = (kv[:, a:a + HEAD_PAD] + k_rope).astype(BF16)
    v_out[...] = (kv[:, hp:] + vone_ref[...]).astype(BF16)
    uf = _dot(hb, w_ref[:, _O_UF:_O_UF + F_WIDTH])
    ur = _dot(hb, w_ref[:, _O_UR:_O_UR + RNN_WIDTH])
    gate_out(1, z1)
    ug = _dot(hb, w_ref[:, _O_UG:_O_UG + RNN_WIDTH])
    gate_out(2, z2)
    uf_out[...] = uf.astype(BF16)
    ur_out[...] = ur
    gug_out[...] = _gelu(ug).astype(BF16)


def _in_call(x2d, mod_l, norm_g, w_in_p, qn, kvn, wuq, wukv, tabs, seq_len, mod_row, tm):
    rows, d = x2d.shape
    tiles_per_seq = seq_len // tm
    hp = N_HEADS * HEAD_PAD

    def row_spec(width):
        return pl.BlockSpec((tm, width), lambda i: (i, 0))

    def mod_spec(chunk):
        return pl.BlockSpec((None, None, 1, d), lambda i: (mod_row(i), chunk, 0, 0))

    tab_spec = pl.BlockSpec((tm, HEAD_PAD), lambda i: (i % tiles_per_seq, 0))
    out_shapes = (
        jax.ShapeDtypeStruct((rows, hp), BF16),
        jax.ShapeDtypeStruct((rows, hp), BF16),
        jax.ShapeDtypeStruct((rows, hp), BF16),
        jax.ShapeDtypeStruct((rows, F_WIDTH), BF16),
        jax.ShapeDtypeStruct((rows, RNN_WIDTH), F32),
        jax.ShapeDtypeStruct((rows, RNN_WIDTH), BF16),
        jax.ShapeDtypeStruct((rows, 3 * d), BF16),
    )
    return pl.pallas_call(
        _in_kernel,
        grid=(rows // tm,),
        in_specs=[
            row_spec(d),
            _spec_of(norm_g),
            mod_spec(0),
            mod_spec(1),
            _spec_of(w_in_p[0]), _spec_of(w_in_p[1]), _spec_of(w_in_p[2]),
            _spec_of(qn),
            _spec_of(kvn),
            _spec_of(wuq),
            _spec_of(wukv),
            tab_spec, tab_spec, tab_spec, tab_spec,
            _resident((1, hp)),
        ],
        out_specs=tuple(row_spec(s.shape[1]) for s in out_shapes),
        out_shape=out_shapes,
        compiler_params=_cparams(("arbitrary",)),
        name="in_proj",
    )(x2d, _arr_of(norm_g), mod_l, mod_l, *map(_arr_of, w_in_p), _arr_of(qn), _arr_of(kvn), _arr_of(wuq), _arr_of(wukv),
      *tabs, _value_ones())


_ATTN_SUB_ROWS = 256


def _attn_kernel(*refs, n_seg):
    q_ref = refs[0]
    k_refs = refs[1:1 + n_seg]
    v_refs = refs[1 + n_seg:1 + 2 * n_seg]
    o_ref = refs[1 + 2 * n_seg]
    nt = (((1,), (1,)), ((), ()))
    sub = _ATTN_SUB_ROWS
    low = lax.broadcasted_iota(jnp.int32, (sub, HEAD_PAD), 1) < V_DIM

    def scores(item):
        r0, hd = item
        a = hd * HEAD_PAD
        qh = q_ref[r0:r0 + sub, a:a + HEAD_PAD]
        return [lax.dot_general(qh, k[:, a:a + HEAD_PAD], nt, preferred_element_type=F32) for k in k_refs]

    items = [(r0, hd) for r0 in range(0, q_ref.shape[0], sub) for hd in range(N_HEADS)]
    outs = {}
    s_next = scores(items[0])
    for idx, (r0, hd) in enumerate(items):
        a = hd * HEAD_PAD
        s = s_next
        if idx + 1 < len(items):
            s_next = scores(items[idx + 1])
        m = functools.reduce(jnp.maximum, [jnp.max(si, axis=-1, keepdims=True) for si in s])
        p = [jnp.exp2((si - m).astype(BF16)) for si in s]
        o = functools.reduce(jnp.add, [_dot(pi, v[:, a:a + HEAD_PAD]) for pi, v in zip(p, v_refs)])
        one_lane = V_DIM if hd % 2 == 0 else 0
        outs[hd] = o * (1.0 / o[:, one_lane:one_lane + 1])
        if hd % 2 == 1:
            o_ref[r0:r0 + sub, (hd - 1) * V_DIM:(hd + 1) * V_DIM] = jnp.where(low, outs.pop(hd - 1), outs.pop(hd)).astype(BF16)


def _attn_call(q, segs, q_len, tq):
    rows, hp = q.shape
    tiles = q_len // tq
    n_seg = len(segs)
    k_specs = [pl.BlockSpec((ln, hp), lambda i: (i // tiles, 0)) for _, _, ln in segs]
    return pl.pallas_call(
        functools.partial(_attn_kernel, n_seg=n_seg),
        grid=(rows // tq,),
        in_specs=[pl.BlockSpec((tq, hp), lambda i: (i, 0))] + k_specs + k_specs,
        out_specs=pl.BlockSpec((tq, N_HEADS * V_DIM), lambda i: (i, 0)),
        out_shape=jax.ShapeDtypeStruct((rows, N_HEADS * V_DIM), BF16),
        compiler_params=_cparams(("arbitrary",)),
        name="attention",
    )(q, *[k for k, _, _ in segs], *[v for _, v, _ in segs])


def _fourier_kernel(u_ref, cs_ref, flip_ref, dft_ref, o_ref, ab_ref, mid_ref, *, seq_len, tmf):
    half = seq_len // 2
    m = pl.program_id(1)

    @pl.when(m == 0)
    def _():
        u_lo = u_ref[0:half, :].astype(F32)
        rev = _dot(flip_ref[...], u_ref[half:seq_len, :])
        rev = pltpu.roll(rev, 1, axis=0)
        row0 = lax.broadcasted_iota(jnp.int32, rev.shape, 0) == 0
        ue = jnp.where(row0, u_lo, u_lo + rev).astype(BF16)
        uo = jnp.where(row0, 0.0, u_lo - rev).astype(BF16)
        cs = cs_ref[...]
        ab_ref[0:half, :] = _dot(ue, cs[:, :F_WIDTH]).astype(BF16)
        ab_ref[half:seq_len, :] = _dot(uo, cs[:, F_WIDTH:]).astype(BF16)
        mid_ref[...] = _dot(u_ref[half:half + 2 * SUBLANES, :], cs[:, :F_WIDTH])[0:SUBLANES, :]

    k = m * tmf + lax.broadcasted_iota(jnp.int32, (tmf, F_WIDTH), 0)
    sign = (1 - 2 * (k & 1)).astype(F32) * seq_len ** -0.5
    o_ref[...] = (_dot(dft_ref[...], ab_ref[...]) + sign * mid_ref[0:1, :]).astype(BF16)


def _fourier_call(uf, cs, flip, dft, seq_len, tmf):
    rows = uf.shape[0]
    tiles = seq_len // tmf
    return pl.pallas_call(
        functools.partial(_fourier_kernel, seq_len=seq_len, tmf=tmf),
        grid=(rows // seq_len, tiles),
        in_specs=[
            pl.BlockSpec((seq_len, F_WIDTH), lambda b, m: (b, 0)),
            _resident(cs.shape),
            _resident(flip.shape),
            pl.BlockSpec((tmf, seq_len), lambda b, m: (m, 0)),
        ],
        out_specs=pl.BlockSpec((tmf, F_WIDTH), lambda b, m: (b * tiles + m, 0)),
        out_shape=jax.ShapeDtypeStruct((rows, F_WIDTH), BF16),
        scratch_shapes=[pltpu.VMEM((seq_len, F_WIDTH), BF16), pltpu.VMEM((SUBLANES, F_WIDTH), F32)],
        compiler_params=_cparams(("arbitrary", "arbitrary")),
        name="fourier",
    )(uf, cs, flip, dft)


_RNN_HALF = RNN_WIDTH // 2
_RNN_CHUNK = 256


_RNN_SEGS = SUBLANES
_RNN_SEG_PAD = SUBLANES


def _rnn_kernel(urc_ref, urx_ref, gugc_ref, gugx_ref, cw_ref, cb_ref, wg_ref, bg_ref, lam_ref,
                yc_ref, yx_ref, pad_ref, af_ref, bf_ref, ab_ref, bb_ref, *, ctx_len, seq_len):
    half = _RNN_HALF
    n_slab = half // LANES
    zeros8 = jnp.zeros((SUBLANES, half), F32)
    row8 = lax.broadcasted_iota(jnp.int32, (_RNN_SEGS, LANES), 0)
    dirs = ((af_ref, bf_ref), (ab_ref, bb_ref))

    def gates(ur_ref, n, j):
        lo = j * half
        seg = n // _RNN_SEGS
        pitch = seg + _RNN_SEG_PAD
        pad_ref[0:SUBLANES, :] = zeros8
        pad_ref[SUBLANES:SUBLANES + n, :] = ur_ref[:, lo:lo + half]
        pad_ref[SUBLANES + n:2 * SUBLANES + n, :] = zeros8
        cw = cw_ref[:, lo:lo + half]
        cb = cb_ref[:, lo:lo + half]
        lam = lam_ref[:, lo:lo + half]
        k = (-0.5 * LRU_C * math.log2(math.e)) * (jnp.maximum(-lam, 0.0) + jnp.log1p(jnp.exp(-jnp.abs(lam))))
        wg = wg_ref[j]
        bg = bg_ref[j]
        ext_rows = _RNN_CHUNK + 2 * SUBLANES

        def put(ref, c, val):
            for sl in range(n_slab):
                v = val[:, sl * LANES:(sl + 1) * LANES]
                if seg == _RNN_CHUNK:
                    ref[sl, pl.ds(pl.multiple_of(c * pitch, SUBLANES), _RNN_CHUNK), :] = v
                else:
                    for sg in range(_RNN_SEGS):
                        ref[sl, sg * pitch:sg * pitch + seg, :] = v[sg * seg:(sg + 1) * seg, :]

        def chunk(c, carry):
            r0 = pl.multiple_of(c * _RNN_CHUNK, _RNN_CHUNK)
            ext = pad_ref[pl.ds(r0, ext_rows), :]
            sl = slice(SUBLANES, SUBLANES + _RNN_CHUNK)
            uc = (cb + cw[0:1] * pltpu.roll(ext, 2, axis=0)[sl] + cw[1:2] * pltpu.roll(ext, 1, axis=0)[sl]
                  + cw[2:3] * ext[sl] + cw[3:4] * pltpu.roll(ext, ext_rows - 1, axis=0)[sl])
            g = _dot(uc.astype(BF16), wg) + bg
            uch = 0.5 * uc
            for d, (a_ref, b_ref) in enumerate(dirs):
                kd = k[d:d + 1]
                a = jnp.exp2(kd + kd * jnp.tanh(g[:, (2 * d) * half:(2 * d + 1) * half]))
                put(a_ref, c, a)
                om = 1.0 - a * a
                sq = om * lax.rsqrt(jnp.maximum(om, 1e-30))
                put(b_ref, c, (sq * uch) * (1.0 + jnp.tanh(g[:, (2 * d + 1) * half:(2 * d + 2) * half])))
            return carry

        assert seg == _RNN_CHUNK or n == _RNN_CHUNK
        lax.fori_loop(0, n // _RNN_CHUNK, chunk, 0)

    def scan(n, h0):
        seg = n // _RNN_SEGS
        pitch = seg + _RNN_SEG_PAD
        ones = jnp.ones((_RNN_SEGS, LANES), F32)
        zeros = jnp.zeros((_RNN_SEGS, LANES), F32)

        def step(i, carry):
            out = []
            for d, (a_ref, b_ref) in enumerate(dirs):
                pos = i if d == 0 else seg - 1 - i
                for sl in range(n_slab):
                    p, h = carry[d * n_slab + sl]
                    rows = pl.ds(pos, _RNN_SEGS, stride=pitch)
                    a = a_ref[sl, rows, :]
                    p = a * p
                    h = a * h + b_ref[sl, rows, :]
                    a_ref[sl, rows, :] = p
                    b_ref[sl, rows, :] = h
                    out.append((p, h))
            return tuple(out)

        ends = lax.fori_loop(0, seg, step, tuple((ones, zeros) for _ in range(2 * n_slab)), unroll=2)

        starts, finals = [], []
        for d in range(2):
            for sl in range(n_slab):
                p, h = ends[d * n_slab + sl]
                cur = h0[d * n_slab + sl]
                hin = zeros
                order = range(_RNN_SEGS) if d == 0 else range(_RNN_SEGS - 1, -1, -1)
                for sg in order:
                    hin = jnp.where(row8 == sg, jnp.broadcast_to(cur, hin.shape), hin)
                    cur = p[sg:sg + 1, :] * cur + h[sg:sg + 1, :]
                starts.append(hin)
                finals.append(cur)
        return starts, finals

    def emit(n, starts, gug_ref, y_ref, lo):
        seg = n // _RNN_SEGS
        pitch = seg + _RNN_SEG_PAD
        for sg in range(_RNN_SEGS):
            src = slice(sg * pitch, sg * pitch + seg)
            dst = slice(sg * seg, (sg + 1) * seg)
            for sl in range(n_slab):
                hsum = None
                for d, (a_ref, b_ref) in enumerate(dirs):
                    hin = starts[d * n_slab + sl][sg:sg + 1, :]
                    hd = a_ref[sl, src, :] * hin + b_ref[sl, src, :]
                    hsum = hd if hsum is None else hsum + hd
                lanes = slice(lo + sl * LANES, lo + (sl + 1) * LANES)
                y_ref[dst, lanes] = (hsum * gug_ref[dst, lanes].astype(F32)).astype(BF16)

    zero_state = [jnp.zeros((1, LANES), F32)] * (2 * n_slab)
    for j in range(RNN_WIDTH // half):
        lo = j * half
        gates(urc_ref, ctx_len, j)
        starts, finals = scan(ctx_len, zero_state)
        emit(ctx_len, starts, gugc_ref, yc_ref, lo)
        gates(urx_ref, seq_len, j)
        starts, _ = scan(seq_len, finals)
        emit(seq_len, starts, gugx_ref, yx_ref, lo)


def _rnn_call(ur_c, ur_x, gug_c, gug_x, cw, cb, wg, bg, lam, ctx_len, seq_len):
    batch = ur_x.shape[0] // seq_len
    w = RNN_WIDTH

    def seq_spec(n):
        return pl.BlockSpec((n, w), lambda b: (b, 0))

    return pl.pallas_call(
        functools.partial(_rnn_kernel, ctx_len=ctx_len, seq_len=seq_len),
        grid=(batch,),
        in_specs=[seq_spec(ctx_len), seq_spec(seq_len), seq_spec(ctx_len), seq_spec(seq_len),
                  _spec_of(cw), _spec_of(cb), _spec_of(wg), _spec_of(bg), _spec_of(lam)],
        out_specs=(seq_spec(ctx_len), seq_spec(seq_len)),
        out_shape=(jax.ShapeDtypeStruct((batch * ctx_len, w), BF16),
                   jax.ShapeDtypeStruct((batch * seq_len, w), BF16)),
        scratch_shapes=[pltpu.VMEM((seq_len + 2 * SUBLANES, _RNN_HALF), F32)]
        + [pltpu.VMEM((_RNN_HALF // LANES, seq_len + _RNN_SEGS * _RNN_SEG_PAD, LANES), F32)] * 4,
        compiler_params=_cparams(("arbitrary",)),
        name="rglru",
    )(ur_c, ur_x, gug_c, gug_x, *map(_arr_of, (cw, cb, wg, bg, lam)))


_MERGE_SUB_ROWS = 256


def _merge_kernel(x_ref, att_ref, yf_ref, yr_ref, g_ref, gate_ref, woa_ref, wof_ref, wor_ref, wout_ref, o_ref):
    d = D_MODEL
    tm = x_ref.shape[0]
    sub = min(_MERGE_SUB_ROWS, tm)

    def branches(r0):
        rows = slice(r0, r0 + sub)
        return (_dot(att_ref[rows, :], woa_ref[...]), _dot(yf_ref[rows, :], wof_ref[...]),
                _dot(yr_ref[rows, :], wor_ref[...]))

    nxt = branches(0)
    for r0 in range(0, tm, sub):
        rows = slice(r0, r0 + sub)
        ya, yf, yr = nxt
        if r0 + sub < tm:
            nxt = branches(r0 + sub)
        m = (g_ref[rows, 0:d].astype(F32) * ya + g_ref[rows, d:2 * d].astype(F32) * yf
             + g_ref[rows, 2 * d:3 * d].astype(F32) * yr)
        o_ref[rows, :] = x_ref[rows, :] + gate_ref[...] * _dot(m.astype(BF16), wout_ref[...])


def _merge_call(x2d, att, yf, yr, g, mod_l, woa, wof, wor, wout, mod_row, tm):
    rows, d = x2d.shape

    def row_spec(width):
        return pl.BlockSpec((tm, width), lambda i: (i, 0))

    return pl.pallas_call(
        _merge_kernel,
        grid=(rows // tm,),
        in_specs=[row_spec(d), row_spec(att.shape[1]), row_spec(yf.shape[1]), row_spec(yr.shape[1]),
                  row_spec(3 * d),
                  pl.BlockSpec((None, None, 1, d), lambda i: (mod_row(i), 2, 0, 0)),
                  _spec_of(woa), _spec_of(wof), _spec_of(wor), _spec_of(wout)],
        out_specs=row_spec(d),
        out_shape=jax.ShapeDtypeStruct((rows, d), F32),
        compiler_params=_cparams(("arbitrary",)),
        name="merge",
    )(x2d, att, yf, yr, g, mod_l, *map(_arr_of, (woa, wof, wor, wout)))


_FF_CHUNK = 256
_FF_ROW_BLOCKS = 2


def _ffn_kernel(x_ref, xp_ref, xn_ref, g_ref, shift_ref, scale_ref, gate_ref,
                wu_ref, cw_ref, cb_ref, wd_ref, fin_ref, o_ref, h_ref, u0_ref, u1_ref, a_ref, *, tm, seq_len, final_norm):
    i = pl.program_id(0)
    tiles_per_seq = max(1, seq_len // tm)
    ext = tm + 2 * SUBLANES
    n_chunks = D_FF // _FF_CHUNK
    fc = _FF_CHUNK

    def prep(x):
        return (_rms(x, g_ref[...]) * (1.0 + scale_ref[...]) + shift_ref[...]).astype(BF16)

    first = (i % tiles_per_seq) == 0
    last = (i % tiles_per_seq) == tiles_per_seq - 1
    def cols(c):
        return pl.ds(pl.multiple_of(c * fc, fc), fc)

    def up_proj_rows(t, u_ref, lo, hi):
        for j in range(2):
            u_ref[lo:hi, j * fc:(j + 1) * fc] = _dot(h_ref[lo:hi, :], wu_ref[:, cols(j * n_chunks + t)])

    sp = -(-(SUBLANES + tm // 2) // (2 * SUBLANES)) * (2 * SUBLANES)
    hp = prep(xp_ref[...])
    h_ref[0:SUBLANES, :] = jnp.where(first, jnp.zeros_like(hp), hp)
    h_ref[SUBLANES:sp, :] = prep(x_ref[0:sp - SUBLANES, :])
    up_proj_rows(0, u0_ref, 0, sp)
    hn = prep(xn_ref[...])
    h_ref[sp:SUBLANES + tm, :] = prep(x_ref[sp - SUBLANES:tm, :])
    h_ref[SUBLANES + tm:, :] = jnp.where(last, jnp.zeros_like(hn), hn)
    up_proj_rows(0, u0_ref, sp, ext)

    n_blk = _FF_ROW_BLOCKS
    hm = tm // n_blk
    hme = hm + 2 * SUBLANES
    slh = slice(SUBLANES, SUBLANES + hm)

    def conv(u, cw, cb, r):
        prev = pltpu.roll(u, 1, axis=0)[slh]
        nxt = pltpu.roll(u, hme - 1, axis=0)[slh]
        if seq_len < tm:
            pos = (r * hm + lax.broadcasted_iota(jnp.int32, prev.shape, 0)) % seq_len
            prev = jnp.where(pos == 0, 0.0, prev)
            nxt = jnp.where(pos == seq_len - 1, 0.0, nxt)
        return cb + cw[0:1] * prev + cw[1:2] * u[slh] + cw[2:3] * nxt

    def activate(c, u_ref, r):
        rows = slice(r * hm, r * hm + hme)
        up = conv(u_ref[rows, 0:fc], cw_ref[:, cols(c)], cb_ref[:, cols(c)], r)
        gt = conv(u_ref[rows, fc:2 * fc], cw_ref[:, cols(n_chunks + c)], cb_ref[:, cols(n_chunks + c)], r)
        a_ref[r * hm:(r + 1) * hm, cols(c)] = _gelu(gt.astype(BF16)) * up.astype(BF16)

    u_bufs = (u0_ref, u1_ref)

    def step(t, parity, do_up=True, do_act=True):
        for r in range(n_blk):
            if do_up and r % (n_blk // 2) == 0:
                j = r // (n_blk // 2)
                u_bufs[parity][:, j * fc:(j + 1) * fc] = _dot(h_ref[...], wu_ref[:, cols(j * n_chunks + t)])
            if do_act:
                activate(t - 1, u_bufs[1 - parity], r)

    assert n_chunks % 2 == 1

    def body(k, carry):
        t = 2 * k + 1
        step(t, 1)
        step(t + 1, 0)
        return carry

    lax.fori_loop(0, n_chunks // 2, body, 0)
    for r in range(n_blk):
        activate(n_chunks - 1, u0_ref, r)
        rows = slice(r * hm, (r + 1) * hm)
        y = x_ref[rows, :] + gate_ref[...] * _dot(a_ref[rows, :], wd_ref[...])
        if final_norm:
            y = _rms(y, fin_ref[...])
        o_ref[rows, :] = y


def _ffn_call(x2d, mod_l, norm_g, wu, cw, cb, wd, fin_g, seq_len, mod_row, tm, final_norm):
    rows, d = x2d.shape
    assert seq_len % tm == 0 or tm % seq_len == 0
    blk8 = tm // SUBLANES
    n_blk8 = rows // SUBLANES

    def mod_spec(chunk):
        return pl.BlockSpec((None, None, 1, d), lambda i: (mod_row(i), chunk, 0, 0))

    return pl.pallas_call(
        functools.partial(_ffn_kernel, tm=tm, seq_len=seq_len, final_norm=final_norm),
        grid=(rows // tm,),
        in_specs=[
            pl.BlockSpec((tm, d), lambda i: (i, 0)),
            pl.BlockSpec((SUBLANES, d), lambda i: (jnp.maximum(i * blk8 - 1, 0), 0)),
            pl.BlockSpec((SUBLANES, d), lambda i: (jnp.minimum((i + 1) * blk8, n_blk8 - 1), 0)),
            _spec_of(norm_g),
            mod_spec(3), mod_spec(4), mod_spec(5),
            _spec_of(wu), _spec_of(cw), _spec_of(cb), _spec_of(wd),
            _resident((1, d)),
        ],
        out_specs=pl.BlockSpec((tm, d), lambda i: (i, 0)),
        out_shape=jax.ShapeDtypeStruct((rows, d), F32),
        scratch_shapes=[pltpu.VMEM((tm + 2 * SUBLANES, d), BF16),
                        pltpu.VMEM((tm + 2 * SUBLANES, 2 * _FF_CHUNK), F32),
                        pltpu.VMEM((tm + 2 * SUBLANES, 2 * _FF_CHUNK), F32),
                        pltpu.VMEM((tm, D_FF), BF16)],
        compiler_params=_cparams(("arbitrary",), flags=_INTERLEAVE),
        name="conv_ffn",
    )(x2d, x2d, x2d, _arr_of(norm_g), mod_l, mod_l, mod_l, *map(_arr_of, (wu, cw, cb, wd)), fin_g)


def _rope_tables(seq_len, rope):
    scale = QK_DIM ** -0.5 * math.log2(math.e)
    if rope:
        n_rows = seq_len // GRID_W
        row = jnp.repeat(jnp.arange(n_rows, dtype=F32), GRID_W)
        col = jnp.tile(jnp.arange(GRID_W, dtype=F32), n_rows)
        n_freq = QK_ROPE // 4
        inv = ROPE_BASE ** (-jnp.arange(n_freq, dtype=F32) / n_freq)
        ang = jnp.concatenate([row[:, None] * inv, col[:, None] * inv], axis=-1)
        cos, sin = jnp.cos(ang), jnp.sin(ang)
    else:
        cos = jnp.ones((seq_len, QK_ROPE // 2), F32)
        sin = jnp.zeros((seq_len, QK_ROPE // 2), F32)
    ones = jnp.ones((seq_len, QK_NOPE), F32)
    zeros = jnp.zeros((seq_len, QK_NOPE), F32)
    tail = jnp.zeros((seq_len, HEAD_PAD - QK_DIM), F32)
    cos2 = jnp.concatenate([cos, cos], axis=-1)
    sin2 = jnp.concatenate([-sin, sin], axis=-1)
    cq = scale * jnp.concatenate([ones, cos2, tail], axis=-1)
    sq = scale * jnp.concatenate([zeros, sin2, tail], axis=-1)
    ck = jnp.concatenate([zeros, cos2, tail], axis=-1)
    sk = jnp.concatenate([zeros, sin2, tail], axis=-1)
    return cq, sq, ck, sk


def _value_ones():
    lane = np.arange(N_HEADS * HEAD_PAD) % HEAD_PAD
    head = np.arange(N_HEADS * HEAD_PAD) // HEAD_PAD
    return jnp.asarray(np.where(head % 2 == 0, lane == V_DIM, lane == 0)[None, :], F32)


def _prep_w_in(w):
    n_front = Q_LORA + KV_LORA
    w = w.astype(BF16)
    kr = w[..., n_front:n_front + QK_ROPE]
    z_lo = jnp.zeros(w.shape[:-1] + (QK_NOPE,), w.dtype)
    z_hi = jnp.zeros(w.shape[:-1] + (HEAD_PAD - QK_DIM,), w.dtype)
    w_kr = jnp.concatenate([z_lo, kr, z_hi], axis=-1)
    return w[..., :n_front], w_kr, w[..., n_front + QK_ROPE:]


def _prep_w_uq(w):
    h = w.shape[-2]
    pad = jnp.zeros(w.shape[:-1] + (HEAD_PAD - QK_DIM,), w.dtype)
    return jnp.concatenate([w, pad], axis=-1).reshape(w.shape[:-2] + (h * HEAD_PAD,)).astype(BF16)


def _prep_w_ukv(w):
    h = w.shape[-2]
    flat = w.shape[:-2] + (h * HEAD_PAD,)
    k = jnp.concatenate([w[..., :QK_NOPE], jnp.zeros(w.shape[:-1] + (HEAD_PAD - QK_NOPE,), w.dtype)], axis=-1)
    v = w[..., QK_NOPE:]
    z = jnp.zeros_like(v)
    odd = (jnp.arange(h) % 2 == 1)[:, None]
    vp = jnp.concatenate([jnp.where(odd, z, v), jnp.where(odd, v, z)], axis=-1)
    return jnp.concatenate([k.reshape(flat), vp.reshape(flat)], axis=-1).astype(BF16)


def _prep_gates(w_a, b_a, w_x, b_x):
    half = _RNN_HALF
    heads_per_half = half // RNN_BLOCK

    def blockdiag(w):
        eye = jnp.eye(heads_per_half, dtype=w.dtype)
        return jnp.einsum('hij,hk->hikj', w, eye).reshape(half, half)

    ws, bs = [], []
    for j in range(RNN_WIDTH // half):
        hs = slice(j * heads_per_half, (j + 1) * heads_per_half)
        ls = slice(j * half, (j + 1) * half)
        ws.append(jnp.concatenate([blockdiag(w_a[0, hs]), blockdiag(w_x[0, hs]),
                                   blockdiag(w_a[1, hs]), blockdiag(w_x[1, hs])], axis=-1))
        bs.append(jnp.concatenate([b_a[0, ls], b_x[0, ls], b_a[1, ls], b_x[1, ls]], axis=-1)[None, :])
    return (0.5 * jnp.stack(ws)).astype(BF16), 0.5 * jnp.stack(bs)


def _dft_tables(seq_len):
    lo_n = 32 if seq_len % 32 == 0 else 1
    half = seq_len // 2
    t = jnp.arange(half, dtype=jnp.int32)

    def table(rows):
        ang = (2.0 * math.pi / seq_len) * ((rows[:, None] * t[None, :]) % seq_len).astype(F32)
        return jnp.cos(ang), jnp.sin(ang)

    c_hi, s_hi = table(jnp.arange(seq_len // lo_n, dtype=jnp.int32) * lo_n)
    c_lo, s_lo = table(jnp.arange(lo_n, dtype=jnp.int32))
    s = seq_len ** -0.5
    cos = (c_hi[:, None, :] * c_lo[None, :, :] - s_hi[:, None, :] * s_lo[None, :, :]).reshape(seq_len, half)
    sin = (s_hi[:, None, :] * c_lo[None, :, :] + c_hi[:, None, :] * s_lo[None, :, :]).reshape(seq_len, half)
    flip = (t[:, None] + t[None, :] == half - 1).astype(BF16)
    return jnp.concatenate([cos * s, sin * s], axis=-1).astype(BF16), flip


def _channel_dft():
    k = jnp.arange(F_GROUP_W, dtype=jnp.int32)
    ang = (2.0 * math.pi / F_GROUP_W) * ((k[:, None] * k[None, :]) % F_GROUP_W).astype(F32)
    s = F_GROUP_W ** -0.5
    eye = jnp.eye(F_GROUPS, dtype=F32)
    c = jnp.kron(eye, jnp.cos(ang) * s)
    sn = jnp.kron(eye, jnp.sin(ang) * s)
    return jnp.concatenate([c, -sn], axis=-1).astype(BF16)


def kernel(x, c, ctx, c_ctx, w_ada, b_ada, norm_mix, norm_ffn, w_in, q_norm, kv_norm, w_uq, w_ukv, w_o_attn,
           w_o_fourier, rnn_conv_w, rnn_conv_b, rg_w_a, rg_b_a, rg_w_x, rg_b_x, rg_lambda, w_o_rnn, w_out,
           w_up, ffn_conv_w, ffn_conv_b, w_down, final_norm):
    batch, seq_len, d = x.shape
    ctx_len = ctx.shape[1]
    depth = w_ada.shape[0]
    assert batch + 1 <= MOD_ROWS and d == D_MODEL

    cc = jnp.concatenate([c, c_ctx[None, :], jnp.zeros((MOD_ROWS - batch - 1, d), F32)], axis=0)
    mod = _ada_call(cc, w_ada, b_ada).reshape(depth, MOD_ROWS, N_MOD, 1, d)

    tabs_x = _rope_tables(seq_len, True)
    tabs_c = _rope_tables(ctx_len, False)
    dft_x, flip_x = _dft_tables(seq_len)
    dft_c, flip_c = _dft_tables(ctx_len)
    cs = _channel_dft()

    tm_in = min(512, seq_len)
    tm_x = min(512, seq_len)
    tm_c = min(256, ctx_len)
    tq = min(512, seq_len)
    tm_ffn = min(1024, seq_len)
    tm_ffn_c = min(1024, batch * ctx_len)

    def x_row(tm):
        return lambda i: i // (seq_len // tm)

    c_row = lambda i: batch

    xs = x.reshape(batch * seq_len, d)
    cs_tok = ctx.reshape(batch * ctx_len, d)

    w_in_all = _prep_w_in(w_in)
    wuq_all = _prep_w_uq(w_uq)
    wukv_all = _prep_w_ukv(w_ukv)
    wg_all, bg_all = jax.vmap(_prep_gates)(rg_w_a, rg_b_a, rg_w_x, rg_b_x)
    stacked = dict(
        nm=norm_mix[:, None, :], nf=norm_ffn[:, None, :], qn=q_norm[:, None, :], kvn=kv_norm[:, None, :],
        woa=w_o_attn.astype(BF16), wof=w_o_fourier.astype(BF16), wor=w_o_rnn.astype(BF16), wout=w_out.astype(BF16),
        wup=w_up.astype(BF16), fcw=ffn_conv_w, fcb=ffn_conv_b[:, None, :], wdn=w_down.astype(BF16),
        rcw=rnn_conv_w, rcb=rnn_conv_b[:, None, :], lam=rg_lambda, wuq=wuq_all, wukv=wukv_all, wg=wg_all, bg=bg_all)
    fin = final_norm[None, :]

    for l in range(depth):
        last = l == depth - 1
        mod_l = mod[l]
        p = {k: _Layered(v, l) for k, v in stacked.items()}
        w_in_p = tuple(_Layered(v, l) for v in w_in_all)
        nm, nf, qn, kvn, wuq, wukv, wg, bg = (p[k] for k in ('nm', 'nf', 'qn', 'kvn', 'wuq', 'wukv', 'wg', 'bg'))
        woa, wof, wor, wout = (p[k] for k in ('woa', 'wof', 'wor', 'wout'))
        wup, fcw, fcb, wdn, rcw, rcb = (p[k] for k in ('wup', 'fcw', 'fcb', 'wdn', 'rcw', 'rcb'))

        q_x, k_x, v_x, uf_x, ur_x, gug_x, g_x = _in_call(xs, mod_l, nm, w_in_p, qn, kvn, wuq, wukv, tabs_x,
                                                          seq_len, x_row(tm_in), tm_in)
        q_c, k_c, v_c, uf_c, ur_c, gug_c, g_c = _in_call(cs_tok, mod_l, nm, w_in_p, qn, kvn, wuq, wukv, tabs_c,
                                                          ctx_len, c_row, tm_c)

        att_x = _attn_call(q_x, [(k_c, v_c, ctx_len), (k_x, v_x, seq_len)], seq_len, tq)
        yf_x = _fourier_call(uf_x, cs, flip_x, dft_x, seq_len, min(512, seq_len))
        yr_c, yr_x = _rnn_call(ur_c, ur_x, gug_c, gug_x, rcw, rcb, wg, bg, p['lam'], ctx_len, seq_len)

        xs = _merge_call(xs, att_x, yf_x, yr_x, g_x, mod_l, woa, wof, wor, wout, x_row(tm_x), tm_x)
        xs = _ffn_call(xs, mod_l, nf, wup, fcw, fcb, wdn, fin, seq_len, x_row(tm_ffn), tm_ffn, last)

        if not last:
            att_c = _attn_call(q_c, [(k_c, v_c, ctx_len)], ctx_len, min(_ATTN_SUB_ROWS, ctx_len))
            yf_c = _fourier_call(uf_c, cs, flip_c, dft_c, ctx_len, ctx_len)
            cs_tok = _merge_call(cs_tok, att_c, yf_c, yr_c, g_c, mod_l, woa, wof, wor, wout, c_row, tm_c)
            cs_tok = _ffn_call(cs_tok, mod_l, nf, wup, fcw, fcb, wdn, fin, ctx_len, c_row, tm_ffn_c, False)

    return xs.reshape(batch, seq_len, d)
```
